```python
import math
import jax, jax.numpy as jnp
from jax import lax
import numpy as np

D_MODEL = 1024
BATCH = 2
SEQ = 8192
DEPTH = 2

N_META = 16
N_A_LAYERS = DEPTH // 2
N_B_LAYERS = DEPTH - N_A_LAYERS
NORM_EPS = 1e-6

GDN_QK_HEADS = 8
GDN_V_HEADS = 16
GDN_DK = 128
GDN_DV = 128
GDN_CONV = 4
GDN_CHUNK = 64
GDN_QK_W = GDN_QK_HEADS * GDN_DK
GDN_V_W = GDN_V_HEADS * GDN_DV
GDN_CONV_W = 2 * GDN_QK_W + GDN_V_W
GDN_IN_W = GDN_CONV_W + GDN_V_W + 2 * GDN_V_HEADS

MLA_HEADS = 16
MLA_NOPE = 128
MLA_ROPE = 64
MLA_V = 128
MLA_Q_RANK = 256
MLA_KV_RANK = 128
MLA_QK = MLA_NOPE + MLA_ROPE
MLA_V_W = MLA_HEADS * MLA_V
MLA_IN_W = MLA_Q_RANK + MLA_V_W
ROPE_THETA = 10000.0
Q_BLOCK = 128

kernel_name = "yoco_gdn_mla_hybrid"


def rmsnorm(x, g):
    xf = x.astype(jnp.float32)
    y = xf * lax.rsqrt(jnp.mean(xf * xf, axis=-1, keepdims=True) + NORM_EPS)
    return (y * g.astype(jnp.float32)).astype(x.dtype)


def l2norm(x):
    xf = x.astype(jnp.float32)
    return (xf * lax.rsqrt(jnp.sum(xf * xf, axis=-1, keepdims=True) + NORM_EPS)).astype(x.dtype)


def causal_depthwise_conv(x, w):
    k_len, ch = w.shape
    return lax.conv_general_dilated(x, w[:, None, :].astype(x.dtype), window_strides=(1,),
                                    padding=[(k_len - 1, 0)],
                                    dimension_numbers=('NWC', 'WIO', 'NWC'),
                                    feature_group_count=ch)


def rope_tables(length):
    inv = ROPE_THETA ** (-jnp.arange(0, MLA_ROPE, 2, dtype=jnp.float32) / MLA_ROPE)
    ang = jnp.arange(length, dtype=jnp.float32)[:, None] * inv[None, :]
    return jnp.cos(ang), jnp.sin(ang)


def apply_rope(x, cos, sin):
    xf = x.astype(jnp.float32)
    half = MLA_ROPE // 2
    x1, x2 = xf[..., :half], xf[..., half:]
    return jnp.concatenate([x1 * cos - x2 * sin, x2 * cos + x1 * sin], axis=-1).astype(x.dtype)


def gated_delta_rule_chunked(q, k, v, beta, g):
    B, L, H, dk = q.shape
    dv = v.shape[-1]
    C = GDN_CHUNK
    pad = (-L) % C
    n_chunks = (L + pad) // C

    def blocks(t):
        t = jnp.pad(t.astype(jnp.float32), [(0, 0), (pad, 0)] + [(0, 0)] * (t.ndim - 2))
        t = t.reshape((B, n_chunks, C) + t.shape[2:])
        return jnp.moveaxis(t, 3, 1)

    q, k, v, beta, g = blocks(q), blocks(k), blocks(v), blocks(beta), blocks(g)
    gc = jnp.cumsum(g, axis=-1)
    idx = jnp.arange(C)
    incl = idx[:, None] >= idx[None, :]
    strict = idx[:, None] > idx[None, :]
    decay = jnp.exp(jnp.where(incl, gc[..., :, None] - gc[..., None, :], -jnp.inf))

    kb = k * beta[..., None]
    vb = v * beta[..., None]
    m = jnp.einsum('bhnid,bhnjd->bhnij', kb, k) * jnp.where(strict, decay, 0.0)
    eye = jnp.eye(C, dtype=jnp.float32)
    rhs = jnp.concatenate([vb, kb * jnp.exp(gc)[..., None]], axis=-1)
    sol = lax.linalg.triangular_solve(m + eye, rhs, left_side=True, lower=True, unit_diagonal=True)
    u, w = sol[..., :dv], sol[..., dv:]

    attn = jnp.einsum('bhnid,bhnjd->bhnij', q, k) * decay
    q_dec = q * jnp.exp(gc)[..., None]
    k_dec = k * jnp.exp(gc[..., -1:] - gc)[..., None]
    g_last = jnp.exp(gc[..., -1])

    def step(state, xs):
        u_c, w_c, qd_c, kd_c, a_c, gl_c = xs
        v_new = u_c - jnp.einsum('bhcd,bhde->bhce', w_c, state)
        o_c = jnp.einsum('bhcd,bhde->bhce', qd_c, state) + jnp.einsum('bhij,bhje->bhie', a_c, v_new)
        state = state * gl_c[..., None, None] + jnp.einsum('bhcd,bhce->bhde', kd_c, v_new)
        return state, o_c

    xs = tuple(jnp.moveaxis(t, 2, 0) for t in (u, w, q_dec, k_dec, attn, g_last))
    s0 = jnp.zeros((B, H, dk, dv), jnp.float32)
    _, o = lax.scan(step, s0, xs)
    o = jnp.transpose(o, (1, 0, 3, 2, 4)).reshape(B, n_chunks * C, H, dv)
    return o[:, pad:]


def gdn_mixer(h, w_in, conv_w, a_log, dt_bias, out_norm, w_out):
    B, L, _ = h.shape
    proj = h @ w_in
    s1, s2, s3 = GDN_CONV_W, GDN_CONV_W + GDN_V_W, GDN_CONV_W + GDN_V_W + GDN_V_HEADS
    qkv, z, b, a = proj[..., :s1], proj[..., s1:s2], proj[..., s2:s3], proj[..., s3:]
    qkv = jax.nn.silu(causal_depthwise_conv(qkv, conv_w))
    q = l2norm(qkv[..., :GDN_QK_W].reshape(B, L, GDN_QK_HEADS, GDN_DK)) * (GDN_DK ** -0.5)
    k = l2norm(qkv[..., GDN_QK_W:2 * GDN_QK_W].reshape(B, L, GDN_QK_HEADS, GDN_DK))
    v = qkv[..., 2 * GDN_QK_W:].reshape(B, L, GDN_V_HEADS, GDN_DV)
    rep = GDN_V_HEADS // GDN_QK_HEADS
    q = jnp.repeat(q, rep, axis=2)
    k = jnp.repeat(k, rep, axis=2)
    beta = jax.nn.sigmoid(b.astype(jnp.float32))
    g = -jnp.exp(a_log.astype(jnp.float32)) * jax.nn.softplus(a.astype(jnp.float32) + dt_bias.astype(jnp.float32))
    o = gated_delta_rule_chunked(q, k, v, beta, g)
    o = rmsnorm(o, out_norm) * jax.nn.silu(z.astype(jnp.float32).reshape(B, L, GDN_V_HEADS, GDN_DV))
    return o.reshape(B, L, GDN_V_W).astype(h.dtype) @ w_out


def mla_shared_kv(h, kv_norm, kv_w_down, kv_latent_norm, kv_w_up, cos, sin):
    B, L, _ = h.shape
    ckr = rmsnorm(h, kv_norm) @ kv_w_down
    c_kv = rmsnorm(ckr[..., :MLA_KV_RANK], kv_latent_norm)
    k_rope = apply_rope(ckr[..., MLA_KV_RANK:], cos, sin)
    kv = (c_kv @ kv_w_up).reshape(B, L, MLA_HEADS, MLA_NOPE + MLA_V)
    return kv[..., :MLA_NOPE], k_rope, kv[..., MLA_NOPE:]


def causal_block_attention(q_nope, q_rope, k_nope, k_rope, v):
    B, L, H, _ = q_nope.shape
    n_blocks = -(-L // Q_BLOCK)
    pad = n_blocks * Q_BLOCK - L

    def blocks(t):
        t = jnp.pad(t, [(0, 0), (0, pad)] + [(0, 0)] * (t.ndim - 2))
        return jnp.moveaxis(t.reshape((B, n_blocks, Q_BLOCK) + t.shape[2:]), 1, 0)

    scale = MLA_QK ** -0.5
    k_pos = jnp.arange(L)

    def one_block(args):
        qn_b, qr_b, blk = args
        s = (jnp.einsum('bqhd,bkhd->bhqk', qn_b, k_nope, preferred_element_type=jnp.float32)
             + jnp.einsum('bqhr,bkr->bhqk', qr_b, k_rope, preferred_element_type=jnp.float32))
        q_pos = blk * Q_BLOCK + jnp.arange(Q_BLOCK)
        s = jnp.where(q_pos[:, None] >= k_pos[None, :], s * scale, -jnp.inf)
        p = jax.nn.softmax(s, axis=-1)
        return jnp.einsum('bhqk,bkhd->bqhd', p.astype(v.dtype), v)

    o = lax.map(one_block, (blocks(q_nope), blocks(q_rope), jnp.arange(n_blocks)))
    return jnp.moveaxis(o, 0, 1).reshape(B, n_blocks * Q_BLOCK, H, MLA_V)[:, :L]


def mla_mixer(h, w_in, q_latent_norm, w_q_up, w_out, k_nope, k_rope, v, cos, sin):
    B, L, _ = h.shape
    proj = h @ w_in
    c_q = rmsnorm(proj[..., :MLA_Q_RANK], q_latent_norm)
    z = proj[..., MLA_Q_RANK:]
    q = (c_q @ w_q_up).reshape(B, L, MLA_HEADS, MLA_QK)
    q_nope = q[..., :MLA_NOPE]
    q_rope = apply_rope(q[..., MLA_NOPE:], cos[:, None, :], sin[:, None, :])
    o = causal_block_attention(q_nope, q_rope, k_nope, k_rope, v).reshape(B, L, MLA_V_W)
    return (o * jax.nn.silu(z)) @ w_out


def setup_inputs(seed: int = 0) -> dict:
    key = jax.random.key(seed)
    ks = jax.random.split(key, 20)
    nrm = lambda k, shape, s: jax.random.normal(k, shape, jnp.float32) * s
    gain = lambda k, shape: 1.0 + 0.02 * jax.random.normal(k, shape, jnp.float32)
    dt = jnp.exp(jax.random.uniform(ks[5], (N_A_LAYERS, GDN_V_HEADS), jnp.float32,
                                    math.log(1e-3), math.log(1e-1)))
    return {
        "x": nrm(ks[0], (BATCH, SEQ, D_MODEL), 1.0),
        "meta_tokens": nrm(ks[1], (N_META, D_MODEL), 1.0),
        "pre_norm": gain(ks[2], (DEPTH, D_MODEL)),
        "post_norm": gain(ks[3], (DEPTH, D_MODEL)),
        "gdn_w_in": nrm(ks[4], (N_A_LAYERS, D_MODEL, GDN_IN_W), D_MODEL ** -0.5),
        "gdn_conv_w": nrm(ks[6], (N_A_LAYERS, GDN_CONV, GDN_CONV_W), GDN_CONV ** -0.5),
        "gdn_a_log": jnp.log(jax.random.uniform(ks[7], (N_A_LAYERS, GDN_V_HEADS), jnp.float32, 1.0, 16.0)),
        "gdn_dt_bias": dt + jnp.log(-jnp.expm1(-dt)),
        "gdn_out_norm": gain(ks[8], (N_A_LAYERS, GDN_DV)),
        "gdn_w_out": nrm(ks[9], (N_A_LAYERS, GDN_V_W, D_MODEL), GDN_V_W ** -0.5),
        "kv_norm": gain(ks[10], (D_MODEL,)),
        "kv_w_down": nrm(ks[11], (D_MODEL, MLA_KV_RANK + MLA_ROPE), D_MODEL ** -0.5),
        "kv_latent_norm": gain(ks[12], (MLA_KV_RANK,)),
        "kv_w_up": nrm(ks[13], (MLA_KV_RANK, MLA_HEADS * (MLA_NOPE + MLA_V)), MLA_KV_RANK ** -0.5),
        "mla_w_in": nrm(ks[14], (N_B_LAYERS, D_MODEL, MLA_IN_W), D_MODEL ** -0.5),
        "mla_q_latent_norm": gain(ks[15], (N_B_LAYERS, MLA_Q_RANK)),
        "mla_w_q_up": nrm(ks[16], (N_B_LAYERS, MLA_Q_RANK, MLA_HEADS * MLA_QK), MLA_Q_RANK ** -0.5),
        "mla_w_out": nrm(ks[17], (N_B_LAYERS, MLA_V_W, D_MODEL), MLA_V_W ** -0.5),
    }


def reference(x, meta_tokens, pre_norm, post_norm, gdn_w_in, gdn_conv_w, gdn_a_log, gdn_dt_bias,
              gdn_out_norm, gdn_w_out, kv_norm, kv_w_down, kv_latent_norm, kv_w_up,
              mla_w_in, mla_q_latent_norm, mla_w_q_up, mla_w_out):
    B = x.shape[0]
    meta = jnp.broadcast_to(meta_tokens[None].astype(x.dtype), (B, N_META, D_MODEL))
    h = jnp.concatenate([meta, x], axis=1)
    cos, sin = rope_tables(h.shape[1])
    shared_kv = None
    for layer in range(DEPTH):
        hn = rmsnorm(h, pre_norm[layer])
        if layer < N_A_LAYERS:
            y = gdn_mixer(hn, gdn_w_in[layer], gdn_conv_w[layer], gdn_a_log[layer], gdn_dt_bias[layer],
                          gdn_out_norm[layer], gdn_w_out[layer])
        else:
            if layer == N_A_LAYERS:
                shared_kv = mla_shared_kv(h, kv_norm, kv_w_down, kv_latent_norm, kv_w_up, cos, sin)
            j = layer - N_A_LAYERS
            k_nope, k_rope, v = shared_kv
            y = mla_mixer(hn, mla_w_in[j], mla_q_latent_norm[j], mla_w_q_up[j], mla_w_out[j],
                          k_nope, k_rope, v, cos, sin)
        h = h + rmsnorm(y, post_norm[layer])
    return h[:, N_META:]
```

```python
import functools
import math

import jax
import jax.numpy as jnp
from jax import lax
from jax.experimental import pallas as pl
from jax.experimental.pallas import tpu as pltpu

NORM_EPS = 1e-6
N_META_ROWS = 16

GDN_QK_HEADS = 8
GDN_V_HEADS = 16
GDN_HEAD = 128
GDN_CONV_TAPS = 4
GDN_CHUNK = 64
GDN_QK_W = GDN_QK_HEADS * GDN_HEAD
GDN_V_W = GDN_V_HEADS * GDN_HEAD
GDN_CONV_W = 2 * GDN_QK_W + GDN_V_W

MLA_HEADS = 16
MLA_NOPE = 128
MLA_ROPE = 64
MLA_V = 128
MLA_Q_RANK = 256
MLA_KV_RANK = 128
MLA_QK = MLA_NOPE + MLA_ROPE
ROPE_THETA = 10000.0

LANES = 128
ROW_TILE = 256
FRONT_PAD = ROW_TILE - N_META_ROWS
CONV_COLS = 512
HALO_ROWS = 8
ATTN_TQ = 512
ATTN_TK = 256
VMEM_LIMIT = 56 * 1024 * 1024

F32 = jnp.float32
BF16 = jnp.bfloat16


def _dot(a, b):
    return jnp.dot(a, b, preferred_element_type=F32)


def _dot_nt(a, b):
    return lax.dot_general(a, b, (((1,), (1,)), ((), ())), preferred_element_type=F32)


def _dot_tn(a, b):
    return lax.dot_general(a, b, (((0,), (0,)), ((), ())), preferred_element_type=F32)


def _dot_exact(a, b):
    return jnp.dot(a, b, preferred_element_type=F32, precision=lax.Precision.HIGHEST)


def _silu(x):
    return x / (1.0 + jnp.exp(-x))


def _softplus(x):
    return jnp.maximum(x, 0.0) + jnp.log1p(jnp.exp(-jnp.abs(x)))


def _rms_scale(x):
    return lax.rsqrt(jnp.mean(x * x, axis=-1, keepdims=True) + NORM_EPS)


def _const_spec(shape):
    nd = len(shape)
    return pl.BlockSpec(shape, lambda *_: (0,) * nd, pipeline_mode=pl.Buffered(1))


def _gdn_in_kernel(x_ref, head_ref, gain_ref, wqkv_ref, wz_ref, wba_ref, wbat_ref, convw_ref,
                   arow_ref, dtrow_ref, acol_ref, dtcol_ref,
                   q_ref, k_ref, v_ref, zs_ref, gcol_ref, grow_ref,
                   halo_scr, buf_scr):
    i = pl.program_id(1)
    x = jnp.where(i == 0, head_ref[...], x_ref[0])
    hn = (x * _rms_scale(x) * gain_ref[...]).astype(BF16)

    @pl.when(i == 0)
    def _():
        halo_scr[...] = jnp.zeros_like(halo_scr)

    lo = HALO_ROWS - (GDN_CONV_TAPS - 1)
    for c in range(GDN_CONV_W // CONV_COLS):
        cs = slice(c * CONV_COLS, (c + 1) * CONV_COLS)
        p = _dot(hn, wqkv_ref[:, cs])
        buf_scr[0:HALO_ROWS, :] = halo_scr[:, cs]
        buf_scr[HALO_ROWS:HALO_ROWS + ROW_TILE, :] = p
        halo_scr[:, cs] = p[ROW_TILE - HALO_ROWS:, :]
        y = convw_ref[0:1, cs] * buf_scr[lo:lo + ROW_TILE, :]
        for j in range(1, GDN_CONV_TAPS):
            y = y + convw_ref[j:j + 1, cs] * buf_scr[lo + j:lo + j + ROW_TILE, :]
        y = _silu(y)
        for hh in range(CONV_COLS // GDN_HEAD):
            col = c * CONV_COLS + hh * GDN_HEAD
            yh = y[:, hh * GDN_HEAD:(hh + 1) * GDN_HEAD]
            if col < 2 * GDN_QK_W:
                r = lax.rsqrt(jnp.sum(yh * yh, axis=-1, keepdims=True) + NORM_EPS)
                if col < GDN_QK_W:
                    q_ref[0, :, col:col + GDN_HEAD] = (yh * r * (GDN_HEAD ** -0.5)).astype(BF16)
                else:
                    k_ref[0, :, col - GDN_QK_W:col - GDN_QK_W + GDN_HEAD] = (yh * r).astype(BF16)
            else:
                v_ref[0, :, col - 2 * GDN_QK_W:col - 2 * GDN_QK_W + GDN_HEAD] = yh.astype(BF16)

    for c in range(GDN_V_W // CONV_COLS):
        cs = slice(c * CONV_COLS, (c + 1) * CONV_COLS)
        zs_ref[0, :, cs] = _silu(_dot(hn, wz_ref[:, cs])).astype(BF16)

    nh = GDN_V_HEADS
    ba = _dot(hn, wba_ref[...])
    bat = _dot_nt(wbat_ref[...], hn)
    beta_c = 1.0 / (1.0 + jnp.exp(-ba))
    g_c = -jnp.exp(arow_ref[...]) * _softplus(ba + dtrow_ref[...])
    beta_r = 1.0 / (1.0 + jnp.exp(-bat[0:nh, :]))
    g_r = -jnp.exp(acol_ref[...]) * _softplus(bat[nh:2 * nh, :] + dtcol_ref[...])

    ri = lax.broadcasted_iota(jnp.int32, (ROW_TILE, ROW_TILE), 0)
    ci = lax.broadcasted_iota(jnp.int32, (ROW_TILE, ROW_TILE), 1)
    same = (ri // GDN_CHUNK) == (ci // GDN_CHUNK)
    lower = jnp.where(same & (ri >= ci), 1.0, 0.0).astype(F32)
    upper = jnp.where(same & (ri <= ci), 1.0, 0.0).astype(F32)
    block = jnp.where(same, 1.0, 0.0).astype(F32)
    gc_c = _dot_exact(lower, g_c)
    gl_c = _dot_exact(block, g_c)
    gc_r = _dot_exact(g_r, upper)
    grp = lax.broadcasted_iota(jnp.int32, (ROW_TILE, LANES), 1) // nh
    gcol_ref[0] = jnp.where(grp == 0, beta_c,
                            jnp.where(grp == 1, gc_c,
                                      jnp.where(grp == 2, jnp.exp(gc_c),
                                                jnp.where(grp == 3, jnp.exp(gl_c - gc_c), 0.0))))
    rows = jnp.concatenate([gc_r, beta_r, beta_r * jnp.exp(gc_r)], axis=0)
    for c in range(ROW_TILE // GDN_CHUNK):
        grow_ref[0, c] = rows[:, c * GDN_CHUNK:(c + 1) * GDN_CHUNK]


def _unit_lower_inverse(m, masks):
    eye, diag8, merge_masks = masks
    m8 = jnp.where(diag8, m, 0.0)
    m8b = m8.astype(BF16)
    q2 = _dot(m8b, m8b)
    p = eye - m8
    q2b = q2.astype(BF16)
    p = p + _dot(p.astype(BF16), q2b)
    q4 = _dot(q2b, q2b)
    p = p + _dot(p.astype(BF16), q4.astype(BF16))
    for mask in merge_masks:
        off = jnp.where(mask, m, 0.0).astype(BF16)
        pb = p.astype(BF16)
        p = p - _dot(pb, _dot(off, pb).astype(BF16))
    return p


def _gdn_chunk_kernel(q_ref, k_ref, v_ref, zs_ref, gcol_ref, grow_ref, onorm_ref, o_ref, state_scr):
    @pl.when(pl.program_id(1) == 0)
    def _():
        state_scr[...] = jnp.zeros_like(state_scr)

    C = GDN_CHUNK
    ri = lax.broadcasted_iota(jnp.int32, (C, C), 0)
    ci = lax.broadcasted_iota(jnp.int32, (C, C), 1)
    incl = ri >= ci
    strict = ri > ci
    eye = jnp.where(ri == ci, 1.0, 0.0).astype(F32)
    diag8 = strict & ((ri // 8) == (ci // 8))
    merges = tuple(strict & ((ri // (2 * s)) == (ci // (2 * s))) & ((ri // s) != (ci // s))
                   for s in (8, 16, 32))
    masks = (eye, diag8, merges)
    nh = GDN_V_HEADS
    rep = GDN_V_HEADS // GDN_QK_HEADS
    gcol = gcol_ref[0]
    grow = grow_ref[0, 0]
    onorm = onorm_ref[...]

    for p_ in range(GDN_QK_HEADS):
        ps = slice(p_ * GDN_HEAD, (p_ + 1) * GDN_HEAD)
        qp = q_ref[0, :, ps]
        kp = k_ref[0, :, ps]
        kk = _dot_nt(kp, kp)
        qk = _dot_nt(qp, kp)
        kpf = kp.astype(F32)
        for h in range(p_ * rep, (p_ + 1) * rep):
            hs = slice(h * GDN_HEAD, (h + 1) * GDN_HEAD)
            beta_c = gcol[:, h:h + 1]
            gc_c = gcol[:, nh + h:nh + h + 1]
            eg_c = gcol[:, 2 * nh + h:2 * nh + h + 1]
            egl_c = gcol[:, 3 * nh + h:3 * nh + h + 1]
            gc_r = grow[h:h + 1, :]
            beta_r = grow[nh + h:nh + h + 1, :]
            bege_r = grow[2 * nh + h:2 * nh + h + 1, :]

            decay = jnp.exp(jnp.where(incl, gc_c - gc_r, -jnp.inf))
            m = jnp.where(strict, beta_c * kk * decay, 0.0)
            t = _unit_lower_inverse(m, masks)
            u = _dot((t * beta_r).astype(BF16), v_ref[0, :, hs])
            w = _dot((t * bege_r).astype(BF16), kp)
            s_old = state_scr[h]
            sb = s_old.astype(BF16)
            v_new = u - _dot(w.astype(BF16), sb)
            vnb = v_new.astype(BF16)
            o = eg_c * _dot(qp, sb) + _dot((qk * decay).astype(BF16), vnb)
            kd = (kpf * egl_c).astype(BF16)
            g_last = eg_c[C - 1:C, :]
            state_scr[h] = s_old * g_last + _dot_tn(kd, vnb)
            on = o * _rms_scale(o) * onorm
            o_ref[0, :, hs] = (on * zs_ref[0, :, hs].astype(F32)).astype(BF16)


def _out_proj_first_kernel(o_ref, w_ref, gain_ref, x_ref, head_ref, h_ref):
    y = _dot(o_ref[0], w_ref[...])
    res = jnp.where(pl.program_id(1) == 0, head_ref[...], x_ref[0])
    h_ref[0] = res + y * _rms_scale(y) * gain_ref[...]


def _out_proj_last_kernel(o_ref, w_ref, gain_ref, h_ref, out_ref):
    y = _dot(o_ref[0], w_ref[...])
    out_ref[0] = h_ref[0] + y * _rms_scale(y) * gain_ref[...]


def _mla_in_kernel(h_ref, pre_ref, kvn_ref, wcq_ref, wz_ref, qln_ref, wqn_ref, wqr_ref, wqrr_ref,
                   wkd_ref, kvln_ref, wkvu_ref, cos_ref, sin_ref,
                   q_ref, kk_ref, vv_ref, zs_ref):
    h = h_ref[0]
    hr = h * _rms_scale(h)
    hn = (hr * pre_ref[...]).astype(BF16)
    hk = (hr * kvn_ref[...]).astype(BF16)
    cosp = cos_ref[...]
    sinp = sin_ref[...]

    for c in range(wz_ref.shape[1] // CONV_COLS):
        cs = slice(c * CONV_COLS, (c + 1) * CONV_COLS)
        zs_ref[0, :, cs] = _silu(_dot(hn, wz_ref[:, cs])).astype(BF16)

    cq = _dot(hn, wcq_ref[...])
    cq = (cq * _rms_scale(cq) * qln_ref[...]).astype(BF16)
    scale = MLA_QK ** -0.5
    for hd in range(MLA_HEADS):
        hs = slice(hd * LANES, (hd + 1) * LANES)
        qn = _dot(cq, wqn_ref[:, hs])
        qr = _dot(cq, wqr_ref[:, hs]) * cosp + _dot(cq, wqrr_ref[:, hs]) * sinp
        q_ref[0, hd, :, 0:MLA_NOPE] = (qn * scale).astype(BF16)
        q_ref[0, hd, :, MLA_NOPE:MLA_QK] = (qr[:, 0:MLA_ROPE] * scale).astype(BF16)

    ckr = _dot(hk, wkd_ref[...])
    lat = ckr[:, 0:MLA_KV_RANK]
    ckv = (lat * _rms_scale(lat) * kvln_ref[...]).astype(BF16)
    kr = ckr[:, LANES:2 * LANES] * cosp + ckr[:, 2 * LANES:3 * LANES] * sinp
    krb = kr[:, 0:MLA_ROPE].astype(BF16)
    for hd in range(MLA_HEADS):
        kn = _dot(ckv, wkvu_ref[:, hd * MLA_NOPE:(hd + 1) * MLA_NOPE])
        vv = _dot(ckv, wkvu_ref[:, MLA_HEADS * MLA_NOPE + hd * MLA_V:MLA_HEADS * MLA_NOPE + (hd + 1) * MLA_V])
        kk_ref[0, hd, :, 0:MLA_NOPE] = kn.astype(BF16)
        kk_ref[0, hd, :, MLA_NOPE:MLA_QK] = krb
        vv_ref[0, hd] = vv.astype(BF16)


def _attn_kernel(q_ref, k_ref, v_ref, zs_ref, o_ref, m_scr, l_scr, acc_scr):
    i = pl.program_id(2)
    tq, tk = ATTN_TQ, ATTN_TK
    sub = tq // tk
    q_row0 = ROW_TILE + i * tq
    m_scr[...] = jnp.full_like(m_scr, -jnp.inf)
    l_scr[...] = jnp.zeros_like(l_scr)
    acc_scr[...] = jnp.zeros_like(acc_scr)

    def tile(j, r0, nrows, mask_fn):
        rs = slice(r0, r0 + nrows)
        k0 = j * tk if isinstance(j, int) else pl.multiple_of(j * tk, tk)
        kt = k_ref[0, 0, pl.ds(k0, tk), :]
        vt = v_ref[0, 0, pl.ds(k0, tk), :]
        s = _dot_nt(q_ref[0, 0, rs, :], kt)
        if mask_fn is not None:
            qpos = q_row0 + r0 + lax.broadcasted_iota(jnp.int32, (nrows, tk), 0)
            kpos = j * tk + lax.broadcasted_iota(jnp.int32, (nrows, tk), 1)
            s = jnp.where(mask_fn(qpos, kpos), s, -jnp.inf)
        m_old = m_scr[rs, :]
        m_new = jnp.maximum(m_old, jnp.max(s, axis=-1, keepdims=True))
        alpha = jnp.exp(m_old - m_new)
        p = jnp.exp(s - m_new)
        l_scr[rs, :] = alpha * l_scr[rs, :] + jnp.sum(p, axis=-1, keepdims=True)
        acc_scr[rs, :] = alpha * acc_scr[rs, :] + _dot(p.astype(BF16), vt)
        m_scr[rs, :] = m_new

    tile(0, 0, tq, lambda qpos, kpos: kpos >= FRONT_PAD)
    n_full = ROW_TILE // tk + i * sub

    def body(j, carry):
        tile(j, 0, tq, None)
        return carry

    lax.fori_loop(1, n_full, body, 0)
    for d in range(sub):
        tile(n_full + d, d * tk, tq - d * tk, lambda qpos, kpos: kpos <= qpos)

    o = acc_scr[...] / l_scr[...]
    o_ref[0] = (o * zs_ref[0].astype(F32)).astype(BF16)


def _params(*sem):
    return pltpu.CompilerParams(dimension_semantics=sem, vmem_limit_bytes=VMEM_LIMIT)


def kernel(x, meta_tokens, pre_norm, post_norm, gdn_w_in, gdn_conv_w, gdn_a_log, gdn_dt_bias, gdn_out_norm, gdn_w_out, kv_norm, kv_w_down, kv_latent_norm, kv_w_up, mla_w_in, mla_q_latent_norm, mla_w_q_up, mla_w_out):
    B, S, D = x.shape
    assert S % ATTN_TQ == 0 and S % ROW_TILE == 0
    assert gdn_w_in.shape[0] == 1 and mla_w_in.shape[0] == 1
    T = ROW_TILE
    nt = S // T + 1
    Lp = nt * T
    nch = Lp // GDN_CHUNK
    nh = GDN_V_HEADS

    head = jnp.concatenate([jnp.zeros((FRONT_PAD, D), F32), meta_tokens.astype(F32)], axis=0)
    row = lambda a: a.reshape(1, -1).astype(F32)

    w_in = gdn_w_in[0]
    w_qkv = w_in[:, :GDN_CONV_W].astype(BF16)
    w_z = w_in[:, GDN_CONV_W:GDN_CONV_W + GDN_V_W].astype(BF16)
    w_ba_f = w_in[:, GDN_CONV_W + GDN_V_W:]
    w_b, w_a = w_ba_f[:, :nh], w_ba_f[:, nh:]
    w_ba = jnp.concatenate([w_b, w_a, w_a, w_a, jnp.zeros((D, LANES - 4 * nh), F32)], axis=-1).astype(BF16)
    w_bat = w_ba_f.T.astype(BF16)
    a_log = gdn_a_log[0].astype(F32)
    dt_b = gdn_dt_bias[0].astype(F32)
    lane_groups = lambda a: jnp.concatenate([jnp.zeros((nh,), F32), a, a, a,
                                             jnp.zeros((LANES - 4 * nh,), F32)]).reshape(1, LANES)

    x_spec = pl.BlockSpec((1, T, D), lambda b, i: (b, jnp.maximum(i - 1, 0), 0))
    tile_spec = lambda w: pl.BlockSpec((1, T, w), lambda b, i: (b, i, 0))

    q, k, v, zs, gcol, grow = pl.pallas_call(
        _gdn_in_kernel,
        grid=(B, nt),
        in_specs=[x_spec, _const_spec((T, D)), _const_spec((1, D)),
                  _const_spec((D, GDN_CONV_W)), _const_spec((D, GDN_V_W)),
                  _const_spec((D, LANES)), _const_spec((2 * nh, D)),
                  _const_spec((GDN_CONV_TAPS, GDN_CONV_W)),
                  _const_spec((1, LANES)), _const_spec((1, LANES)), _const_spec((nh, 1)), _const_spec((nh, 1))],
        out_specs=[tile_spec(GDN_QK_W), tile_spec(GDN_QK_W), tile_spec(GDN_V_W), tile_spec(GDN_V_W),
                   tile_spec(LANES),
                   pl.BlockSpec((1, T // GDN_CHUNK, 3 * nh, GDN_CHUNK), lambda b, i: (b, i, 0, 0))],
        out_shape=[jax.ShapeDtypeStruct((B, Lp, GDN_QK_W), BF16),
                   jax.ShapeDtypeStruct((B, Lp, GDN_QK_W), BF16),
                   jax.ShapeDtypeStruct((B, Lp, GDN_V_W), BF16),
                   jax.ShapeDtypeStruct((B, Lp, GDN_V_W), BF16),
                   jax.ShapeDtypeStruct((B, Lp, LANES), F32),
                   jax.ShapeDtypeStruct((B, nch, 3 * nh, GDN_CHUNK), F32)],
        scratch_shapes=[pltpu.VMEM((HALO_ROWS, GDN_CONV_W), F32),
                        pltpu.VMEM((HALO_ROWS + T, CONV_COLS), F32)],
        compiler_params=_params("arbitrary", "arbitrary"),
        name="gdn_in",
    )(x, head, row(pre_norm[0]), w_qkv, w_z, w_ba, w_bat, gdn_conv_w[0].astype(F32),
      lane_groups(a_log), lane_groups(dt_b), a_log.reshape(nh, 1), dt_b.reshape(nh, 1))

    chunk_spec = lambda w: pl.BlockSpec((1, GDN_CHUNK, w), lambda b, n: (b, n, 0))
    o_gdn = pl.pallas_call(
        _gdn_chunk_kernel,
        grid=(B, nch),
        in_specs=[chunk_spec(GDN_QK_W), chunk_spec(GDN_QK_W), chunk_spec(GDN_V_W), chunk_spec(GDN_V_W),
                  chunk_spec(LANES),
                  pl.BlockSpec((1, 1, 3 * nh, GDN_CHUNK), lambda b, n: (b, n, 0, 0)),
                  _const_spec((1, GDN_HEAD))],
        out_specs=chunk_spec(GDN_V_W),
        out_shape=jax.ShapeDtypeStruct((B, Lp, GDN_V_W), BF16),
        scratch_shapes=[pltpu.VMEM((nh, GDN_HEAD, GDN_HEAD), F32)],
        compiler_params=_params("arbitrary", "arbitrary"),
        name="gdn_chunk",
    )(q, k, v, zs, gcol, grow, row(gdn_out_norm[0]))

    h1 = pl.pallas_call(
        _out_proj_first_kernel,
        grid=(B, nt),
        in_specs=[tile_spec(GDN_V_W), _const_spec((GDN_V_W, D)), _const_spec((1, D)),
                  x_spec, _const_spec((T, D))],
        out_specs=tile_spec(D),
        out_shape=jax.ShapeDtypeStruct((B, Lp, D), F32),
        compiler_params=_params("parallel", "arbitrary"),
        name="gdn_out",
    )(o_gdn, gdn_w_out[0].astype(BF16), row(post_norm[0]), x, head)

    w_in1 = mla_w_in[0]
    w_cq = w_in1[:, :MLA_Q_RANK].astype(BF16)
    w_z1 = w_in1[:, MLA_Q_RANK:].astype(BF16)
    zw = w_z1.shape[1]
    half = MLA_ROPE // 2
    rot = lambda w: jnp.concatenate([-w[..., half:], w[..., :half]], axis=-1)
    lane_pad = lambda w: jnp.pad(w, [(0, 0)] * (w.ndim - 1) + [(0, LANES - w.shape[-1])])
    wq = mla_w_q_up[0].reshape(MLA_Q_RANK, MLA_HEADS, MLA_QK)
    w_qn = wq[..., :MLA_NOPE].reshape(MLA_Q_RANK, MLA_HEADS * MLA_NOPE).astype(BF16)
    w_qr = lane_pad(wq[..., MLA_NOPE:]).reshape(MLA_Q_RANK, MLA_HEADS * LANES).astype(BF16)
    w_qrr = lane_pad(rot(wq[..., MLA_NOPE:])).reshape(MLA_Q_RANK, MLA_HEADS * LANES).astype(BF16)
    wkd_r = kv_w_down[:, MLA_KV_RANK:]
    w_kd = jnp.concatenate([kv_w_down[:, :MLA_KV_RANK], lane_pad(wkd_r), lane_pad(rot(wkd_r))],
                           axis=-1).astype(BF16)
    wku = kv_w_up.reshape(MLA_KV_RANK, MLA_HEADS, MLA_NOPE + MLA_V)
    w_kvu = jnp.concatenate([wku[..., :MLA_NOPE].reshape(MLA_KV_RANK, -1),
                             wku[..., MLA_NOPE:].reshape(MLA_KV_RANK, -1)], axis=-1).astype(BF16)

    inv = ROPE_THETA ** (-jnp.arange(0, MLA_ROPE, 2, dtype=F32) / MLA_ROPE)
    pos = (jnp.arange(Lp, dtype=jnp.int32) - FRONT_PAD).astype(F32)
    ang = pos[:, None] * inv[None, :]
    zpad = jnp.zeros((Lp, LANES - MLA_ROPE), F32)
    cosp = jnp.concatenate([jnp.cos(ang), jnp.cos(ang), zpad], axis=-1)
    sinp = jnp.concatenate([jnp.sin(ang), jnp.sin(ang), zpad], axis=-1)

    head_tile = lambda w: pl.BlockSpec((1, MLA_HEADS, T, w), lambda b, i: (b, 0, i, 0))
    q1, k1, v1, zs1 = pl.pallas_call(
        _mla_in_kernel,
        grid=(B, nt),
        in_specs=[tile_spec(D), _const_spec((1, D)), _const_spec((1, D)),
                  _const_spec((D, MLA_Q_RANK)), _const_spec((D, zw)), _const_spec((1, MLA_Q_RANK)),
                  _const_spec((MLA_Q_RANK, MLA_HEADS * MLA_NOPE)),
                  _const_spec((MLA_Q_RANK, MLA_HEADS * LANES)),
                  _const_spec((MLA_Q_RANK, MLA_HEADS * LANES)),
                  _const_spec((D, 3 * LANES)), _const_spec((1, MLA_KV_RANK)),
                  _const_spec((MLA_KV_RANK, MLA_HEADS * (MLA_NOPE + MLA_V))),
                  pl.BlockSpec((T, LANES), lambda b, i: (i, 0)),
                  pl.BlockSpec((T, LANES), lambda b, i: (i, 0))],
        out_specs=[pl.BlockSpec((1, MLA_HEADS, T, MLA_QK), lambda b, i: (b, 0, jnp.maximum(i - 1, 0), 0)),
                   head_tile(MLA_QK), head_tile(MLA_V),
                   pl.BlockSpec((1, T, zw), lambda b, i: (b, jnp.maximum(i - 1, 0), 0))],
        out_shape=[jax.ShapeDtypeStruct((B, MLA_HEADS, S, MLA_QK), BF16),
                   jax.ShapeDtypeStruct((B, MLA_HEADS, Lp, MLA_QK), BF16),
                   jax.ShapeDtypeStruct((B, MLA_HEADS, Lp, MLA_V), BF16),
                   jax.ShapeDtypeStruct((B, S, zw), BF16)],
        compiler_params=_params("arbitrary", "arbitrary"),
        name="mla_in",
    )(h1, row(pre_norm[1]), row(kv_norm), w_cq, w_z1, row(mla_q_latent_norm[0]), w_qn, w_qr, w_qrr,
      w_kd, row(kv_latent_norm), w_kvu, cosp, sinp)

    o_attn = pl.pallas_call(
        _attn_kernel,
        grid=(B, MLA_HEADS, S // ATTN_TQ),
        in_specs=[pl.BlockSpec((1, 1, ATTN_TQ, MLA_QK), lambda b, h, i: (b, h, i, 0)),
                  pl.BlockSpec((1, 1, Lp, MLA_QK), lambda b, h, i: (b, h, 0, 0)),
                  pl.BlockSpec((1, 1, Lp, MLA_V), lambda b, h, i: (b, h, 0, 0)),
                  pl.BlockSpec((1, ATTN_TQ, MLA_V), lambda b, h, i: (b, i, h))],
        out_specs=pl.BlockSpec((1, ATTN_TQ, MLA_V), lambda b, h, i: (b, i, h)),
        out_shape=jax.ShapeDtypeStruct((B, S, MLA_HEADS * MLA_V), BF16),
        scratch_shapes=[pltpu.VMEM((ATTN_TQ, 1), F32), pltpu.VMEM((ATTN_TQ, 1), F32),
                        pltpu.VMEM((ATTN_TQ, MLA_V), F32)],
        compiler_params=_params("parallel", "parallel", "arbitrary"),
        name="mla_attn",
    )(q1, k1, v1, zs1)

    out = pl.pallas_call(
        _out_proj_last_kernel,
        grid=(B, S // T),
        in_specs=[pl.BlockSpec((1, T, MLA_HEADS * MLA_V), lambda b, i: (b, i, 0)),
                  _const_spec((MLA_HEADS * MLA_V, D)), _const_spec((1, D)),
                  pl.BlockSpec((1, T, D), lambda b, i: (b, i + 1, 0))],
        out_specs=pl.BlockSpec((1, T, D), lambda b, i: (b, i, 0)),
        out_shape=jax.ShapeDtypeStruct((B, S, D), x.dtype),
        compiler_params=_params("parallel", "arbitrary"),
        name="mla_out",
    )(o_attn, mla_w_out[0].astype(BF16), row(post_norm[1]), h1)
    return out
```

```python
import functools
import math

import jax
import jax.numpy as jnp
from jax import lax
from jax.experimental import pallas as pl
from jax.experimental.pallas import tpu as pltpu

NORM_EPS = 1e-6
N_META_ROWS = 16

GDN_QK_HEADS = 8
GDN_V_HEADS = 16
GDN_HEAD = 128
GDN_CONV_TAPS = 4
GDN_CHUNK = 64
GDN_QK_W = GDN_QK_HEADS * GDN_HEAD
GDN_V_W = GDN_V_HEADS * GDN_HEAD
GDN_CONV_W = 2 * GDN_QK_W + GDN_V_W

MLA_HEADS = 16
MLA_NOPE = 128
MLA_ROPE = 64
MLA_V = 128
MLA_Q_RANK = 256
MLA_KV_RANK = 128
MLA_QK = MLA_NOPE + MLA_ROPE
ROPE_THETA = 10000.0

LANES = 128
ROW_TILE = 256
FRONT_PAD = ROW_TILE - N_META_ROWS
CONV_COLS = 512
HALO_ROWS = 8
ATTN_TQ = 1024
ATTN_SUB = 256
ATTN_TK = 512
VMEM_LIMIT = 56 * 1024 * 1024

F32 = jnp.float32
BF16 = jnp.bfloat16


def _dot(a, b):
    return jnp.dot(a, b, preferred_element_type=F32)


def _dot_nt(a, b):
    return lax.dot_general(a, b, (((1,), (1,)), ((), ())), preferred_element_type=F32)


def _dot_tn(a, b):
    return lax.dot_general(a, b, (((0,), (0,)), ((), ())), preferred_element_type=F32)


def _dot_exact(a, b):
    return jnp.dot(a, b, preferred_element_type=F32, precision=lax.Precision.HIGHEST)


def _silu(x):
    return x / (1.0 + jnp.exp(-x))


def _softplus(x):
    return jnp.maximum(x, 0.0) + jnp.log1p(jnp.exp(-jnp.abs(x)))


def _rms_scale(x):
    return lax.rsqrt(jnp.mean(x * x, axis=-1, keepdims=True) + NORM_EPS)


def _const_spec(shape):
    nd = len(shape)
    return pl.BlockSpec(shape, lambda *_: (0,) * nd, pipeline_mode=pl.Buffered(1))


def _gdn_in_kernel(x_ref, head_ref, gain_ref, wqkv_ref, wz_ref, wba_ref, wbat_ref, convw_ref,
                   arow_ref, dtrow_ref, acol_ref, dtcol_ref,
                   q_ref, k_ref, v_ref, zs_ref, gcol_ref, grow_ref,
                   halo_scr, buf_scr):
    i = pl.program_id(1)
    x = jnp.where(i == 0, head_ref[...], x_ref[0])
    hn = (x * _rms_scale(x) * gain_ref[...]).astype(BF16)

    @pl.when(i == 0)
    def _():
        halo_scr[...] = jnp.zeros_like(halo_scr)

    lo = HALO_ROWS - (GDN_CONV_TAPS - 1)
    for c in range(GDN_CONV_W // CONV_COLS):
        cs = slice(c * CONV_COLS, (c + 1) * CONV_COLS)
        p = _dot(hn, wqkv_ref[:, cs])
        buf_scr[0:HALO_ROWS, :] = halo_scr[:, cs]
        buf_scr[HALO_ROWS:HALO_ROWS + ROW_TILE, :] = p
        halo_scr[:, cs] = p[ROW_TILE - HALO_ROWS:, :]
        y = convw_ref[0:1, cs] * buf_scr[lo:lo + ROW_TILE, :]
        for j in range(1, GDN_CONV_TAPS):
            y = y + convw_ref[j:j + 1, cs] * buf_scr[lo + j:lo + j + ROW_TILE, :]
        y = _silu(y)
        for hh in range(CONV_COLS // GDN_HEAD):
            col = c * CONV_COLS + hh * GDN_HEAD
            yh = y[:, hh * GDN_HEAD:(hh + 1) * GDN_HEAD]
            if col < 2 * GDN_QK_W:
                r = lax.rsqrt(jnp.sum(yh * yh, axis=-1, keepdims=True) + NORM_EPS)
                if col < GDN_QK_W:
                    q_ref[0, :, col:col + GDN_HEAD] = (yh * r * (GDN_HEAD ** -0.5)).astype(BF16)
                else:
                    k_ref[0, :, col - GDN_QK_W:col - GDN_QK_W + GDN_HEAD] = (yh * r).astype(BF16)
            else:
                v_ref[0, :, col - 2 * GDN_QK_W:col - 2 * GDN_QK_W + GDN_HEAD] = yh.astype(BF16)

    for c in range(GDN_V_W // CONV_COLS):
        cs = slice(c * CONV_COLS, (c + 1) * CONV_COLS)
        zs_ref[0, :, cs] = _silu(_dot(hn, wz_ref[:, cs])).astype(BF16)

    nh = GDN_V_HEADS
    ba = _dot(hn, wba_ref[...])
    bat = _dot_nt(wbat_ref[...], hn)
    beta_c = 1.0 / (1.0 + jnp.exp(-ba))
    g_c = -jnp.exp(arow_ref[...]) * _softplus(ba + dtrow_ref[...])
    beta_r = 1.0 / (1.0 + jnp.exp(-bat[0:nh, :]))
    g_r = -jnp.exp(acol_ref[...]) * _softplus(bat[nh:2 * nh, :] + dtcol_ref[...])

    ri = lax.broadcasted_iota(jnp.int32, (ROW_TILE, ROW_TILE), 0)
    ci = lax.broadcasted_iota(jnp.int32, (ROW_TILE, ROW_TILE), 1)
    same = (ri // GDN_CHUNK) == (ci // GDN_CHUNK)
    lower = jnp.where(same & (ri >= ci), 1.0, 0.0).astype(F32)
    upper = jnp.where(same & (ri <= ci), 1.0, 0.0).astype(F32)
    block = jnp.where(same, 1.0, 0.0).astype(F32)
    gc_c = _dot_exact(lower, g_c)
    gl_c = _dot_exact(block, g_c)
    gc_r = _dot_exact(g_r, upper)
    grp = lax.broadcasted_iota(jnp.int32, (ROW_TILE, LANES), 1) // nh
    gcol_ref[0] = jnp.where(grp == 0, beta_c,
                            jnp.where(grp == 1, gc_c,
                                      jnp.where(grp == 2, jnp.exp(gc_c),
                                                jnp.where(grp == 3, jnp.exp(gl_c - gc_c), 0.0))))
    rows = jnp.concatenate([gc_r, beta_r, beta_r * jnp.exp(gc_r)], axis=0)
    for c in range(ROW_TILE // GDN_CHUNK):
        grow_ref[0, c] = rows[:, c * GDN_CHUNK:(c + 1) * GDN_CHUNK]


def _unit_lower_inverse(m, masks):
    eye, diag8, merge_masks = masks
    m8 = jnp.where(diag8, m, 0.0)
    m8b = m8.astype(BF16)
    q2 = _dot(m8b, m8b)
    p = eye - m8
    q2b = q2.astype(BF16)
    p = p + _dot(p.astype(BF16), q2b)
    q4 = _dot(q2b, q2b)
    p = p + _dot(p.astype(BF16), q4.astype(BF16))
    for mask in merge_masks:
        off = jnp.where(mask, m, 0.0).astype(BF16)
        pb = p.astype(BF16)
        p = p - _dot(pb, _dot(off, pb).astype(BF16))
    return p


def _gdn_chunk_kernel(q_ref, k_ref, v_ref, zs_ref, gcol_ref, grow_ref, onorm_ref, o_ref, state_scr):
    @pl.when(pl.program_id(1) == 0)
    def _():
        state_scr[...] = jnp.zeros_like(state_scr)

    C = GDN_CHUNK
    ri = lax.broadcasted_iota(jnp.int32, (C, C), 0)
    ci = lax.broadcasted_iota(jnp.int32, (C, C), 1)
    incl = ri >= ci
    strict = ri > ci
    eye = jnp.where(ri == ci, 1.0, 0.0).astype(F32)
    diag8 = strict & ((ri // 8) == (ci // 8))
    merges = tuple(strict & ((ri // (2 * s)) == (ci // (2 * s))) & ((ri // s) != (ci // s))
                   for s in (8, 16, 32))
    masks = (eye, diag8, merges)
    nh = GDN_V_HEADS
    rep = GDN_V_HEADS // GDN_QK_HEADS
    gcol = gcol_ref[0]
    grow = grow_ref[0, 0]
    onorm = onorm_ref[...]

    for p_ in range(GDN_QK_HEADS):
        ps = slice(p_ * GDN_HEAD, (p_ + 1) * GDN_HEAD)
        qp = q_ref[0, :, ps]
        kp = k_ref[0, :, ps]
        kk = _dot_nt(kp, kp)
        qk = _dot_nt(qp, kp)
        kpf = kp.astype(F32)
        for h in range(p_ * rep, (p_ + 1) * rep):
            hs = slice(h * GDN_HEAD, (h + 1) * GDN_HEAD)
            beta_c = gcol[:, h:h + 1]
            gc_c = gcol[:, nh + h:nh + h + 1]
            eg_c = gcol[:, 2 * nh + h:2 * nh + h + 1]
            egl_c = gcol[:, 3 * nh + h:3 * nh + h + 1]
            gc_r = grow[h:h + 1, :]
            beta_r = grow[nh + h:nh + h + 1, :]
            bege_r = grow[2 * nh + h:2 * nh + h + 1, :]

            decay = jnp.exp(jnp.where(incl, gc_c - gc_r, -jnp.inf))
            m = jnp.where(strict, beta_c * kk * decay, 0.0)
            t = _unit_lower_inverse(m, masks)
            u = _dot((t * beta_r).astype(BF16), v_ref[0, :, hs])
            w = _dot((t * bege_r).astype(BF16), kp)
            s_old = state_scr[h]
            sb = s_old.astype(BF16)
            v_new = u - _dot(w.astype(BF16), sb)
            vnb = v_new.astype(BF16)
            o = eg_c * _dot(qp, sb) + _dot((qk * decay).astype(BF16), vnb)
            kd = (kpf * egl_c).astype(BF16)
            g_last = eg_c[C - 1:C, :]
            state_scr[h] = s_old * g_last + _dot_tn(kd, vnb)
            on = o * _rms_scale(o) * onorm
            o_ref[0, :, hs] = (on * zs_ref[0, :, hs].astype(F32)).astype(BF16)


def _out_proj_first_kernel(o_ref, w_ref, gain_ref, x_ref, head_ref, h_ref):
    y = _dot(o_ref[0], w_ref[...])
    res = jnp.where(pl.program_id(1) == 0, head_ref[...], x_ref[0])
    h_ref[0] = res + y * _rms_scale(y) * gain_ref[...]


def _out_proj_last_kernel(o_ref, w_ref, gain_ref, h_ref, out_ref):
    y = _dot(o_ref[0], w_ref[...])
    out_ref[0] = h_ref[0] + y * _rms_scale(y) * gain_ref[...]


def _mla_in_kernel(h_ref, pre_ref, kvn_ref, wcq_ref, wz_ref, qln_ref, wqn_ref, wqr_ref, wqrr_ref,
                   wkd_ref, kvln_ref, wkvu_ref, cos_ref, sin_ref,
                   q_ref, kk_ref, vv_ref, zs_ref):
    h = h_ref[0]
    hr = h * _rms_scale(h)
    hn = (hr * pre_ref[...]).astype(BF16)
    hk = (hr * kvn_ref[...]).astype(BF16)
    cosp = cos_ref[...]
    sinp = sin_ref[...]

    for c in range(wz_ref.shape[1] // CONV_COLS):
        cs = slice(c * CONV_COLS, (c + 1) * CONV_COLS)
        zs_ref[0, :, cs] = _silu(_dot(hn, wz_ref[:, cs])).astype(BF16)

    cq = _dot(hn, wcq_ref[...])
    cq = (cq * _rms_scale(cq) * qln_ref[...]).astype(BF16)
    scale = MLA_QK ** -0.5 * math.log2(math.e)
    for hd in range(MLA_HEADS):
        hs = slice(hd * LANES, (hd + 1) * LANES)
        qn = _dot(cq, wqn_ref[:, hs])
        qr = _dot(cq, wqr_ref[:, hs]) * cosp + _dot(cq, wqrr_ref[:, hs]) * sinp
        q_ref[0, hd, :, 0:MLA_NOPE] = (qn * scale).astype(BF16)
        q_ref[0, hd, :, MLA_NOPE:MLA_QK] = (qr[:, 0:MLA_ROPE] * scale).astype(BF16)

    ckr = _dot(hk, wkd_ref[...])
    lat = ckr[:, 0:MLA_KV_RANK]
    ckv = (lat * _rms_scale(lat) * kvln_ref[...]).astype(BF16)
    kr = ckr[:, LANES:2 * LANES] * cosp + ckr[:, 2 * LANES:3 * LANES] * sinp
    krb = kr[:, 0:MLA_ROPE].astype(BF16)
    for hd in range(MLA_HEADS):
        kn = _dot(ckv, wkvu_ref[:, hd * MLA_NOPE:(hd + 1) * MLA_NOPE])
        vv = _dot(ckv, wkvu_ref[:, MLA_HEADS * MLA_NOPE + hd * MLA_V:MLA_HEADS * MLA_NOPE + (hd + 1) * MLA_V])
        kk_ref[0, hd, :, 0:MLA_NOPE] = kn.astype(BF16)
        kk_ref[0, hd, :, MLA_NOPE:MLA_QK] = krb
        vv_ref[0, hd] = vv.astype(BF16)


def _attn_kernel(q_ref, k_ref, v_ref, zs_ref, o_ref, m_scr, l_scr, acc_scr):
    i = pl.program_id(2)
    tq, sb, tk = ATTN_TQ, ATTN_SUB, ATTN_TK
    m_scr[...] = jnp.full_like(m_scr, -jnp.inf)
    l_scr[...] = jnp.zeros_like(l_scr)
    acc_scr[...] = jnp.zeros_like(acc_scr)

    def update(r, k0, nk, mask):
        rs = slice(r * sb, (r + 1) * sb)
        kt = k_ref[0, 0, pl.ds(k0, nk), :]
        vt = v_ref[0, 0, pl.ds(k0, nk), :]
        s = _dot_nt(q_ref[0, 0, rs, :], kt)
        if mask is not None:
            s = jnp.where(mask, s, -jnp.inf)
        m_old = m_scr[rs, :]
        m_new = jnp.maximum(m_old, jnp.max(s, axis=-1, keepdims=True))
        alpha = jnp.exp2(m_old - m_new)
        ps = [jnp.exp2(s[:, c * LANES:(c + 1) * LANES] - m_new) for c in range(nk // LANES)]
        psum = ps[0]
        for pc in ps[1:]:
            psum = psum + pc
        l_scr[rs, :] = alpha * l_scr[rs, :] + psum
        p = jnp.concatenate(ps, axis=1).astype(BF16)
        acc_scr[rs, :] = alpha * acc_scr[rs, :] + _dot(p, vt)
        m_scr[rs, :] = m_new

    nsub = tq // sb
    ri = lax.broadcasted_iota(jnp.int32, (sb, sb), 0)
    ci = lax.broadcasted_iota(jnp.int32, (sb, sb), 1)
    for r in range(nsub):
        update(r, 0, ROW_TILE, ci >= FRONT_PAD)

    def body(j, carry):
        k0 = pl.multiple_of(ROW_TILE + j * tk, ROW_TILE)
        for r in range(nsub):
            update(r, k0, tk, None)
        return carry

    lax.fori_loop(0, (i * tq) // tk, body, 0)
    base = pl.multiple_of(ROW_TILE + i * tq, ROW_TILE)
    for r in range(nsub):
        if r > 0:
            update(r, base, r * sb, None)
        update(r, pl.multiple_of(base + r * sb, sb), sb, ci <= ri)

    l = jnp.sum(l_scr[...], axis=-1, keepdims=True)
    o_ref[0] = (acc_scr[...] / l * zs_ref[0].astype(F32)).astype(BF16)


def _params(*sem):
    return pltpu.CompilerParams(dimension_semantics=sem, vmem_limit_bytes=VMEM_LIMIT)


def kernel(x, meta_tokens, pre_norm, post_norm, gdn_w_in, gdn_conv_w, gdn_a_log, gdn_dt_bias, gdn_out_norm, gdn_w_out, kv_norm, kv_w_down, kv_latent_norm, kv_w_up, mla_w_in, mla_q_latent_norm, mla_w_q_up, mla_w_out):
    B, S, D = x.shape
    assert S % ATTN_TQ == 0 and S % ROW_TILE == 0
    assert gdn_w_in.shape[0] == 1 and mla_w_in.shape[0] == 1
    T = ROW_TILE
    nt = S // T + 1
    Lp = nt * T
    nch = Lp // GDN_CHUNK
    nh = GDN_V_HEADS

    head = jnp.concatenate([jnp.zeros((FRONT_PAD, D), F32), meta_tokens.astype(F32)], axis=0)
    row = lambda a: a.reshape(1, -1).astype(F32)

    w_in = gdn_w_in[0]
    w_qkv = w_in[:, :GDN_CONV_W].astype(BF16)
    w_z = w_in[:, GDN_CONV_W:GDN_CONV_W + GDN_V_W].astype(BF16)
    w_ba_f = w_in[:, GDN_CONV_W + GDN_V_W:]
    w_b, w_a = w_ba_f[:, :nh], w_ba_f[:, nh:]
    w_ba = jnp.concatenate([w_b, w_a, w_a, w_a, jnp.zeros((D, LANES - 4 * nh), F32)], axis=-1).astype(BF16)
    w_bat = w_ba_f.T.astype(BF16)
    a_log = gdn_a_log[0].astype(F32)
    dt_b = gdn_dt_bias[0].astype(F32)
    lane_groups = lambda a: jnp.concatenate([jnp.zeros((nh,), F32), a, a, a,
                                             jnp.zeros((LANES - 4 * nh,), F32)]).reshape(1, LANES)

    x_spec = pl.BlockSpec((1, T, D), lambda b, i: (b, jnp.maximum(i - 1, 0), 0))
    tile_spec = lambda w: pl.BlockSpec((1, T, w), lambda b, i: (b, i, 0))

    q, k, v, zs, gcol, grow = pl.pallas_call(
        _gdn_in_kernel,
        grid=(B, nt),
        in_specs=[x_spec, _const_spec((T, D)), _const_spec((1, D)),
                  _const_spec((D, GDN_CONV_W)), _const_spec((D, GDN_V_W)),
                  _const_spec((D, LANES)), _const_spec((2 * nh, D)),
                  _const_spec((GDN_CONV_TAPS, GDN_CONV_W)),
                  _const_spec((1, LANES)), _const_spec((1, LANES)), _const_spec((nh, 1)), _const_spec((nh, 1))],
        out_specs=[tile_spec(GDN_QK_W), tile_spec(GDN_QK_W), tile_spec(GDN_V_W), tile_spec(GDN_V_W),
                   tile_spec(LANES),
                   pl.BlockSpec((1, T // GDN_CHUNK, 3 * nh, GDN_CHUNK), lambda b, i: (b, i, 0, 0))],
        out_shape=[jax.ShapeDtypeStruct((B, Lp, GDN_QK_W), BF16),
                   jax.ShapeDtypeStruct((B, Lp, GDN_QK_W), BF16),
                   jax.ShapeDtypeStruct((B, Lp, GDN_V_W), BF16),
                   jax.ShapeDtypeStruct((B, Lp, GDN_V_W), BF16),
                   jax.ShapeDtypeStruct((B, Lp, LANES), F32),
                   jax.ShapeDtypeStruct((B, nch, 3 * nh, GDN_CHUNK), F32)],
        scratch_shapes=[pltpu.VMEM((HALO_ROWS, GDN_CONV_W), F32),
                        pltpu.VMEM((HALO_ROWS + T, CONV_COLS), F32)],
        compiler_params=_params("arbitrary", "arbitrary"),
        name="gdn_in",
    )(x, head, row(pre_norm[0]), w_qkv, w_z, w_ba, w_bat, gdn_conv_w[0].astype(F32),
      lane_groups(a_log), lane_groups(dt_b), a_log.reshape(nh, 1), dt_b.reshape(nh, 1))

    chunk_spec = lambda w: pl.BlockSpec((1, GDN_CHUNK, w), lambda b, n: (b, n, 0))
    o_gdn = pl.pallas_call(
        _gdn_chunk_kernel,
        grid=(B, nch),
        in_specs=[chunk_spec(GDN_QK_W), chunk_spec(GDN_QK_W), chunk_spec(GDN_V_W), chunk_spec(GDN_V_W),
                  chunk_spec(LANES),
                  pl.BlockSpec((1, 1, 3 * nh, GDN_CHUNK), lambda b, n: (b, n, 0, 0)),
                  _const_spec((1, GDN_HEAD))],
        out_specs=chunk_spec(GDN_V_W),
        out_shape=jax.ShapeDtypeStruct((B, Lp, GDN_V_W), BF16),
        scratch_shapes=[pltpu.VMEM((nh, GDN_HEAD, GDN_HEAD), F32)],
        compiler_params=_params("arbitrary", "arbitrary"),
        name="gdn_chunk",
    )(q, k, v, zs, gcol, grow, row(gdn_out_norm[0]))

    h1 = pl.pallas_call(
        _out_proj_first_kernel,
        grid=(B, nt),
        in_specs=[tile_spec(GDN_V_W), _const_spec((GDN_V_W, D)), _const_spec((1, D)),
                  x_spec, _const_spec((T, D))],
        out_specs=tile_spec(D),
        out_shape=jax.ShapeDtypeStruct((B, Lp, D), F32),
        compiler_params=_params("parallel", "arbitrary"),
        name="gdn_out",
    )(o_gdn, gdn_w_out[0].astype(BF16), row(post_norm[0]), x, head)

    w_in1 = mla_w_in[0]
    w_cq = w_in1[:, :MLA_Q_RANK].astype(BF16)
    w_z1 = w_in1[:, MLA_Q_RANK:].astype(BF16)
    zw = w_z1.shape[1]
    half = MLA_ROPE // 2
    rot = lambda w: jnp.concatenate([-w[..., half:], w[..., :half]], axis=-1)
    lane_pad = lambda w: jnp.pad(w, [(0, 0)] * (w.ndim - 1) + [(0, LANES - w.shape[-1])])
    wq = mla_w_q_up[0].reshape(MLA_Q_RANK, MLA_HEADS, MLA_QK)
    w_qn = wq[..., :MLA_NOPE].reshape(MLA_Q_RANK, MLA_HEADS * MLA_NOPE).astype(BF16)
    w_qr = lane_pad(wq[..., MLA_NOPE:]).reshape(MLA_Q_RANK, MLA_HEADS * LANES).astype(BF16)
    w_qrr = lane_pad(rot(wq[..., MLA_NOPE:])).reshape(MLA_Q_RANK, MLA_HEADS * LANES).astype(BF16)
    wkd_r = kv_w_down[:, MLA_KV_RANK:]
    w_kd = jnp.concatenate([kv_w_down[:, :MLA_KV_RANK], lane_pad(wkd_r), lane_pad(rot(wkd_r))],
                           axis=-1).astype(BF16)
    wku = kv_w_up.reshape(MLA_KV_RANK, MLA_HEADS, MLA_NOPE + MLA_V)
    w_kvu = jnp.concatenate([wku[..., :MLA_NOPE].reshape(MLA_KV_RANK, -1),
                             wku[..., MLA_NOPE:].reshape(MLA_KV_RANK, -1)], axis=-1).astype(BF16)

    inv = ROPE_THETA ** (-jnp.arange(0, MLA_ROPE, 2, dtype=F32) / MLA_ROPE)
    pos = (jnp.arange(Lp, dtype=jnp.int32) - FRONT_PAD).astype(F32)
    ang = pos[:, None] * inv[None, :]
    zpad = jnp.zeros((Lp, LANES - MLA_ROPE), F32)
    cosp = jnp.concatenate([jnp.cos(ang), jnp.cos(ang), zpad], axis=-1)
    sinp = jnp.concatenate([jnp.sin(ang), jnp.sin(ang), zpad], axis=-1)

    head_tile = lambda w: pl.BlockSpec((1, MLA_HEADS, T, w), lambda b, i: (b, 0, i, 0))
    q1, k1, v1, zs1 = pl.pallas_call(
        _mla_in_kernel,
        grid=(B, nt),
        in_specs=[tile_spec(D), _const_spec((1, D)), _const_spec((1, D)),
                  _const_spec((D, MLA_Q_RANK)), _const_spec((D, zw)), _const_spec((1, MLA_Q_RANK)),
                  _const_spec((MLA_Q_RANK, MLA_HEADS * MLA_NOPE)),
                  _const_spec((MLA_Q_RANK, MLA_HEADS * LANES)),
                  _const_spec((MLA_Q_RANK, MLA_HEADS * LANES)),
                  _const_spec((D, 3 * LANES)), _const_spec((1, MLA_KV_RANK)),
                  _const_spec((MLA_KV_RANK, MLA_HEADS * (MLA_NOPE + MLA_V))),
                  pl.BlockSpec((T, LANES), lambda b, i: (i, 0)),
                  pl.BlockSpec((T, LANES), lambda b, i: (i, 0))],
        out_specs=[pl.BlockSpec((1, MLA_HEADS, T, MLA_QK), lambda b, i: (b, 0, jnp.maximum(i - 1, 0), 0)),
                   head_tile(MLA_QK), head_tile(MLA_V),
                   pl.BlockSpec((1, T, zw), lambda b, i: (b, jnp.maximum(i - 1, 0), 0))],
        out_shape=[jax.ShapeDtypeStruct((B, MLA_HEADS, S, MLA_QK), BF16),
                   jax.ShapeDtypeStruct((B, MLA_HEADS, Lp, MLA_QK), BF16),
                   jax.ShapeDtypeStruct((B, MLA_HEADS, Lp, MLA_V), BF16),
                   jax.ShapeDtypeStruct((B, S, zw), BF16)],
        compiler_params=_params("arbitrary", "arbitrary"),
        name="mla_in",
    )(h1, row(pre_norm[1]), row(kv_norm), w_cq, w_z1, row(mla_q_latent_norm[0]), w_qn, w_qr, w_qrr,
      w_kd, row(kv_latent_norm), w_kvu, cosp, sinp)

    o_attn = pl.pallas_call(
        _attn_kernel,
        grid=(B, MLA_HEADS, S // ATTN_TQ),
        in_specs=[pl.BlockSpec((1, 1, ATTN_TQ, MLA_QK), lambda b, h, i: (b, h, i, 0)),
                  pl.BlockSpec((1, 1, Lp, MLA_QK), lambda b, h, i: (b, h, 0, 0)),
                  pl.BlockSpec((1, 1, Lp, MLA_V), lambda b, h, i: (b, h, 0, 0)),
                  pl.BlockSpec((1, ATTN_TQ, MLA_V), lambda b, h, i: (b, i, h))],
        out_specs=pl.BlockSpec((1, ATTN_TQ, MLA_V), lambda b, h, i: (b, i, h)),
        out_shape=jax.ShapeDtypeStruct((B, S, MLA_HEADS * MLA_V), BF16),
        scratch_shapes=[pltpu.VMEM((ATTN_TQ, LANES), F32), pltpu.VMEM((ATTN_TQ, LANES), F32),
                        pltpu.VMEM((ATTN_TQ, MLA_V), F32)],
        compiler_params=_params("parallel", "parallel", "arbitrary"),
        name="mla_attn",
    )(q1, k1, v1, zs1)

    out = pl.pallas_call(
        _out_proj_last_kernel,
        grid=(B, S // T),
        in_specs=[pl.BlockSpec((1, T, MLA_HEADS * MLA_V), lambda b, i: (b, i, 0)),
                  _const_spec((MLA_HEADS * MLA_V, D)), _const_spec((1, D)),
                  pl.BlockSpec((1, T, D), lambda b, i: (b, i + 1, 0))],
        out_specs=pl.BlockSpec((1, T, D), lambda b, i: (b, i, 0)),
        out_shape=jax.ShapeDtypeStruct((B, S, D), x.dtype),
        compiler_params=_params("parallel", "arbitrary"),
        name="mla_out",
    )(o_attn, mla_w_out[0].astype(BF16), row(post_norm[1]), h1)
    return out
```

```python
import functools
import math

import jax
import jax.numpy as jnp
from jax import lax
from jax.experimental import pallas as pl
from jax.experimental.pallas import tpu as pltpu

NORM_EPS = 1e-6
N_META_ROWS = 16

GDN_QK_HEADS = 8
GDN_V_HEADS = 16
GDN_HEAD = 128
GDN_CONV_TAPS = 4
GDN_CHUNK = 64
GDN_QK_W = GDN_QK_HEADS * GDN_HEAD
GDN_V_W = GDN_V_HEADS * GDN_HEAD
GDN_CONV_W = 2 * GDN_QK_W + GDN_V_W

MLA_HEADS = 16
MLA_NOPE = 128
MLA_ROPE = 64
MLA_V = 128
MLA_Q_RANK = 256
MLA_KV_RANK = 128
MLA_QK = MLA_NOPE + MLA_ROPE
ROPE_THETA = 10000.0

LANES = 128
ROW_TILE = 256
FRONT_PAD = ROW_TILE - N_META_ROWS
CONV_COLS = 512
HALO_ROWS = 8
ATTN_TQ = 1024
ATTN_SUB = 256
ATTN_TK = 512
VMEM_LIMIT = 56 * 1024 * 1024

F32 = jnp.float32
BF16 = jnp.bfloat16


def _dot(a, b):
    return jnp.dot(a, b, preferred_element_type=F32)


def _dot_nt(a, b):
    return lax.dot_general(a, b, (((1,), (1,)), ((), ())), preferred_element_type=F32)


def _dot_tn(a, b):
    return lax.dot_general(a, b, (((0,), (0,)), ((), ())), preferred_element_type=F32)


def _dot_exact(a, b):
    return jnp.dot(a, b, preferred_element_type=F32, precision=lax.Precision.HIGHEST)


def _silu(x):
    return x / (1.0 + jnp.exp(-x))


def _softplus(x):
    return jnp.maximum(x, 0.0) + jnp.log1p(jnp.exp(-jnp.abs(x)))


def _rms_scale(x):
    return lax.rsqrt(jnp.mean(x * x, axis=-1, keepdims=True) + NORM_EPS)


def _const_spec(shape):
    nd = len(shape)
    return pl.BlockSpec(shape, lambda *_: (0,) * nd, pipeline_mode=pl.Buffered(1))


def _gdn_in_kernel(x_ref, head_ref, gain_ref, wqkv_ref, wz_ref, wba_ref, wbat_ref, convw_ref,
                   arow_ref, dtrow_ref, acol_ref, dtcol_ref,
                   q_ref, k_ref, v_ref, zs_ref, gcol_ref, grow_ref,
                   halo_scr, buf_scr):
    i = pl.program_id(1)
    x = jnp.where(i == 0, head_ref[...], x_ref[0])
    hn = (x * _rms_scale(x) * gain_ref[...]).astype(BF16)

    @pl.when(i == 0)
    def _():
        halo_scr[...] = jnp.zeros_like(halo_scr)

    lo = HALO_ROWS - (GDN_CONV_TAPS - 1)
    for c in range(GDN_CONV_W // CONV_COLS):
        cs = slice(c * CONV_COLS, (c + 1) * CONV_COLS)
        p = _dot(hn, wqkv_ref[:, cs])
        buf_scr[0:HALO_ROWS, :] = halo_scr[:, cs]
        buf_scr[HALO_ROWS:HALO_ROWS + ROW_TILE, :] = p
        halo_scr[:, cs] = p[ROW_TILE - HALO_ROWS:, :]
        y = convw_ref[0:1, cs] * buf_scr[lo:lo + ROW_TILE, :]
        for j in range(1, GDN_CONV_TAPS):
            y = y + convw_ref[j:j + 1, cs] * buf_scr[lo + j:lo + j + ROW_TILE, :]
        y = _silu(y)
        for hh in range(CONV_COLS // GDN_HEAD):
            col = c * CONV_COLS + hh * GDN_HEAD
            yh = y[:, hh * GDN_HEAD:(hh + 1) * GDN_HEAD]
            if col < 2 * GDN_QK_W:
                r = lax.rsqrt(jnp.sum(yh * yh, axis=-1, keepdims=True) + NORM_EPS)
                if col < GDN_QK_W:
                    q_ref[0, :, col:col + GDN_HEAD] = (yh * r * (GDN_HEAD ** -0.5)).astype(BF16)
                else:
                    k_ref[0, :, col - GDN_QK_W:col - GDN_QK_W + GDN_HEAD] = (yh * r).astype(BF16)
            else:
                v_ref[0, :, col - 2 * GDN_QK_W:col - 2 * GDN_QK_W + GDN_HEAD] = yh.astype(BF16)

    for c in range(GDN_V_W // CONV_COLS):
        cs = slice(c * CONV_COLS, (c + 1) * CONV_COLS)
        zs_ref[0, :, cs] = _silu(_dot(hn, wz_ref[:, cs])).astype(BF16)

    nh = GDN_V_HEADS
    ba = _dot(hn, wba_ref[...])
    bat = _dot_nt(wbat_ref[...], hn)
    beta_c = 1.0 / (1.0 + jnp.exp(-ba))
    g_c = -jnp.exp(arow_ref[...]) * _softplus(ba + dtrow_ref[...])
    beta_r = 1.0 / (1.0 + jnp.exp(-bat[0:nh, :]))
    g_r = -jnp.exp(acol_ref[...]) * _softplus(bat[nh:2 * nh, :] + dtcol_ref[...])

    ri = lax.broadcasted_iota(jnp.int32, (ROW_TILE, ROW_TILE), 0)
    ci = lax.broadcasted_iota(jnp.int32, (ROW_TILE, ROW_TILE), 1)
    same = (ri // GDN_CHUNK) == (ci // GDN_CHUNK)
    lower = jnp.where(same & (ri >= ci), 1.0, 0.0).astype(F32)
    upper = jnp.where(same & (ri <= ci), 1.0, 0.0).astype(F32)
    block = jnp.where(same, 1.0, 0.0).astype(F32)
    gc_c = _dot_exact(lower, g_c)
    gl_c = _dot_exact(block, g_c)
    gc_r = _dot_exact(g_r, upper)
    grp = lax.broadcasted_iota(jnp.int32, (ROW_TILE, LANES), 1) // nh
    gcol_ref[0] = jnp.where(grp == 0, beta_c,
                            jnp.where(grp == 1, gc_c,
                                      jnp.where(grp == 2, jnp.exp(gc_c),
                                                jnp.where(grp == 3, jnp.exp(gl_c - gc_c), 0.0))))
    npair = nh // 2
    for kind, xr in enumerate((gc_r, beta_r, beta_r * jnp.exp(gc_r))):
        for c in range(ROW_TILE // GDN_CHUNK):
            cs = slice(c * GDN_CHUNK, (c + 1) * GDN_CHUNK)
            grow_ref[0, c, kind * npair:(kind + 1) * npair, :] = jnp.concatenate(
                [xr[0:npair, cs], xr[npair:nh, cs]], axis=-1)


def _gdn_prep_kernel(q_ref, k_ref, gcol_ref, grow_ref, tb_ref, w_ref, a_ref, kd_ref):
    C = GDN_CHUNK
    nh = GDN_V_HEADS
    npair = nh // 2
    row = lax.broadcasted_iota(jnp.int32, (C, LANES), 0)
    lane = lax.broadcasted_iota(jnp.int32, (C, LANES), 1)
    left = lane < C
    col = jnp.where(left, lane, lane - C)
    incl = row >= col
    strict = row > col
    eye = jnp.where(row == col, 1.0, 0.0).astype(F32)
    diag8 = strict & ((row // 8) == (col // 8))
    merges = tuple(strict & ((row // (2 * s)) == (col // (2 * s))) & ((row // s) != (col // s))
                   for s in (8, 16, 32))
    zero_b = jnp.zeros((C, LANES), BF16)

    def blockdiag(x):
        xb = x.astype(BF16)
        return jnp.concatenate([jnp.where(left, xb, zero_b), jnp.where(left, zero_b, xb)], axis=0)

    def pair_cols(g, base, p_):
        return jnp.where(left, g[:, base + 2 * p_:base + 2 * p_ + 1], g[:, base + 2 * p_ + 1:base + 2 * p_ + 2])

    streams = [(c, p_) for c in range(ROW_TILE // C) for p_ in range(npair)]
    ms, kps, bege, betar = [], [], [], []
    for c, p_ in streams:
        rs = slice(c * C, (c + 1) * C)
        ps = slice(p_ * GDN_HEAD, (p_ + 1) * GDN_HEAD)
        qp = q_ref[0, rs, ps]
        kp = k_ref[0, rs, ps]
        g = gcol_ref[0, rs, :]
        both = _dot_nt(jnp.concatenate([qp, kp], axis=0), jnp.concatenate([kp, kp], axis=0))
        qk2, kk2 = both[0:C], both[C:2 * C]
        decay = jnp.exp(jnp.where(incl, pair_cols(g, nh, p_) - grow_ref[0, c, p_:p_ + 1, :], -jnp.inf))
        ms.append(jnp.where(strict, pair_cols(g, 0, p_) * kk2 * decay, 0.0))
        a_ref[0, rs, ps] = (qk2 * decay).astype(BF16)
        kpf = kp.astype(F32)
        for h in (2 * p_, 2 * p_ + 1):
            kd_ref[0, rs, h * GDN_HEAD:(h + 1) * GDN_HEAD] = (kpf * g[:, 3 * nh + h:3 * nh + h + 1]).astype(BF16)
        betar.append(grow_ref[0, c, npair + p_:npair + p_ + 1, :])
        kps.append(kp)
        bege.append(grow_ref[0, c, 2 * npair + p_:2 * npair + p_ + 1, :])

    m8 = [jnp.where(diag8, m, 0.0) for m in ms]
    q2 = [_dot(x.astype(BF16), blockdiag(x)) for x in m8]
    pinv = [eye - x for x in m8]
    pinv = [p + _dot(p.astype(BF16), blockdiag(y)) for p, y in zip(pinv, q2)]
    q4 = [_dot(y.astype(BF16), blockdiag(y)) for y in q2]
    pinv = [p + _dot(p.astype(BF16), blockdiag(y)) for p, y in zip(pinv, q4)]
    for mask in merges:
        cp = [_dot(jnp.where(mask, m, 0.0).astype(BF16), blockdiag(p)) for m, p in zip(ms, pinv)]
        pinv = [p - _dot(p.astype(BF16), blockdiag(y)) for p, y in zip(pinv, cp)]

    zero_k = jnp.zeros((C, GDN_HEAD), BF16)
    for (c, p_), p, kp, bg, br in zip(streams, pinv, kps, bege, betar):
        rs = slice(c * C, (c + 1) * C)
        ps = slice(p_ * GDN_HEAD, (p_ + 1) * GDN_HEAD)
        kbd = jnp.concatenate([jnp.concatenate([kp, zero_k], axis=1),
                               jnp.concatenate([zero_k, kp], axis=1)], axis=0)
        w_ref[0, rs, 2 * p_ * GDN_HEAD:(2 * p_ + 2) * GDN_HEAD] = _dot((p * bg).astype(BF16), kbd).astype(BF16)
        tb_ref[0, rs, ps] = (p * br).astype(BF16)


def _gdn_scan_kernel(tb_ref, w_ref, a_ref, kd_ref, q_ref, v_ref, zs_ref, gcol_ref, onorm_ref, o_ref, state_scr):
    @pl.when(pl.program_id(0) == 0)
    def _():
        state_scr[...] = jnp.zeros_like(state_scr)

    C = GDN_CHUNK
    nh = GDN_V_HEADS
    nb = q_ref.shape[0]
    onorm = onorm_ref[...]
    zero_v = jnp.zeros((C, GDN_HEAD), BF16)

    def blockdiag(x0, x1):
        return jnp.concatenate([jnp.concatenate([x0, zero_v], axis=1),
                                jnp.concatenate([zero_v, x1], axis=1)], axis=0)

    pairs = [(b, p_) for b in range(nb) for p_ in range(nh // 2)]
    heads = [(b, h) for b, p_ in pairs for h in (2 * p_, 2 * p_ + 1)]
    hsl = lambda h: slice(h * GDN_HEAD, (h + 1) * GDN_HEAD)

    u2 = [_dot(tb_ref[b, :, hsl(p_)], blockdiag(v_ref[b, :, hsl(2 * p_)], v_ref[b, :, hsl(2 * p_ + 1)]))
          for b, p_ in pairs]
    u = {(b, 2 * p_ + j): x[:, j * GDN_HEAD:(j + 1) * GDN_HEAD] for (b, p_), x in zip(pairs, u2) for j in (0, 1)}
    s_old = {bh: state_scr[bh[0], bh[1]] for bh in heads}
    wq = {(b, h): _dot(jnp.concatenate([w_ref[b, :, hsl(h)], q_ref[b, :, hsl(h // 2)]], axis=0),
                       s_old[(b, h)].astype(BF16)) for b, h in heads}
    vnb = {bh: (u[bh] - wq[bh][0:C]).astype(BF16) for bh in heads}
    o2 = [_dot(a_ref[b, :, hsl(p_)], blockdiag(vnb[(b, 2 * p_)], vnb[(b, 2 * p_ + 1)])) for b, p_ in pairs]
    for b, h in heads:
        eg_c = gcol_ref[b, :, 2 * nh + h:2 * nh + h + 1]
        state_scr[b, h] = s_old[(b, h)] * eg_c[C - 1:C, :] + _dot_tn(kd_ref[b, :, hsl(h)], vnb[(b, h)])
    for (b, p_), o2p in zip(pairs, o2):
        for j in (0, 1):
            h = 2 * p_ + j
            eg_c = gcol_ref[b, :, 2 * nh + h:2 * nh + h + 1]
            o = eg_c * wq[(b, h)][C:2 * C] + o2p[:, j * GDN_HEAD:(j + 1) * GDN_HEAD]
            on = o * _rms_scale(o) * onorm
            o_ref[b, :, hsl(h)] = (on * zs_ref[b, :, hsl(h)].astype(F32)).astype(BF16)


def _out_proj_first_kernel(o_ref, w_ref, gain_ref, x_ref, head_ref, h_ref):
    y = _dot(o_ref[0], w_ref[...])
    res = jnp.where(pl.program_id(1) == 0, head_ref[...], x_ref[0])
    h_ref[0] = res + y * _rms_scale(y) * gain_ref[...]


def _out_proj_last_kernel(o_ref, w_ref, gain_ref, h_ref, out_ref):
    y = _dot(o_ref[0], w_ref[...])
    out_ref[0] = h_ref[0] + y * _rms_scale(y) * gain_ref[...]


def _mla_in_kernel(h_ref, pre_ref, kvn_ref, wcq_ref, wz_ref, qln_ref, wqn_ref, wqr_ref, wqrr_ref,
                   wkd_ref, kvln_ref, wkvu_ref, cos_ref, sin_ref,
                   q_ref, kk_ref, vv_ref, zs_ref):
    h = h_ref[0]
    hr = h * _rms_scale(h)
    hn = (hr * pre_ref[...]).astype(BF16)
    hk = (hr * kvn_ref[...]).astype(BF16)
    cosp = cos_ref[...]
    sinp = sin_ref[...]

    for c in range(wz_ref.shape[1] // CONV_COLS):
        cs = slice(c * CONV_COLS, (c + 1) * CONV_COLS)
        zs_ref[0, :, cs] = _silu(_dot(hn, wz_ref[:, cs])).astype(BF16)

    cq = _dot(hn, wcq_ref[...])
    cq = (cq * _rms_scale(cq) * qln_ref[...]).astype(BF16)
    scale = MLA_QK ** -0.5 * math.log2(math.e)
    for hd in range(MLA_HEADS):
        hs = slice(hd * LANES, (hd + 1) * LANES)
        qn = _dot(cq, wqn_ref[:, hs])
        qr = _dot(cq, wqr_ref[:, hs]) * cosp + _dot(cq, wqrr_ref[:, hs]) * sinp
        q_ref[0, hd, :, 0:MLA_NOPE] = (qn * scale).astype(BF16)
        q_ref[0, hd, :, MLA_NOPE:MLA_QK] = (qr[:, 0:MLA_ROPE] * scale).astype(BF16)

    ckr = _dot(hk, wkd_ref[...])
    lat = ckr[:, 0:MLA_KV_RANK]
    ckv = (lat * _rms_scale(lat) * kvln_ref[...]).astype(BF16)
    kr = ckr[:, LANES:2 * LANES] * cosp + ckr[:, 2 * LANES:3 * LANES] * sinp
    krb = kr[:, 0:MLA_ROPE].astype(BF16)
    for hd in range(MLA_HEADS):
        kn = _dot(ckv, wkvu_ref[:, hd * MLA_NOPE:(hd + 1) * MLA_NOPE])
        vv = _dot(ckv, wkvu_ref[:, MLA_HEADS * MLA_NOPE + hd * MLA_V:MLA_HEADS * MLA_NOPE + (hd + 1) * MLA_V])
        kk_ref[0, hd, :, 0:MLA_NOPE] = kn.astype(BF16)
        kk_ref[0, hd, :, MLA_NOPE:MLA_QK] = krb
        vv_ref[0, hd] = vv.astype(BF16)


def _attn_kernel(q_ref, k_ref, v_ref, zs_ref, o_ref, m_scr, l_scr, acc_scr, sa_scr, sb_scr):
    i = pl.program_id(2)
    tq, sb, tk = ATTN_TQ, ATTN_SUB, ATTN_TK
    m_scr[...] = jnp.full_like(m_scr, -jnp.inf)
    l_scr[...] = jnp.zeros_like(l_scr)
    acc_scr[...] = jnp.zeros_like(acc_scr)

    def rows(r):
        return slice(r * sb, (r + 1) * sb)

    def scores(r, k0, nk):
        return _dot_nt(q_ref[0, 0, rows(r), :], k_ref[0, 0, pl.ds(k0, nk), :])

    def absorb(r, parts):
        rs = rows(r)
        m_old = m_scr[rs, :]
        m_new = m_old
        for s, _, _ in parts:
            m_new = jnp.maximum(m_new, jnp.max(s, axis=-1, keepdims=True))
        alpha = jnp.exp2(m_old - m_new)
        psum = alpha * l_scr[rs, :]
        acc = alpha * acc_scr[rs, :]
        for s, k0, nk in parts:
            ps = [jnp.exp2(s[:, c * LANES:(c + 1) * LANES] - m_new) for c in range(nk // LANES)]
            for pc in ps:
                psum = psum + pc
            acc = acc + _dot(jnp.concatenate(ps, axis=1).astype(BF16), v_ref[0, 0, pl.ds(k0, nk), :])
        l_scr[rs, :] = psum
        acc_scr[rs, :] = acc
        m_scr[rs, :] = m_new

    nsub = tq // sb
    block0 = lambda j: pl.multiple_of(ROW_TILE + j * tk, ROW_TILE)

    ri = lax.broadcasted_iota(jnp.int32, (sb, sb), 0)
    ci = lax.broadcasted_iota(jnp.int32, (sb, sb), 1)
    base = pl.multiple_of(ROW_TILE + i * tq, ROW_TILE)
    s_meta = [jnp.where(ci >= FRONT_PAD, scores(r, 0, ROW_TILE), -jnp.inf) for r in range(nsub)]
    s_diag = [scores(r, base, (r + 1) * sb) for r in range(nsub)]
    for r in range(nsub):
        sa_scr[r] = scores(r, block0(0), tk)
    for r in range(nsub):
        tri = jnp.where(ci <= ri, s_diag[r][:, r * sb:], -jnp.inf)
        sd = tri if r == 0 else jnp.concatenate([s_diag[r][:, :r * sb], tri], axis=1)
        absorb(r, [(s_meta[r], 0, ROW_TILE), (sd, base, (r + 1) * sb)])

    assert tq % (2 * tk) == 0

    def body(jj, carry):
        j = 2 * jj
        for cur, nxt, step in ((sa_scr, sb_scr, 0), (sb_scr, sa_scr, 1)):
            for r in range(nsub):
                nxt[r] = scores(r, block0(j + step + 1), tk)
                absorb(r, [(cur[r], block0(j + step), tk)])
        return carry

    lax.fori_loop(0, (i * tq) // (2 * tk), body, 0)

    l = jnp.sum(l_scr[...], axis=-1, keepdims=True)
    o_ref[0] = (acc_scr[...] / l * zs_ref[0].astype(F32)).astype(BF16)


def _params(*sem):
    return pltpu.CompilerParams(dimension_semantics=sem, vmem_limit_bytes=VMEM_LIMIT)


def kernel(x, meta_tokens, pre_norm, post_norm, gdn_w_in, gdn_conv_w, gdn_a_log, gdn_dt_bias, gdn_out_norm, gdn_w_out, kv_norm, kv_w_down, kv_latent_norm, kv_w_up, mla_w_in, mla_q_latent_norm, mla_w_q_up, mla_w_out):
    B, S, D = x.shape
    assert S % ATTN_TQ == 0 and S % ROW_TILE == 0
    assert gdn_w_in.shape[0] == 1 and mla_w_in.shape[0] == 1
    T = ROW_TILE
    nt = S // T + 1
    Lp = nt * T
    nch = Lp // GDN_CHUNK
    nh = GDN_V_HEADS

    head = jnp.concatenate([jnp.zeros((FRONT_PAD, D), F32), meta_tokens.astype(F32)], axis=0)
    row = lambda a: a.reshape(1, -1).astype(F32)

    w_in = gdn_w_in[0]
    w_qkv = w_in[:, :GDN_CONV_W].astype(BF16)
    w_z = w_in[:, GDN_CONV_W:GDN_CONV_W + GDN_V_W].astype(BF16)
    w_ba_f = w_in[:, GDN_CONV_W + GDN_V_W:]
    w_b, w_a = w_ba_f[:, :nh], w_ba_f[:, nh:]
    w_ba = jnp.concatenate([w_b, w_a, w_a, w_a, jnp.zeros((D, LANES - 4 * nh), F32)], axis=-1).astype(BF16)
    eo = lambda a: jnp.concatenate([a[0::2], a[1::2]], axis=0)
    w_bat = jnp.concatenate([eo(w_b.T), eo(w_a.T)], axis=0).astype(BF16)
    a_log = gdn_a_log[0].astype(F32)
    dt_b = gdn_dt_bias[0].astype(F32)
    lane_groups = lambda a: jnp.concatenate([jnp.zeros((nh,), F32), a, a, a,
                                             jnp.zeros((LANES - 4 * nh,), F32)]).reshape(1, LANES)

    x_spec = pl.BlockSpec((1, T, D), lambda b, i: (b, jnp.maximum(i - 1, 0), 0))
    tile_spec = lambda w: pl.BlockSpec((1, T, w), lambda b, i: (b, i, 0))

    q, k, v, zs, gcol, grow = pl.pallas_call(
        _gdn_in_kernel,
        grid=(B, nt),
        in_specs=[x_spec, _const_spec((T, D)), _const_spec((1, D)),
                  _const_spec((D, GDN_CONV_W)), _const_spec((D, GDN_V_W)),
                  _const_spec((D, LANES)), _const_spec((2 * nh, D)),
                  _const_spec((GDN_CONV_TAPS, GDN_CONV_W)),
                  _const_spec((1, LANES)), _const_spec((1, LANES)), _const_spec((nh, 1)), _const_spec((nh, 1))],
        out_specs=[tile_spec(GDN_QK_W), tile_spec(GDN_QK_W), tile_spec(GDN_V_W), tile_spec(GDN_V_W),
                   tile_spec(LANES),
                   pl.BlockSpec((1, T // GDN_CHUNK, 3 * nh // 2, LANES), lambda b, i: (b, i, 0, 0))],
        out_shape=[jax.ShapeDtypeStruct((B, Lp, GDN_QK_W), BF16),
                   jax.ShapeDtypeStruct((B, Lp, GDN_QK_W), BF16),
                   jax.ShapeDtypeStruct((B, Lp, GDN_V_W), BF16),
                   jax.ShapeDtypeStruct((B, Lp, GDN_V_W), BF16),
                   jax.ShapeDtypeStruct((B, Lp, LANES), F32),
                   jax.ShapeDtypeStruct((B, nch, 3 * nh // 2, LANES), F32)],
        scratch_shapes=[pltpu.VMEM((HALO_ROWS, GDN_CONV_W), F32),
                        pltpu.VMEM((HALO_ROWS + T, CONV_COLS), F32)],
        compiler_params=_params("arbitrary", "arbitrary"),
        name="gdn_in",
    )(x, head, row(pre_norm[0]), w_qkv, w_z, w_ba, w_bat, gdn_conv_w[0].astype(F32),
      lane_groups(a_log), lane_groups(dt_b), eo(a_log).reshape(nh, 1), eo(dt_b).reshape(nh, 1))

    tb, wk, aqk, kd = pl.pallas_call(
        _gdn_prep_kernel,
        grid=(B, nt),
        in_specs=[tile_spec(GDN_QK_W), tile_spec(GDN_QK_W), tile_spec(LANES),
                  pl.BlockSpec((1, T // GDN_CHUNK, 3 * nh // 2, LANES), lambda b, i: (b, i, 0, 0))],
        out_specs=[tile_spec(GDN_QK_W), tile_spec(GDN_V_W), tile_spec(GDN_QK_W), tile_spec(GDN_V_W)],
        out_shape=[jax.ShapeDtypeStruct((B, Lp, GDN_QK_W), BF16),
                   jax.ShapeDtypeStruct((B, Lp, GDN_V_W), BF16),
                   jax.ShapeDtypeStruct((B, Lp, GDN_QK_W), BF16),
                   jax.ShapeDtypeStruct((B, Lp, GDN_V_W), BF16)],
        compiler_params=_params("parallel", "parallel"),
        name="gdn_prep",
    )(q, k, gcol, grow)

    chunk_spec = lambda w: pl.BlockSpec((B, GDN_CHUNK, w), lambda n: (0, n, 0))
    o_gdn = pl.pallas_call(
        _gdn_scan_kernel,
        grid=(nch,),
        in_specs=[chunk_spec(GDN_QK_W), chunk_spec(GDN_V_W), chunk_spec(GDN_QK_W), chunk_spec(GDN_V_W),
                  chunk_spec(GDN_QK_W), chunk_spec(GDN_V_W), chunk_spec(GDN_V_W), chunk_spec(LANES),
                  _const_spec((1, GDN_HEAD))],
        out_specs=chunk_spec(GDN_V_W),
        out_shape=jax.ShapeDtypeStruct((B, Lp, GDN_V_W), BF16),
        scratch_shapes=[pltpu.VMEM((B, nh, GDN_HEAD, GDN_HEAD), F32)],
        compiler_params=_params("arbitrary"),
        name="gdn_scan",
    )(tb, wk, aqk, kd, q, v, zs, gcol, row(gdn_out_norm[0]))

    h1 = pl.pallas_call(
        _out_proj_first_kernel,
        grid=(B, nt),
        in_specs=[tile_spec(GDN_V_W), _const_spec((GDN_V_W, D)), _const_spec((1, D)),
                  x_spec, _const_spec((T, D))],
        out_specs=tile_spec(D),
        out_shape=jax.ShapeDtypeStruct((B, Lp, D), F32),
        compiler_params=_params("parallel", "arbitrary"),
        name="gdn_out",
    )(o_gdn, gdn_w_out[0].astype(BF16), row(post_norm[0]), x, head)

    w_in1 = mla_w_in[0]
    w_cq = w_in1[:, :MLA_Q_RANK].astype(BF16)
    w_z1 = w_in1[:, MLA_Q_RANK:].astype(BF16)
    zw = w_z1.shape[1]
    half = MLA_ROPE // 2
    rot = lambda w: jnp.concatenate([-w[..., half:], w[..., :half]], axis=-1)
    lane_pad = lambda w: jnp.pad(w, [(0, 0)] * (w.ndim - 1) + [(0, LANES - w.shape[-1])])
    wq = mla_w_q_up[0].reshape(MLA_Q_RANK, MLA_HEADS, MLA_QK)
    w_qn = wq[..., :MLA_NOPE].reshape(MLA_Q_RANK, MLA_HEADS * MLA_NOPE).astype(BF16)
    w_qr = lane_pad(wq[..., MLA_NOPE:]).reshape(MLA_Q_RANK, MLA_HEADS * LANES).astype(BF16)
    w_qrr = lane_pad(rot(wq[..., MLA_NOPE:])).reshape(MLA_Q_RANK, MLA_HEADS * LANES).astype(BF16)
    wkd_r = kv_w_down[:, MLA_KV_RANK:]
    w_kd = jnp.concatenate([kv_w_down[:, :MLA_KV_RANK], lane_pad(wkd_r), lane_pad(rot(wkd_r))],
                           axis=-1).astype(BF16)
    wku = kv_w_up.reshape(MLA_KV_RANK, MLA_HEADS, MLA_NOPE + MLA_V)
    w_kvu = jnp.concatenate([wku[..., :MLA_NOPE].reshape(MLA_KV_RANK, -1),
                             wku[..., MLA_NOPE:].reshape(MLA_KV_RANK, -1)], axis=-1).astype(BF16)

    inv = ROPE_THETA ** (-jnp.arange(0, MLA_ROPE, 2, dtype=F32) / MLA_ROPE)
    pos = (jnp.arange(Lp, dtype=jnp.int32) - FRONT_PAD).astype(F32)
    ang = pos[:, None] * inv[None, :]
    zpad = jnp.zeros((Lp, LANES - MLA_ROPE), F32)
    cosp = jnp.concatenate([jnp.cos(ang), jnp.cos(ang), zpad], axis=-1)
    sinp = jnp.concatenate([jnp.sin(ang), jnp.sin(ang), zpad], axis=-1)

    head_tile = lambda w: pl.BlockSpec((1, MLA_HEADS, T, w), lambda b, i: (b, 0, i, 0))
    q1, k1, v1, zs1 = pl.pallas_call(
        _mla_in_kernel,
        grid=(B, nt),
        in_specs=[tile_spec(D), _const_spec((1, D)), _const_spec((1, D)),
                  _const_spec((D, MLA_Q_RANK)), _const_spec((D, zw)), _const_spec((1, MLA_Q_RANK)),
                  _const_spec((MLA_Q_RANK, MLA_HEADS * MLA_NOPE)),
                  _const_spec((MLA_Q_RANK, MLA_HEADS * LANES)),
                  _const_spec((MLA_Q_RANK, MLA_HEADS * LANES)),
                  _const_spec((D, 3 * LANES)), _const_spec((1, MLA_KV_RANK)),
                  _const_spec((MLA_KV_RANK, MLA_HEADS * (MLA_NOPE + MLA_V))),
                  pl.BlockSpec((T, LANES), lambda b, i: (i, 0)),
                  pl.BlockSpec((T, LANES), lambda b, i: (i, 0))],
        out_specs=[pl.BlockSpec((1, MLA_HEADS, T, MLA_QK), lambda b, i: (b, 0, jnp.maximum(i - 1, 0), 0)),
                   head_tile(MLA_QK), head_tile(MLA_V),
                   pl.BlockSpec((1, T, zw), lambda b, i: (b, jnp.maximum(i - 1, 0), 0))],
        out_shape=[jax.ShapeDtypeStruct((B, MLA_HEADS, S, MLA_QK), BF16),
                   jax.ShapeDtypeStruct((B, MLA_HEADS, Lp, MLA_QK), BF16),
                   jax.ShapeDtypeStruct((B, MLA_HEADS, Lp, MLA_V), BF16),
                   jax.ShapeDtypeStruct((B, S, zw), BF16)],
        compiler_params=_params("arbitrary", "arbitrary"),
        name="mla_in",
    )(h1, row(pre_norm[1]), row(kv_norm), w_cq, w_z1, row(mla_q_latent_norm[0]), w_qn, w_qr, w_qrr,
      w_kd, row(kv_latent_norm), w_kvu, cosp, sinp)

    o_attn = pl.pallas_call(
        _attn_kernel,
        grid=(B, MLA_HEADS, S // ATTN_TQ),
        in_specs=[pl.BlockSpec((1, 1, ATTN_TQ, MLA_QK), lambda b, h, i: (b, h, i, 0)),
                  pl.BlockSpec((1, 1, Lp, MLA_QK), lambda b, h, i: (b, h, 0, 0)),
                  pl.BlockSpec((1, 1, Lp, MLA_V), lambda b, h, i: (b, h, 0, 0)),
                  pl.BlockSpec((1, ATTN_TQ, MLA_V), lambda b, h, i: (b, i, h))],
        out_specs=pl.BlockSpec((1, ATTN_TQ, MLA_V), lambda b, h, i: (b, i, h)),
        out_shape=jax.ShapeDtypeStruct((B, S, MLA_HEADS * MLA_V), BF16),
        scratch_shapes=[pltpu.VMEM((ATTN_TQ, LANES), F32), pltpu.VMEM((ATTN_TQ, LANES), F32),
                        pltpu.VMEM((ATTN_TQ, MLA_V), F32),
                        pltpu.VMEM((ATTN_TQ // ATTN_SUB, ATTN_SUB, ATTN_TK), F32),
                        pltpu.VMEM((ATTN_TQ // ATTN_SUB, ATTN_SUB, ATTN_TK), F32)],
        compiler_params=_params("parallel", "parallel", "arbitrary"),
        name="mla_attn",
    )(q1, k1, v1, zs1)

    out = pl.pallas_call(
        _out_proj_last_kernel,
        grid=(B, S // T),
        in_specs=[pl.BlockSpec((1, T, MLA_HEADS * MLA_V), lambda b, i: (b, i, 0)),
                  _const_spec((MLA_HEADS * MLA_V, D)), _const_spec((1, D)),
                  pl.BlockSpec((1, T, D), lambda b, i: (b, i + 1, 0))],
        out_specs=pl.BlockSpec((1, T, D), lambda b, i: (b, i, 0)),
        out_shape=jax.ShapeDtypeStruct((B, S, D), x.dtype),
        compiler_params=_params("parallel", "arbitrary"),
        name="mla_out",
    )(o_attn, mla_w_out[0].astype(BF16), row(post_norm[1]), h1)
    return out
```

```python
import functools
import math

import jax
import jax.numpy as jnp
from jax import lax
from jax.experimental import pallas as pl
from jax.experimental.pallas import tpu as pltpu

NORM_EPS = 1e-6
N_META_ROWS = 16

GDN_QK_HEADS = 8
GDN_V_HEADS = 16
GDN_HEAD = 128
GDN_CONV_TAPS = 4
GDN_CHUNK = 64
GDN_QK_W = GDN_QK_HEADS * GDN_HEAD
GDN_V_W = GDN_V_HEADS * GDN_HEAD
GDN_CONV_W = 2 * GDN_QK_W + GDN_V_W

MLA_HEADS = 16
MLA_NOPE = 128
MLA_ROPE = 64
MLA_V = 128
MLA_Q_RANK = 256
MLA_KV_RANK = 128
MLA_QK = MLA_NOPE + MLA_ROPE
ROPE_THETA = 10000.0

LANES = 128
ROW_TILE = 256
FRONT_PAD = ROW_TILE - N_META_ROWS
CONV_COLS = 512
HALO_ROWS = 8
ATTN_TQ = 1024
ATTN_SUB = 256
ATTN_TK = 512
ATTN_GROUP = 32
VMEM_LIMIT = 56 * 1024 * 1024

F32 = jnp.float32
BF16 = jnp.bfloat16


def _dot(a, b):
    return jnp.dot(a, b, preferred_element_type=F32)


def _dot_nt(a, b):
    return lax.dot_general(a, b, (((1,), (1,)), ((), ())), preferred_element_type=F32)


def _dot_tn(a, b):
    return lax.dot_general(a, b, (((0,), (0,)), ((), ())), preferred_element_type=F32)


def _dot_exact(a, b):
    return jnp.dot(a, b, preferred_element_type=F32, precision=lax.Precision.HIGHEST)


def _silu(x):
    return x * jax.nn.sigmoid(x)


def _softplus(x):
    return jnp.maximum(x, 0.0) + jnp.log1p(jnp.exp(-jnp.abs(x)))


def _rms_scale(x):
    return lax.rsqrt(jnp.mean(x * x, axis=-1, keepdims=True) + NORM_EPS)


def _const_spec(shape):
    nd = len(shape)
    return pl.BlockSpec(shape, lambda *_: (0,) * nd, pipeline_mode=pl.Buffered(1))


def _gdn_in_kernel(x_ref, head_ref, gain_ref, wqkv_ref, wz_ref, wba_ref, wbat_ref, convw_ref,
                   arow_ref, dtrow_ref, acol_ref, dtcol_ref,
                   q_ref, k_ref, v_ref, zs_ref, gcol_ref, grow_ref,
                   halo_scr, buf_scr):
    i = pl.program_id(1)
    x = jnp.where(i == 0, head_ref[...], x_ref[0])
    hn = (x * _rms_scale(x) * gain_ref[...]).astype(BF16)

    @pl.when(i == 0)
    def _():
        halo_scr[...] = jnp.zeros_like(halo_scr)

    lo = HALO_ROWS - (GDN_CONV_TAPS - 1)
    for c in range(GDN_CONV_W // CONV_COLS):
        cs = slice(c * CONV_COLS, (c + 1) * CONV_COLS)
        p = _dot(hn, wqkv_ref[:, cs])
        for hh in range(CONV_COLS // GDN_HEAD):
            col = c * CONV_COLS + hh * GDN_HEAD
            buf_scr[hh, 0:HALO_ROWS, :] = halo_scr[:, col:col + GDN_HEAD]
            buf_scr[hh, HALO_ROWS:HALO_ROWS + ROW_TILE, :] = p[:, hh * GDN_HEAD:(hh + 1) * GDN_HEAD]
            halo_scr[:, col:col + GDN_HEAD] = p[ROW_TILE - HALO_ROWS:, hh * GDN_HEAD:(hh + 1) * GDN_HEAD]
            yh = convw_ref[0:1, col:col + GDN_HEAD] * buf_scr[hh, lo:lo + ROW_TILE, :]
            for j in range(1, GDN_CONV_TAPS):
                yh = yh + convw_ref[j:j + 1, col:col + GDN_HEAD] * buf_scr[hh, lo + j:lo + j + ROW_TILE, :]
            yh = _silu(yh)
            if col < 2 * GDN_QK_W:
                r = lax.rsqrt(jnp.sum(yh * yh, axis=-1, keepdims=True) + NORM_EPS)
                if col < GDN_QK_W:
                    q_ref[0, :, col:col + GDN_HEAD] = (yh * (r * (GDN_HEAD ** -0.5))).astype(BF16)
                else:
                    k_ref[0, :, col - GDN_QK_W:col - GDN_QK_W + GDN_HEAD] = (yh * r).astype(BF16)
            else:
                v_ref[0, :, col - 2 * GDN_QK_W:col - 2 * GDN_QK_W + GDN_HEAD] = yh.astype(BF16)

    for c in range(GDN_V_W // CONV_COLS):
        cs = slice(c * CONV_COLS, (c + 1) * CONV_COLS)
        zs_ref[0, :, cs] = _silu(_dot(hn, wz_ref[:, cs])).astype(BF16)

    nh = GDN_V_HEADS
    ba = _dot(hn, wba_ref[...])
    bat = _dot_nt(wbat_ref[...], hn)
    beta_c = 1.0 / (1.0 + jnp.exp(-ba))
    g_c = -jnp.exp(arow_ref[...]) * _softplus(ba + dtrow_ref[...])
    beta_r = 1.0 / (1.0 + jnp.exp(-bat[0:nh, :]))
    g_r = -jnp.exp(acol_ref[...]) * _softplus(bat[nh:2 * nh, :] + dtcol_ref[...])

    ri = lax.broadcasted_iota(jnp.int32, (ROW_TILE, ROW_TILE), 0)
    ci = lax.broadcasted_iota(jnp.int32, (ROW_TILE, ROW_TILE), 1)
    same = (ri // GDN_CHUNK) == (ci // GDN_CHUNK)
    lower = jnp.where(same & (ri >= ci), 1.0, 0.0).astype(F32)
    upper = jnp.where(same & (ri <= ci), 1.0, 0.0).astype(F32)
    block = jnp.where(same, 1.0, 0.0).astype(F32)
    gc_c = _dot_exact(lower, g_c)
    gl_c = _dot_exact(block, g_c)
    gc_r = _dot_exact(g_r, upper)
    grp = lax.broadcasted_iota(jnp.int32, (ROW_TILE, LANES), 1) // nh
    gcol_ref[0] = jnp.where(grp == 0, beta_c,
                            jnp.where(grp == 1, gc_c,
                                      jnp.where(grp == 2, jnp.exp(gc_c),
                                                jnp.where(grp == 3, jnp.exp(gl_c - gc_c), 0.0))))
    npair = nh // 2
    for kind, xr in enumerate((gc_r, beta_r, beta_r * jnp.exp(gc_r))):
        for c in range(ROW_TILE // GDN_CHUNK):
            cs = slice(c * GDN_CHUNK, (c + 1) * GDN_CHUNK)
            grow_ref[0, c, kind * npair:(kind + 1) * npair, :] = jnp.concatenate(
                [xr[0:npair, cs], xr[npair:nh, cs]], axis=-1)


def _gdn_prep_kernel(q_ref, k_ref, gcol_ref, grow_ref, tb_ref, w_ref, a_ref, kd_ref):
    C = GDN_CHUNK
    nh = GDN_V_HEADS
    npair = nh // 2
    row = lax.broadcasted_iota(jnp.int32, (C, LANES), 0)
    lane = lax.broadcasted_iota(jnp.int32, (C, LANES), 1)
    left = lane < C
    col = jnp.where(left, lane, lane - C)
    incl = row >= col
    strict = row > col
    eye = jnp.where(row == col, 1.0, 0.0).astype(F32)
    diag8 = strict & ((row // 8) == (col // 8))
    merges = tuple(strict & ((row // (2 * s)) == (col // (2 * s))) & ((row // s) != (col // s))
                   for s in (8, 16, 32))
    zero_b = jnp.zeros((C, LANES), BF16)

    def blockdiag(x):
        xb = x.astype(BF16)
        return jnp.concatenate([jnp.where(left, xb, zero_b), jnp.where(left, zero_b, xb)], axis=0)

    def pair_cols(g, base, p_):
        return jnp.where(left, g[:, base + 2 * p_:base + 2 * p_ + 1], g[:, base + 2 * p_ + 1:base + 2 * p_ + 2])

    streams = [(c, p_) for c in range(ROW_TILE // C) for p_ in range(npair)]
    ms, kps, bege, betar = [], [], [], []
    for c, p_ in streams:
        rs = slice(c * C, (c + 1) * C)
        ps = slice(p_ * GDN_HEAD, (p_ + 1) * GDN_HEAD)
        qp = q_ref[0, rs, ps]
        kp = k_ref[0, rs, ps]
        g = gcol_ref[0, rs, :]
        both = _dot_nt(jnp.concatenate([qp, kp], axis=0), jnp.concatenate([kp, kp], axis=0))
        qk2, kk2 = both[0:C], both[C:2 * C]
        decay = jnp.exp(jnp.where(incl, pair_cols(g, nh, p_) - grow_ref[0, c, p_:p_ + 1, :], -jnp.inf))
        ms.append(jnp.where(strict, pair_cols(g, 0, p_) * kk2 * decay, 0.0))
        a_ref[0, rs, ps] = (qk2 * decay).astype(BF16)
        kpf = kp.astype(F32)
        for h in (2 * p_, 2 * p_ + 1):
            kd_ref[0, rs, h * GDN_HEAD:(h + 1) * GDN_HEAD] = (kpf * g[:, 3 * nh + h:3 * nh + h + 1]).astype(BF16)
        betar.append(grow_ref[0, c, npair + p_:npair + p_ + 1, :])
        kps.append(kp)
        bege.append(grow_ref[0, c, 2 * npair + p_:2 * npair + p_ + 1, :])

    m8 = [jnp.where(diag8, m, 0.0) for m in ms]
    q2 = [_dot(x.astype(BF16), blockdiag(x)) for x in m8]
    pinv = [eye - x for x in m8]
    pinv = [p + _dot(p.astype(BF16), blockdiag(y)) for p, y in zip(pinv, q2)]
    q4 = [_dot(y.astype(BF16), blockdiag(y)) for y in q2]
    pinv = [p + _dot(p.astype(BF16), blockdiag(y)) for p, y in zip(pinv, q4)]
    for mask in merges:
        cp = [_dot(jnp.where(mask, m, 0.0).astype(BF16), blockdiag(p)) for m, p in zip(ms, pinv)]
        pinv = [p - _dot(p.astype(BF16), blockdiag(y)) for p, y in zip(pinv, cp)]

    zero_k = jnp.zeros((C, GDN_HEAD), BF16)
    for (c, p_), p, kp, bg, br in zip(streams, pinv, kps, bege, betar):
        rs = slice(c * C, (c + 1) * C)
        ps = slice(p_ * GDN_HEAD, (p_ + 1) * GDN_HEAD)
        kbd = jnp.concatenate([jnp.concatenate([kp, zero_k], axis=1),
                               jnp.concatenate([zero_k, kp], axis=1)], axis=0)
        w_ref[0, rs, 2 * p_ * GDN_HEAD:(2 * p_ + 2) * GDN_HEAD] = _dot((p * bg).astype(BF16), kbd).astype(BF16)
        tb_ref[0, rs, ps] = (p * br).astype(BF16)


def _gdn_scan_kernel(tb_ref, w_ref, a_ref, kd_ref, q_ref, v_ref, zs_ref, gcol_ref, onorm_ref, o_ref, state_scr):
    @pl.when(pl.program_id(0) == 0)
    def _():
        state_scr[...] = jnp.zeros_like(state_scr)

    C = GDN_CHUNK
    nh = GDN_V_HEADS
    nb = q_ref.shape[0]
    onorm = onorm_ref[...]
    zero_v = jnp.zeros((C, GDN_HEAD), BF16)

    def blockdiag(x0, x1):
        return jnp.concatenate([jnp.concatenate([x0, zero_v], axis=1),
                                jnp.concatenate([zero_v, x1], axis=1)], axis=0)

    pairs = [(b, p_) for b in range(nb) for p_ in range(nh // 2)]
    heads = [(b, h) for b, p_ in pairs for h in (2 * p_, 2 * p_ + 1)]
    hsl = lambda h: slice(h * GDN_HEAD, (h + 1) * GDN_HEAD)

    u2 = [_dot(tb_ref[b, :, hsl(p_)], blockdiag(v_ref[b, :, hsl(2 * p_)], v_ref[b, :, hsl(2 * p_ + 1)]))
          for b, p_ in pairs]
    u = {(b, 2 * p_ + j): x[:, j * GDN_HEAD:(j + 1) * GDN_HEAD] for (b, p_), x in zip(pairs, u2) for j in (0, 1)}
    s_old = {bh: state_scr[bh[0], bh[1]] for bh in heads}
    wq = {(b, h): _dot(jnp.concatenate([w_ref[b, :, hsl(h)], q_ref[b, :, hsl(h // 2)]], axis=0),
                       s_old[(b, h)].astype(BF16)) for b, h in heads}
    vnb = {bh: (u[bh] - wq[bh][0:C]).astype(BF16) for bh in heads}
    o2 = [_dot(a_ref[b, :, hsl(p_)], blockdiag(vnb[(b, 2 * p_)], vnb[(b, 2 * p_ + 1)])) for b, p_ in pairs]
    for b, h in heads:
        eg_c = gcol_ref[b, :, 2 * nh + h:2 * nh + h + 1]
        state_scr[b, h] = s_old[(b, h)] * eg_c[C - 1:C, :] + _dot_tn(kd_ref[b, :, hsl(h)], vnb[(b, h)])
    for (b, p_), o2p in zip(pairs, o2):
        for j in (0, 1):
            h = 2 * p_ + j
            eg_c = gcol_ref[b, :, 2 * nh + h:2 * nh + h + 1]
            o = eg_c * wq[(b, h)][C:2 * C] + o2p[:, j * GDN_HEAD:(j + 1) * GDN_HEAD]
            on = o * _rms_scale(o) * onorm
            o_ref[b, :, hsl(h)] = (on * zs_ref[b, :, hsl(h)].astype(F32)).astype(BF16)


def _out_proj_first_kernel(o_ref, w_ref, gain_ref, x_ref, head_ref, h_ref):
    y = _dot(o_ref[0], w_ref[...])
    res = jnp.where(pl.program_id(1) == 0, head_ref[...], x_ref[0])
    h_ref[0] = res + y * _rms_scale(y) * gain_ref[...]


def _out_proj_last_kernel(o_ref, w_ref, gain_ref, h_ref, out_ref):
    y = _dot(o_ref[0], w_ref[...])
    out_ref[0] = h_ref[0] + y * _rms_scale(y) * gain_ref[...]


def _mla_in_kernel(h_ref, pre_ref, kvn_ref, wcq_ref, wz_ref, qln_ref, wqn_ref, wqr_ref, wqrr_ref,
                   wkd_ref, kvln_ref, wkvu_ref, cos_ref, sin_ref,
                   q_ref, kk_ref, vv_ref, zs_ref):
    h = h_ref[0]
    hr = h * _rms_scale(h)
    hn = (hr * pre_ref[...]).astype(BF16)
    hk = (hr * kvn_ref[...]).astype(BF16)
    cosp = cos_ref[...]
    sinp = sin_ref[...]

    for c in range(wz_ref.shape[1] // CONV_COLS):
        cs = slice(c * CONV_COLS, (c + 1) * CONV_COLS)
        zs_ref[0, :, cs] = _silu(_dot(hn, wz_ref[:, cs])).astype(BF16)

    cq = _dot(hn, wcq_ref[...])
    cq = (cq * _rms_scale(cq) * qln_ref[...]).astype(BF16)
    scale = MLA_QK ** -0.5 * math.log2(math.e)
    for hd in range(MLA_HEADS):
        hs = slice(hd * LANES, (hd + 1) * LANES)
        qn = _dot(cq, wqn_ref[:, hs])
        qr = _dot(cq, wqr_ref[:, hs]) * cosp + _dot(cq, wqrr_ref[:, hs]) * sinp
        q_ref[0, hd, :, 0:MLA_NOPE] = (qn * scale).astype(BF16)
        q_ref[0, hd, :, MLA_NOPE:MLA_QK] = (qr[:, 0:MLA_ROPE] * scale).astype(BF16)

    ckr = _dot(hk, wkd_ref[...])
    lat = ckr[:, 0:MLA_KV_RANK]
    ckv = (lat * _rms_scale(lat) * kvln_ref[...]).astype(BF16)
    kr = ckr[:, LANES:2 * LANES] * cosp + ckr[:, 2 * LANES:3 * LANES] * sinp
    krb = kr[:, 0:MLA_ROPE].astype(BF16)
    for hd in range(MLA_HEADS):
        kn = _dot(ckv, wkvu_ref[:, hd * MLA_NOPE:(hd + 1) * MLA_NOPE])
        vv = _dot(ckv, wkvu_ref[:, MLA_HEADS * MLA_NOPE + hd * MLA_V:MLA_HEADS * MLA_NOPE + (hd + 1) * MLA_V])
        kk_ref[0, hd, :, 0:MLA_NOPE] = kn.astype(BF16)
        kk_ref[0, hd, :, MLA_NOPE:MLA_QK] = krb
        vv_ref[0, hd] = vv.astype(BF16)


def _attn_kernel(q_ref, k_ref, v_ref, zs_ref, o_ref, m_scr, al_scr, alpha_scr, sa_scr, sb_scr, p_scr):
    i = pl.program_id(2)
    tq, sb, tk = ATTN_TQ, ATTN_SUB, ATTN_TK
    m_scr[...] = jnp.full_like(m_scr, -jnp.inf)
    al_scr[...] = jnp.zeros_like(al_scr)

    def rows(r):
        return slice(r * sb, (r + 1) * sb)

    def scores(r, k0, nk):
        return _dot_nt(q_ref[0, 0, rows(r), :], k_ref[0, 0, pl.ds(k0, nk), :])

    def values_and_ones(k0, nk):
        return jnp.concatenate([v_ref[0, 0, pl.ds(k0, nk), :], jnp.ones((nk, LANES), BF16)], axis=1)

    def absorb(r, parts):
        rs = rows(r)
        m_old = m_scr[rs, :]
        m_new = m_old
        for s, _, _ in parts:
            m_new = jnp.maximum(m_new, jnp.max(s, axis=-1, keepdims=True))
        alpha = jnp.exp2(m_old - m_new)
        al = jnp.concatenate([alpha, alpha], axis=1) * al_scr[rs, :]
        for s, k0, nk in parts:
            ps = [jnp.exp2(s[:, c * LANES:(c + 1) * LANES] - m_new) for c in range(nk // LANES)]
            al = al + _dot(jnp.concatenate(ps, axis=1).astype(BF16), values_and_ones(k0, nk))
        al_scr[rs, :] = al
        m_scr[rs, :] = m_new

    def absorb_block(r, s_buf, k0):
        for g in range(sb // ATTN_GROUP):
            gr = slice(g * ATTN_GROUP, (g + 1) * ATTN_GROUP)
            ar = slice(r * sb + g * ATTN_GROUP, r * sb + (g + 1) * ATTN_GROUP)
            s = s_buf[r, gr, :]
            m_old = m_scr[ar, :]
            m_new = jnp.maximum(m_old, jnp.max(s, axis=-1, keepdims=True))
            ps = [jnp.exp2(s[:, c * LANES:(c + 1) * LANES] - m_new) for c in range(tk // LANES)]
            p_scr[r, gr, :] = jnp.concatenate(ps, axis=1).astype(BF16)
            alpha_scr[ar, :] = jnp.exp2(m_old - m_new)
            m_scr[ar, :] = m_new
        rs = rows(r)
        alpha = alpha_scr[rs, :]
        al_scr[rs, :] = (jnp.concatenate([alpha, alpha], axis=1) * al_scr[rs, :]
                         + _dot(p_scr[r], values_and_ones(k0, tk)))

    nsub = tq // sb
    block0 = lambda j: pl.multiple_of(ROW_TILE + j * tk, ROW_TILE)

    ri = lax.broadcasted_iota(jnp.int32, (sb, sb), 0)
    ci = lax.broadcasted_iota(jnp.int32, (sb, sb), 1)
    base = pl.multiple_of(ROW_TILE + i * tq, ROW_TILE)
    meta0 = ROW_TILE - LANES
    is_meta = lax.broadcasted_iota(jnp.int32, (sb, LANES), 1) >= FRONT_PAD - meta0
    s_meta = [jnp.where(is_meta, scores(r, meta0, LANES), -jnp.inf) for r in range(nsub)]
    s_diag = [scores(r, base, (r + 1) * sb) for r in range(nsub)]
    for r in range(nsub):
        sa_scr[r] = scores(r, block0(0), tk)
    for r in range(nsub):
        tri = jnp.where(ci <= ri, s_diag[r][:, r * sb:], -jnp.inf)
        sd = tri if r == 0 else jnp.concatenate([s_diag[r][:, :r * sb], tri], axis=1)
        absorb(r, [(s_meta[r], meta0, LANES), (sd, base, (r + 1) * sb)])

    assert tq % (2 * tk) == 0

    def body(jj, carry):
        j = 2 * jj
        for cur, nxt, step in ((sa_scr, sb_scr, 0), (sb_scr, sa_scr, 1)):
            for r in range(nsub):
                nxt[r] = scores(r, block0(j + step + 1), tk)
                absorb_block(r, cur, block0(j + step))
        return carry

    lax.fori_loop(0, (i * tq) // (2 * tk), body, 0)

    o_ref[0] = (al_scr[:, 0:MLA_V] / al_scr[:, MLA_V:] * zs_ref[0].astype(F32)).astype(BF16)


def _params(*sem):
    return pltpu.CompilerParams(dimension_semantics=sem, vmem_limit_bytes=VMEM_LIMIT)


def kernel(x, meta_tokens, pre_norm, post_norm, gdn_w_in, gdn_conv_w, gdn_a_log, gdn_dt_bias, gdn_out_norm, gdn_w_out, kv_norm, kv_w_down, kv_latent_norm, kv_w_up, mla_w_in, mla_q_latent_norm, mla_w_q_up, mla_w_out):
    B, S, D = x.shape
    assert S % ATTN_TQ == 0 and S % ROW_TILE == 0
    assert gdn_w_in.shape[0] == 1 and mla_w_in.shape[0] == 1
    T = ROW_TILE
    nt = S // T + 1
    Lp = nt * T
    nch = Lp // GDN_CHUNK
    nh = GDN_V_HEADS

    head = jnp.concatenate([jnp.zeros((FRONT_PAD, D), F32), meta_tokens.astype(F32)], axis=0)
    row = lambda a: a.reshape(1, -1).astype(F32)

    w_in = gdn_w_in[0]
    w_qkv = w_in[:, :GDN_CONV_W].astype(BF16)
    w_z = w_in[:, GDN_CONV_W:GDN_CONV_W + GDN_V_W].astype(BF16)
    w_ba_f = w_in[:, GDN_CONV_W + GDN_V_W:]
    w_b, w_a = w_ba_f[:, :nh], w_ba_f[:, nh:]
    w_ba = jnp.concatenate([w_b, w_a, w_a, w_a, jnp.zeros((D, LANES - 4 * nh), F32)], axis=-1).astype(BF16)
    eo = lambda a: jnp.concatenate([a[0::2], a[1::2]], axis=0)
    w_bat = jnp.concatenate([eo(w_b.T), eo(w_a.T)], axis=0).astype(BF16)
    a_log = gdn_a_log[0].astype(F32)
    dt_b = gdn_dt_bias[0].astype(F32)
    lane_groups = lambda a: jnp.concatenate([jnp.zeros((nh,), F32), a, a, a,
                                             jnp.zeros((LANES - 4 * nh,), F32)]).reshape(1, LANES)

    x_spec = pl.BlockSpec((1, T, D), lambda b, i: (b, jnp.maximum(i - 1, 0), 0))
    tile_spec = lambda w: pl.BlockSpec((1, T, w), lambda b, i: (b, i, 0))

    q, k, v, zs, gcol, grow = pl.pallas_call(
        _gdn_in_kernel,
        grid=(B, nt),
        in_specs=[x_spec, _const_spec((T, D)), _const_spec((1, D)),
                  _const_spec((D, GDN_CONV_W)), _const_spec((D, GDN_V_W)),
                  _const_spec((D, LANES)), _const_spec((2 * nh, D)),
                  _const_spec((GDN_CONV_TAPS, GDN_CONV_W)),
                  _const_spec((1, LANES)), _const_spec((1, LANES)), _const_spec((nh, 1)), _const_spec((nh, 1))],
        out_specs=[tile_spec(GDN_QK_W), tile_spec(GDN_QK_W), tile_spec(GDN_V_W), tile_spec(GDN_V_W),
                   tile_spec(LANES),
                   pl.BlockSpec((1, T // GDN_CHUNK, 3 * nh // 2, LANES), lambda b, i: (b, i, 0, 0))],
        out_shape=[jax.ShapeDtypeStruct((B, Lp, GDN_QK_W), BF16),
                   jax.ShapeDtypeStruct((B, Lp, GDN_QK_W), BF16),
                   jax.ShapeDtypeStruct((B, Lp, GDN_V_W), BF16),
                   jax.ShapeDtypeStruct((B, Lp, GDN_V_W), BF16),
                   jax.ShapeDtypeStruct((B, Lp, LANES), F32),
                   jax.ShapeDtypeStruct((B, nch, 3 * nh // 2, LANES), F32)],
        scratch_shapes=[pltpu.VMEM((HALO_ROWS, GDN_CONV_W), F32),
                        pltpu.VMEM((CONV_COLS // GDN_HEAD, HALO_ROWS + T, GDN_HEAD), F32)],
        compiler_params=_params("arbitrary", "arbitrary"),
        name="gdn_in",
    )(x, head, row(pre_norm[0]), w_qkv, w_z, w_ba, w_bat, gdn_conv_w[0].astype(F32),
      lane_groups(a_log), lane_groups(dt_b), eo(a_log).reshape(nh, 1), eo(dt_b).reshape(nh, 1))

    tb, wk, aqk, kd = pl.pallas_call(
        _gdn_prep_kernel,
        grid=(B, nt),
        in_specs=[tile_spec(GDN_QK_W), tile_spec(GDN_QK_W), tile_spec(LANES),
                  pl.BlockSpec((1, T // GDN_CHUNK, 3 * nh // 2, LANES), lambda b, i: (b, i, 0, 0))],
        out_specs=[tile_spec(GDN_QK_W), tile_spec(GDN_V_W), tile_spec(GDN_QK_W), tile_spec(GDN_V_W)],
        out_shape=[jax.ShapeDtypeStruct((B, Lp, GDN_QK_W), BF16),
                   jax.ShapeDtypeStruct((B, Lp, GDN_V_W), BF16),
                   jax.ShapeDtypeStruct((B, Lp, GDN_QK_W), BF16),
                   jax.ShapeDtypeStruct((B, Lp, GDN_V_W), BF16)],
        compiler_params=_params("parallel", "parallel"),
        name="gdn_prep",
    )(q, k, gcol, grow)

    chunk_spec = lambda w: pl.BlockSpec((B, GDN_CHUNK, w), lambda n: (0, n, 0))
    o_gdn = pl.pallas_call(
        _gdn_scan_kernel,
        grid=(nch,),
        in_specs=[chunk_spec(GDN_QK_W), chunk_spec(GDN_V_W), chunk_spec(GDN_QK_W), chunk_spec(GDN_V_W),
                  chunk_spec(GDN_QK_W), chunk_spec(GDN_V_W), chunk_spec(GDN_V_W), chunk_spec(LANES),
                  _const_spec((1, GDN_HEAD))],
        out_specs=chunk_spec(GDN_V_W),
        out_shape=jax.ShapeDtypeStruct((B, Lp, GDN_V_W), BF16),
        scratch_shapes=[pltpu.VMEM((B, nh, GDN_HEAD, GDN_HEAD), F32)],
        compiler_params=_params("arbitrary"),
        name="gdn_scan",
    )(tb, wk, aqk, kd, q, v, zs, gcol, row(gdn_out_norm[0]))

    h1 = pl.pallas_call(
        _out_proj_first_kernel,
        grid=(B, nt),
        in_specs=[tile_spec(GDN_V_W), _const_spec((GDN_V_W, D)), _const_spec((1, D)),
                  x_spec, _const_spec((T, D))],
        out_specs=tile_spec(D),
        out_shape=jax.ShapeDtypeStruct((B, Lp, D), F32),
        compiler_params=_params("parallel", "arbitrary"),
        name="gdn_out",
    )(o_gdn, gdn_w_out[0].astype(BF16), row(post_norm[0]), x, head)

    w_in1 = mla_w_in[0]
    w_cq = w_in1[:, :MLA_Q_RANK].astype(BF16)
    w_z1 = w_in1[:, MLA_Q_RANK:].astype(BF16)
    zw = w_z1.shape[1]
    half = MLA_ROPE // 2
    rot = lambda w: jnp.concatenate([-w[..., half:], w[..., :half]], axis=-1)
    lane_pad = lambda w: jnp.pad(w, [(0, 0)] * (w.ndim - 1) + [(0, LANES - w.shape[-1])])
    wq = mla_w_q_up[0].reshape(MLA_Q_RANK, MLA_HEADS, MLA_QK)
    w_qn = wq[..., :MLA_NOPE].reshape(MLA_Q_RANK, MLA_HEADS * MLA_NOPE).astype(BF16)
    w_qr = lane_pad(wq[..., MLA_NOPE:]).reshape(MLA_Q_RANK, MLA_HEADS * LANES).astype(BF16)
    w_qrr = lane_pad(rot(wq[..., MLA_NOPE:])).reshape(MLA_Q_RANK, MLA_HEADS * LANES).astype(BF16)
    wkd_r = kv_w_down[:, MLA_KV_RANK:]
    w_kd = jnp.concatenate([kv_w_down[:, :MLA_KV_RANK], lane_pad(wkd_r), lane_pad(rot(wkd_r))],
                           axis=-1).astype(BF16)
    wku = kv_w_up.reshape(MLA_KV_RANK, MLA_HEADS, MLA_NOPE + MLA_V)
    w_kvu = jnp.concatenate([wku[..., :MLA_NOPE].reshape(MLA_KV_RANK, -1),
                             wku[..., MLA_NOPE:].reshape(MLA_KV_RANK, -1)], axis=-1).astype(BF16)

    inv = ROPE_THETA ** (-jnp.arange(0, MLA_ROPE, 2, dtype=F32) / MLA_ROPE)
    pos = (jnp.arange(Lp, dtype=jnp.int32) - FRONT_PAD).astype(F32)
    ang = pos[:, None] * inv[None, :]
    zpad = jnp.zeros((Lp, LANES - MLA_ROPE), F32)
    cosp = jnp.concatenate([jnp.cos(ang), jnp.cos(ang), zpad], axis=-1)
    sinp = jnp.concatenate([jnp.sin(ang), jnp.sin(ang), zpad], axis=-1)

    head_tile = lambda w: pl.BlockSpec((1, MLA_HEADS, T, w), lambda b, i: (b, 0, i, 0))
    q1, k1, v1, zs1 = pl.pallas_call(
        _mla_in_kernel,
        grid=(B, nt),
        in_specs=[tile_spec(D), _const_spec((1, D)), _const_spec((1, D)),
                  _const_spec((D, MLA_Q_RANK)), _const_spec((D, zw)), _const_spec((1, MLA_Q_RANK)),
                  _const_spec((MLA_Q_RANK, MLA_HEADS * MLA_NOPE)),
                  _const_spec((MLA_Q_RANK, MLA_HEADS * LANES)),
                  _const_spec((MLA_Q_RANK, MLA_HEADS * LANES)),
                  _const_spec((D, 3 * LANES)), _const_spec((1, MLA_KV_RANK)),
                  _const_spec((MLA_KV_RANK, MLA_HEADS * (MLA_NOPE + MLA_V))),
                  pl.BlockSpec((T, LANES), lambda b, i: (i, 0)),
                  pl.BlockSpec((T, LANES), lambda b, i: (i, 0))],
        out_specs=[pl.BlockSpec((1, MLA_HEADS, T, MLA_QK), lambda b, i: (b, 0, jnp.maximum(i - 1, 0), 0)),
                   head_tile(MLA_QK), head_tile(MLA_V),
                   pl.BlockSpec((1, T, zw), lambda b, i: (b, jnp.maximum(i - 1, 0), 0))],
        out_shape=[jax.ShapeDtypeStruct((B, MLA_HEADS, S, MLA_QK), BF16),
                   jax.ShapeDtypeStruct((B, MLA_HEADS, Lp, MLA_QK), BF16),
                   jax.ShapeDtypeStruct((B, MLA_HEADS, Lp, MLA_V), BF16),
                   jax.ShapeDtypeStruct((B, S, zw), BF16)],
        compiler_params=_params("arbitrary", "arbitrary"),
        name="mla_in",
    )(h1, row(pre_norm[1]), row(kv_norm), w_cq, w_z1, row(mla_q_latent_norm[0]), w_qn, w_qr, w_qrr,
      w_kd, row(kv_latent_norm), w_kvu, cosp, sinp)

    o_attn = pl.pallas_call(
        _attn_kernel,
        grid=(B, MLA_HEADS, S // ATTN_TQ),
        in_specs=[pl.BlockSpec((1, 1, ATTN_TQ, MLA_QK), lambda b, h, i: (b, h, i, 0)),
                  pl.BlockSpec((1, 1, Lp, MLA_QK), lambda b, h, i: (b, h, 0, 0)),
                  pl.BlockSpec((1, 1, Lp, MLA_V), lambda b, h, i: (b, h, 0, 0)),
                  pl.BlockSpec((1, ATTN_TQ, MLA_V), lambda b, h, i: (b, i, h))],
        out_specs=pl.BlockSpec((1, ATTN_TQ, MLA_V), lambda b, h, i: (b, i, h)),
        out_shape=jax.ShapeDtypeStruct((B, S, MLA_HEADS * MLA_V), BF16),
        scratch_shapes=[pltpu.VMEM((ATTN_TQ, LANES), F32), pltpu.VMEM((ATTN_TQ, MLA_V + LANES), F32),
                        pltpu.VMEM((ATTN_TQ, LANES), F32),
                        pltpu.VMEM((ATTN_TQ // ATTN_SUB, ATTN_SUB, ATTN_TK), F32),
                        pltpu.VMEM((ATTN_TQ // ATTN_SUB, ATTN_SUB, ATTN_TK), F32),
                        pltpu.VMEM((ATTN_TQ // ATTN_SUB, ATTN_SUB, ATTN_TK), BF16)],
        compiler_params=_params("parallel", "parallel", "arbitrary"),
        name="mla_attn",
    )(q1, k1, v1, zs1)

    out = pl.pallas_call(
        _out_proj_last_kernel,
        grid=(B, S // T),
        in_specs=[pl.BlockSpec((1, T, MLA_HEADS * MLA_V), lambda b, i: (b, i, 0)),
                  _const_spec((MLA_HEADS * MLA_V, D)), _const_spec((1, D)),
                  pl.BlockSpec((1, T, D), lambda b, i: (b, i + 1, 0))],
        out_specs=pl.BlockSpec((1, T, D), lambda b, i: (b, i, 0)),
        out_shape=jax.ShapeDtypeStruct((B, S, D), x.dtype),
        compiler_params=_params("parallel", "arbitrary"),
        name="mla_out",
    )(o_attn, mla_w_out[0].astype(BF16), row(post_norm[1]), h1)
    return out
```

```python
import functools
import math

import jax
import jax.numpy as jnp
from jax import lax
from jax.experimental import pallas as pl
from jax.experimental.pallas import tpu as pltpu

NORM_EPS = 1e-6
N_META_ROWS = 16

GDN_QK_HEADS = 8
GDN_V_HEADS = 16
GDN_HEAD = 128
GDN_CONV_TAPS = 4
GDN_CHUNK = 64
GDN_QK_W = GDN_QK_HEADS * GDN_HEAD
GDN_V_W = GDN_V_HEADS * GDN_HEAD
GDN_CONV_W = 2 * GDN_QK_W + GDN_V_W

MLA_HEADS = 16
MLA_NOPE = 128
MLA_ROPE = 64
MLA_V = 128
MLA_Q_RANK = 256
MLA_KV_RANK = 128
MLA_QK = MLA_NOPE + MLA_ROPE
ROPE_THETA = 10000.0

LANES = 128
ROW_TILE = 256
FRONT_PAD = ROW_TILE - N_META_ROWS
CONV_COLS = 512
HALO_ROWS = 8
ATTN_TQ = 2048
ATTN_SUB = 256
ATTN_TK = 1024
ATTN_GROUP = 16
VMEM_LIMIT = 56 * 1024 * 1024

F32 = jnp.float32
BF16 = jnp.bfloat16


def _dot(a, b):
    return jnp.dot(a, b, preferred_element_type=F32)


def _dot_nt(a, b):
    return lax.dot_general(a, b, (((1,), (1,)), ((), ())), preferred_element_type=F32)


def _dot_tn(a, b):
    return lax.dot_general(a, b, (((0,), (0,)), ((), ())), preferred_element_type=F32)


def _dot_exact(a, b):
    return jnp.dot(a, b, preferred_element_type=F32, precision=lax.Precision.HIGHEST)


def _silu(x):
    return x * jax.nn.sigmoid(x)


def _softplus(x):
    return jnp.maximum(x, 0.0) + jnp.log1p(jnp.exp(-jnp.abs(x)))


def _rms_scale(x):
    return lax.rsqrt(jnp.mean(x * x, axis=-1, keepdims=True) + NORM_EPS)


def _const_spec(shape):
    nd = len(shape)
    return pl.BlockSpec(shape, lambda *_: (0,) * nd, pipeline_mode=pl.Buffered(1))


def _gdn_in_kernel(x_ref, head_ref, gain_ref, win_ref, wba_ref, wbat_ref, convw_ref,
                   arow_ref, dtrow_ref, acol_ref, dtcol_ref,
                   q_ref, k_ref, v_ref, zs_ref, gcol_ref, grow_ref, egl_ref,
                   halo_scr, buf_scr):
    i = pl.program_id(1)
    x = jnp.where(i == 0, head_ref[...], x_ref[0])
    hn = (x * _rms_scale(x) * gain_ref[...]).astype(BF16)

    @pl.when(i == 0)
    def _():
        halo_scr[...] = jnp.zeros_like(halo_scr)

    lo = HALO_ROWS - (GDN_CONV_TAPS - 1)
    for c in range(GDN_CONV_W // CONV_COLS):
        cs = slice(c * CONV_COLS, (c + 1) * CONV_COLS)
        p = _dot(hn, win_ref[:, cs])
        for hh in range(CONV_COLS // GDN_HEAD):
            col = c * CONV_COLS + hh * GDN_HEAD
            buf_scr[hh, 0:HALO_ROWS, :] = halo_scr[:, col:col + GDN_HEAD]
            buf_scr[hh, HALO_ROWS:HALO_ROWS + ROW_TILE, :] = p[:, hh * GDN_HEAD:(hh + 1) * GDN_HEAD]
            halo_scr[:, col:col + GDN_HEAD] = p[ROW_TILE - HALO_ROWS:, hh * GDN_HEAD:(hh + 1) * GDN_HEAD]
            yh = convw_ref[0:1, col:col + GDN_HEAD] * buf_scr[hh, lo:lo + ROW_TILE, :]
            for j in range(1, GDN_CONV_TAPS):
                yh = yh + convw_ref[j:j + 1, col:col + GDN_HEAD] * buf_scr[hh, lo + j:lo + j + ROW_TILE, :]
            yh = _silu(yh)
            if col < 2 * GDN_QK_W:
                r = lax.rsqrt(jnp.sum(yh * yh, axis=-1, keepdims=True) + NORM_EPS)
                if col < GDN_QK_W:
                    q_ref[0, :, col:col + GDN_HEAD] = (yh * (r * (GDN_HEAD ** -0.5))).astype(BF16)
                else:
                    k_ref[0, :, col - GDN_QK_W:col - GDN_QK_W + GDN_HEAD] = (yh * r).astype(BF16)
            else:
                v_ref[0, :, col - 2 * GDN_QK_W:col - 2 * GDN_QK_W + GDN_HEAD] = yh.astype(BF16)

    for c in range(GDN_V_W // CONV_COLS):
        cs = slice(c * CONV_COLS, (c + 1) * CONV_COLS)
        ws = slice(GDN_CONV_W + c * CONV_COLS, GDN_CONV_W + (c + 1) * CONV_COLS)
        zs_ref[0, :, cs] = _silu(_dot(hn, win_ref[:, ws])).astype(BF16)

    nh = GDN_V_HEADS
    ba = _dot(hn, wba_ref[...])
    bat = _dot_nt(wbat_ref[...], hn)
    beta_c = 1.0 / (1.0 + jnp.exp(-ba))
    g_c = -jnp.exp(arow_ref[...]) * _softplus(ba + dtrow_ref[...])
    beta_r = 1.0 / (1.0 + jnp.exp(-bat[0:nh, :]))
    g_r = -jnp.exp(acol_ref[...]) * _softplus(bat[nh:2 * nh, :] + dtcol_ref[...])

    ri = lax.broadcasted_iota(jnp.int32, (ROW_TILE, ROW_TILE), 0)
    ci = lax.broadcasted_iota(jnp.int32, (ROW_TILE, ROW_TILE), 1)
    same = (ri // GDN_CHUNK) == (ci // GDN_CHUNK)
    lower = jnp.where(same & (ri >= ci), 1.0, 0.0).astype(F32)
    upper = jnp.where(same & (ri <= ci), 1.0, 0.0).astype(F32)
    block = jnp.where(same, 1.0, 0.0).astype(F32)
    gc_c = _dot_exact(lower, g_c)
    gl_c = _dot_exact(block, g_c)
    gc_r = _dot_exact(g_r, upper)
    grp = lax.broadcasted_iota(jnp.int32, (ROW_TILE, LANES), 1) // nh
    gcol_ref[0] = jnp.where(grp == 0, beta_c,
                            jnp.where(grp == 1, gc_c,
                                      jnp.where(grp == 2, jnp.exp(gc_c),
                                                jnp.where(grp == 3, jnp.exp(gl_c - gc_c), 0.0))))
    npair = nh // 2
    for kind, xr in enumerate((gc_r, beta_r, beta_r * jnp.exp(gc_r))):
        for c in range(ROW_TILE // GDN_CHUNK):
            cs = slice(c * GDN_CHUNK, (c + 1) * GDN_CHUNK)
            grow_ref[0, c, kind * npair:(kind + 1) * npair, :] = jnp.concatenate(
                [xr[0:npair, cs], xr[npair:nh, cs]], axis=-1)
    for c in range(ROW_TILE // GDN_CHUNK):
        cs = slice(c * GDN_CHUNK, (c + 1) * GDN_CHUNK)
        last = (c + 1) * GDN_CHUNK - 1
        gl = jnp.broadcast_to(gc_r[:, last:last + 1], (nh, LANES))
        egl_ref[0, c] = jnp.exp(gl)
        to_end = jnp.exp(gl[:, 0:GDN_CHUNK] - gc_r[:, cs])
        grow_ref[0, c, 3 * npair:4 * npair, :] = jnp.concatenate([to_end[0:npair], to_end[npair:nh]], axis=-1)


def _gdn_prep_kernel(q_ref, k_ref, gcol_ref, grow_ref, tb_ref, w_ref, a_ref, kdt_ref):
    C = GDN_CHUNK
    nh = GDN_V_HEADS
    npair = nh // 2
    row = lax.broadcasted_iota(jnp.int32, (C, LANES), 0)
    lane = lax.broadcasted_iota(jnp.int32, (C, LANES), 1)
    left = lane < C
    col = jnp.where(left, lane, lane - C)
    incl = row >= col
    strict = row > col
    eye = jnp.where(row == col, 1.0, 0.0).astype(F32)
    diag8 = strict & ((row // 8) == (col // 8))
    merges = tuple(strict & ((row // (2 * s)) == (col // (2 * s))) & ((row // s) != (col // s))
                   for s in (8, 16, 32))
    zero_b = jnp.zeros((C, LANES), BF16)

    def blockdiag(x):
        xb = x.astype(BF16)
        return jnp.concatenate([jnp.where(left, xb, zero_b), jnp.where(left, zero_b, xb)], axis=0)

    def pair_cols(g, base, p_):
        return jnp.take_along_axis(g, jnp.where(left, base + 2 * p_, base + 2 * p_ + 1), axis=1)

    streams = [(c, p_) for c in range(ROW_TILE // C) for p_ in range(npair)]
    ms, kps, bege, betar = [], [], [], []
    for c, p_ in streams:
        rs = slice(c * C, (c + 1) * C)
        ps = slice(p_ * GDN_HEAD, (p_ + 1) * GDN_HEAD)
        qp = q_ref[0, rs, ps]
        kp = k_ref[0, rs, ps]
        g = gcol_ref[0, rs, :]
        both = _dot_nt(jnp.concatenate([qp, kp], axis=0), jnp.concatenate([kp, kp], axis=0))
        qk2, kk2 = both[0:C], both[C:2 * C]
        decay = jnp.exp(jnp.where(incl, pair_cols(g, nh, p_) - grow_ref[0, c, p_:p_ + 1, :], -jnp.inf))
        ms.append(jnp.where(strict, pair_cols(g, 0, p_) * kk2 * decay, 0.0))
        a_ref[0, rs, ps] = (qk2 * decay).astype(BF16)
        kpf = kp.astype(F32)
        kdt_ref[0, c, p_] = (jnp.concatenate([kpf, kpf], axis=0).T
                             * grow_ref[0, c, 3 * npair + p_:3 * npair + p_ + 1, :]).astype(BF16)
        betar.append(grow_ref[0, c, npair + p_:npair + p_ + 1, :])
        kps.append(kp)
        bege.append(grow_ref[0, c, 2 * npair + p_:2 * npair + p_ + 1, :])

    m8 = [jnp.where(diag8, m, 0.0) for m in ms]
    q2 = [_dot(x.astype(BF16), blockdiag(x)) for x in m8]
    pinv = [eye - x for x in m8]
    pinv = [p + _dot(p.astype(BF16), blockdiag(y)) for p, y in zip(pinv, q2)]
    q4 = [_dot(y.astype(BF16), blockdiag(y)) for y in q2]
    pinv = [p + _dot(p.astype(BF16), blockdiag(y)) for p, y in zip(pinv, q4)]
    for mask in merges:
        cp = [_dot(jnp.where(mask, m, 0.0).astype(BF16), blockdiag(p)) for m, p in zip(ms, pinv)]
        pinv = [p - _dot(p.astype(BF16), blockdiag(y)) for p, y in zip(pinv, cp)]

    zero_k = jnp.zeros((C, GDN_HEAD), BF16)
    for (c, p_), p, kp, bg, br in zip(streams, pinv, kps, bege, betar):
        rs = slice(c * C, (c + 1) * C)
        ps = slice(p_ * GDN_HEAD, (p_ + 1) * GDN_HEAD)
        kbd = jnp.concatenate([jnp.concatenate([kp, zero_k], axis=1),
                               jnp.concatenate([zero_k, kp], axis=1)], axis=0)
        w_ref[0, rs, 2 * p_ * GDN_HEAD:(2 * p_ + 2) * GDN_HEAD] = _dot((p * bg).astype(BF16), kbd).astype(BF16)
        tb_ref[0, rs, ps] = (p * br).astype(BF16)


def _gdn_scan_kernel(tb_ref, w_ref, a_ref, kdt_ref, q_ref, v_ref, zs_ref, gcol_ref, egl_ref, onorm_ref, o_ref,
                     state_scr):
    @pl.when(pl.program_id(0) == 0)
    def _():
        state_scr[...] = jnp.zeros_like(state_scr)

    C = GDN_CHUNK
    nh = GDN_V_HEADS
    nb = q_ref.shape[0]
    onorm = onorm_ref[...]
    zero_v = jnp.zeros((C, GDN_HEAD), BF16)
    ones_sq = jnp.ones((GDN_HEAD, GDN_HEAD), BF16)

    def blockdiag(x0, x1):
        return jnp.concatenate([jnp.concatenate([x0, zero_v], axis=1),
                                jnp.concatenate([zero_v, x1], axis=1)], axis=0)

    pairs = [(b, p_) for b in range(nb) for p_ in range(nh // 2)]
    heads = [(b, h) for b, p_ in pairs for h in (2 * p_, 2 * p_ + 1)]
    hsl = lambda h: slice(h * GDN_HEAD, (h + 1) * GDN_HEAD)

    u2 = [_dot(tb_ref[b, :, hsl(p_)], blockdiag(v_ref[b, :, hsl(2 * p_)], v_ref[b, :, hsl(2 * p_ + 1)]))
          for b, p_ in pairs]
    u = {(b, 2 * p_ + j): x[:, j * GDN_HEAD:(j + 1) * GDN_HEAD] for (b, p_), x in zip(pairs, u2) for j in (0, 1)}
    s_old = {bh: state_scr[bh[0], bh[1]] for bh in heads}
    wq = {(b, h): _dot(jnp.concatenate([w_ref[b, :, hsl(h)], q_ref[b, :, hsl(h // 2)]], axis=0),
                       s_old[(b, h)].astype(BF16)) for b, h in heads}
    vnb = {bh: (u[bh] - wq[bh][0:C]).astype(BF16) for bh in heads}
    od = [_dot(jnp.concatenate([a_ref[b, :, hsl(p_)], kdt_ref[b, 0, p_]], axis=0),
               blockdiag(vnb[(b, 2 * p_)], vnb[(b, 2 * p_ + 1)])) for b, p_ in pairs]
    os = {}
    for (b, p_), x in zip(pairs, od):
        for j in (0, 1):
            h = 2 * p_ + j
            js = slice(j * GDN_HEAD, (j + 1) * GDN_HEAD)
            erow = (h % 2) * (nh // 2) + h // 2
            state_scr[b, h] = s_old[(b, h)] * egl_ref[b, 0, erow:erow + 1, :] + x[C:, js]
            os[(b, h)] = gcol_ref[b, :, 2 * nh + h:2 * nh + h + 1] * wq[(b, h)][C:2 * C] + x[0:C, js]
    sq = {bh: _dot((o * o).astype(BF16), ones_sq) for bh, o in os.items()}
    for (b, h), o in os.items():
        on = o * lax.rsqrt(sq[(b, h)] * (1.0 / GDN_HEAD) + NORM_EPS) * onorm
        o_ref[b, :, hsl(h)] = (on * zs_ref[b, :, hsl(h)].astype(F32)).astype(BF16)


def _out_proj_first_kernel(o_ref, w_ref, gain_ref, x_ref, head_ref, h_ref):
    y = _dot(o_ref[0], w_ref[...])
    res = jnp.where(pl.program_id(1) == 0, head_ref[...], x_ref[0])
    h_ref[0] = res + y * _rms_scale(y) * gain_ref[...]


def _out_proj_last_kernel(o_ref, w_ref, gain_ref, h_ref, out_ref):
    y = _dot(o_ref[0], w_ref[...])
    out_ref[0] = h_ref[0] + y * _rms_scale(y) * gain_ref[...]


def _mla_in_kernel(h_ref, pre_ref, kvn_ref, win_ref, qln_ref, wqn_ref, wqr_ref, wqrr_ref,
                   wkd_ref, kvln_ref, wkvu_ref, cos_ref, sin_ref,
                   q_ref, kk_ref, vv_ref, zs_ref):
    h = h_ref[0]
    hr = h * _rms_scale(h)
    hn = (hr * pre_ref[...]).astype(BF16)
    hk = (hr * kvn_ref[...]).astype(BF16)
    cosp = cos_ref[...]
    sinp = sin_ref[...]

    for c in range((win_ref.shape[1] - MLA_Q_RANK) // CONV_COLS):
        cs = slice(c * CONV_COLS, (c + 1) * CONV_COLS)
        ws = slice(MLA_Q_RANK + c * CONV_COLS, MLA_Q_RANK + (c + 1) * CONV_COLS)
        zs_ref[0, :, cs] = _silu(_dot(hn, win_ref[:, ws])).astype(BF16)

    cq = _dot(hn, win_ref[:, 0:MLA_Q_RANK])
    cq = (cq * _rms_scale(cq) * qln_ref[...]).astype(BF16)
    scale = MLA_QK ** -0.5 * math.log2(math.e)
    for hd in range(MLA_HEADS):
        hs = slice(hd * LANES, (hd + 1) * LANES)
        qn = _dot(cq, wqn_ref[:, hs])
        qr = _dot(cq, wqr_ref[:, hs]) * cosp + _dot(cq, wqrr_ref[:, hs]) * sinp
        q_ref[0, hd, :, 0:MLA_NOPE] = (qn * scale).astype(BF16)
        q_ref[0, hd, :, MLA_NOPE:MLA_QK] = (qr[:, 0:MLA_ROPE] * scale).astype(BF16)

    ckr = _dot(hk, wkd_ref[...])
    lat = ckr[:, 0:MLA_KV_RANK]
    ckv = (lat * _rms_scale(lat) * kvln_ref[...]).astype(BF16)
    kr = ckr[:, LANES:2 * LANES] * cosp + ckr[:, 2 * LANES:3 * LANES] * sinp
    krb = kr[:, 0:MLA_ROPE].astype(BF16)
    for hd in range(MLA_HEADS):
        kn = _dot(ckv, wkvu_ref[:, hd * MLA_NOPE:(hd + 1) * MLA_NOPE])
        vv = _dot(ckv, wkvu_ref[:, MLA_HEADS * MLA_NOPE + hd * MLA_V:MLA_HEADS * MLA_NOPE + (hd + 1) * MLA_V])
        kk_ref[0, hd, :, 0:MLA_NOPE] = kn.astype(BF16)
        kk_ref[0, hd, :, MLA_NOPE:MLA_QK] = krb
        vv_ref[0, hd] = vv.astype(BF16)


def _attn_kernel(q_ref, k_ref, v_ref, zs_ref, o_ref, m_scr, al_scr, alpha_scr, sa_scr, sb_scr, p_scr):
    i = pl.program_id(2)
    tq, sb, tk = ATTN_TQ, ATTN_SUB, ATTN_TK
    m_scr[...] = jnp.full_like(m_scr, -jnp.inf)
    al_scr[...] = jnp.zeros_like(al_scr)

    def rows(r):
        return slice(r * sb, (r + 1) * sb)

    def scores(r, k0, nk):
        return _dot_nt(q_ref[0, 0, rows(r), :], k_ref[0, 0, pl.ds(k0, nk), :])

    def values_and_ones(k0, nk):
        return jnp.concatenate([v_ref[0, 0, pl.ds(k0, nk), :], jnp.ones((nk, LANES), BF16)], axis=1)

    def absorb(r, parts):
        rs = rows(r)
        m_old = m_scr[rs, :]
        m_new = m_old
        for s, _, _ in parts:
            m_new = jnp.maximum(m_new, jnp.max(s, axis=-1, keepdims=True))
        alpha = jnp.exp2(m_old - m_new)
        al = jnp.concatenate([alpha, alpha], axis=1) * al_scr[rs, :]
        for s, k0, nk in parts:
            ps = [jnp.exp2(s[:, c * LANES:(c + 1) * LANES] - m_new) for c in range(nk // LANES)]
            al = al + _dot(jnp.concatenate(ps, axis=1).astype(BF16), values_and_ones(k0, nk))
        al_scr[rs, :] = al
        m_scr[rs, :] = m_new

    def absorb_block(r, s_buf, k0):
        for g in range(sb // ATTN_GROUP):
            gr = slice(g * ATTN_GROUP, (g + 1) * ATTN_GROUP)
            ar = slice(r * sb + g * ATTN_GROUP, r * sb + (g + 1) * ATTN_GROUP)
            s = s_buf[r, gr, :]
            m_old = m_scr[ar, :]
            m_new = jnp.maximum(m_old, jnp.max(s, axis=-1, keepdims=True))
            ps = [jnp.exp2(s[:, c * LANES:(c + 1) * LANES] - m_new) for c in range(tk // LANES)]
            p_scr[r, gr, :] = jnp.concatenate(ps, axis=1).astype(BF16)
            alpha_scr[ar, :] = jnp.exp2(m_old - m_new)
            m_scr[ar, :] = m_new
        rs = rows(r)
        alpha = alpha_scr[rs, :]
        al_scr[rs, :] = (jnp.concatenate([alpha, alpha], axis=1) * al_scr[rs, :]
                         + _dot(p_scr[r], values_and_ones(k0, tk)))

    nsub = tq // sb
    block0 = lambda j: pl.multiple_of(ROW_TILE + j * tk, ROW_TILE)

    ri = lax.broadcasted_iota(jnp.int32, (sb, sb), 0)
    ci = lax.broadcasted_iota(jnp.int32, (sb, sb), 1)
    base = pl.multiple_of(ROW_TILE + i * tq, ROW_TILE)
    meta0 = ROW_TILE - LANES
    is_meta = lax.broadcasted_iota(jnp.int32, (sb, LANES), 1) >= FRONT_PAD - meta0
    s_meta = [jnp.where(is_meta, scores(r, meta0, LANES), -jnp.inf) for r in range(nsub)]
    s_diag = [scores(r, base, (r + 1) * sb) for r in range(nsub)]
    for r in range(nsub):
        sa_scr[r] = scores(r, block0(0), tk)
    for r in range(nsub):
        tri = jnp.where(ci <= ri, s_diag[r][:, r * sb:], -jnp.inf)
        sd = tri if r == 0 else jnp.concatenate([s_diag[r][:, :r * sb], tri], axis=1)
        absorb(r, [(s_meta[r], meta0, LANES), (sd, base, (r + 1) * sb)])

    assert tq % (2 * tk) == 0

    def body(jj, carry):
        j = 2 * jj
        for cur, nxt, step in ((sa_scr, sb_scr, 0), (sb_scr, sa_scr, 1)):
            for r in range(nsub):
                nxt[r] = scores(r, block0(j + step + 1), tk)
                absorb_block(r, cur, block0(j + step))
        return carry

    lax.fori_loop(0, (i * tq) // (2 * tk), body, 0)

    o_ref[0] = (al_scr[:, 0:MLA_V] / al_scr[:, MLA_V:] * zs_ref[0].astype(F32)).astype(BF16)


def _params(*sem):
    return pltpu.CompilerParams(dimension_semantics=sem, vmem_limit_bytes=VMEM_LIMIT)


def kernel(x, meta_tokens, pre_norm, post_norm, gdn_w_in, gdn_conv_w, gdn_a_log, gdn_dt_bias, gdn_out_norm, gdn_w_out, kv_norm, kv_w_down, kv_latent_norm, kv_w_up, mla_w_in, mla_q_latent_norm, mla_w_q_up, mla_w_out):
    B, S, D = x.shape
    assert S % ATTN_TQ == 0 and S % ROW_TILE == 0
    assert gdn_w_in.shape[0] == 1 and mla_w_in.shape[0] == 1
    T = ROW_TILE
    nt = S // T + 1
    Lp = nt * T
    nch = Lp // GDN_CHUNK
    nh = GDN_V_HEADS

    head = jnp.concatenate([jnp.zeros((FRONT_PAD, D), F32), meta_tokens.astype(F32)], axis=0)
    row = lambda a: a.reshape(1, -1).astype(F32)

    w_in = gdn_w_in[0]
    w_ba_f = w_in[:, GDN_CONV_W + GDN_V_W:]
    w_b, w_a = w_ba_f[:, :nh], w_ba_f[:, nh:]
    w_ba = jnp.concatenate([w_b, w_a, w_a, w_a, jnp.zeros((D, LANES - 4 * nh), F32)], axis=-1).astype(BF16)
    eo = lambda a: jnp.concatenate([a[0::2], a[1::2]], axis=0)
    w_bat = jnp.concatenate([eo(w_b.T), eo(w_a.T)], axis=0).astype(BF16)
    a_log = gdn_a_log[0].astype(F32)
    dt_b = gdn_dt_bias[0].astype(F32)
    lane_groups = lambda a: jnp.concatenate([jnp.zeros((nh,), F32), a, a, a,
                                             jnp.zeros((LANES - 4 * nh,), F32)]).reshape(1, LANES)

    x_spec = pl.BlockSpec((1, T, D), lambda b, i: (b, jnp.maximum(i - 1, 0), 0))
    tile_spec = lambda w: pl.BlockSpec((1, T, w), lambda b, i: (b, i, 0))

    gate_spec = lambda r: pl.BlockSpec((1, T // GDN_CHUNK, r, LANES), lambda b, i: (b, i, 0, 0))
    q, k, v, zs, gcol, grow, egl = pl.pallas_call(
        _gdn_in_kernel,
        grid=(B, nt),
        in_specs=[x_spec, _const_spec((T, D)), _const_spec((1, D)),
                  _const_spec(w_in.shape), _const_spec((D, LANES)), _const_spec((2 * nh, D)),
                  _const_spec((GDN_CONV_TAPS, GDN_CONV_W)),
                  _const_spec((1, LANES)), _const_spec((1, LANES)), _const_spec((nh, 1)), _const_spec((nh, 1))],
        out_specs=[tile_spec(GDN_QK_W), tile_spec(GDN_QK_W), tile_spec(GDN_V_W), tile_spec(GDN_V_W),
                   tile_spec(LANES), gate_spec(2 * nh), gate_spec(nh)],
        out_shape=[jax.ShapeDtypeStruct((B, Lp, GDN_QK_W), BF16),
                   jax.ShapeDtypeStruct((B, Lp, GDN_QK_W), BF16),
                   jax.ShapeDtypeStruct((B, Lp, GDN_V_W), BF16),
                   jax.ShapeDtypeStruct((B, Lp, GDN_V_W), BF16),
                   jax.ShapeDtypeStruct((B, Lp, LANES), F32),
                   jax.ShapeDtypeStruct((B, nch, 2 * nh, LANES), F32),
                   jax.ShapeDtypeStruct((B, nch, nh, LANES), F32)],
        scratch_shapes=[pltpu.VMEM((HALO_ROWS, GDN_CONV_W), F32),
                        pltpu.VMEM((CONV_COLS // GDN_HEAD, HALO_ROWS + T, GDN_HEAD), F32)],
        compiler_params=_params("arbitrary", "arbitrary"),
        name="gdn_in",
    )(x, head, row(pre_norm[0]), w_in.astype(BF16), w_ba, w_bat, gdn_conv_w[0].astype(F32),
      lane_groups(a_log), lane_groups(dt_b), eo(a_log).reshape(nh, 1), eo(dt_b).reshape(nh, 1))

    npair = nh // 2
    tb, wk, aqk, kdt = pl.pallas_call(
        _gdn_prep_kernel,
        grid=(B, nt),
        in_specs=[tile_spec(GDN_QK_W), tile_spec(GDN_QK_W), tile_spec(LANES), gate_spec(2 * nh)],
        out_specs=[tile_spec(GDN_QK_W), tile_spec(GDN_V_W), tile_spec(GDN_QK_W),
                   pl.BlockSpec((1, T // GDN_CHUNK, npair, GDN_HEAD, LANES), lambda b, i: (b, i, 0, 0, 0))],
        out_shape=[jax.ShapeDtypeStruct((B, Lp, GDN_QK_W), BF16),
                   jax.ShapeDtypeStruct((B, Lp, GDN_V_W), BF16),
                   jax.ShapeDtypeStruct((B, Lp, GDN_QK_W), BF16),
                   jax.ShapeDtypeStruct((B, nch, npair, GDN_HEAD, LANES), BF16)],
        compiler_params=_params("parallel", "parallel"),
        name="gdn_prep",
    )(q, k, gcol, grow)

    chunk_spec = lambda w: pl.BlockSpec((B, GDN_CHUNK, w), lambda n: (0, n, 0))
    o_gdn = pl.pallas_call(
        _gdn_scan_kernel,
        grid=(nch,),
        in_specs=[chunk_spec(GDN_QK_W), chunk_spec(GDN_V_W), chunk_spec(GDN_QK_W),
                  pl.BlockSpec((B, 1, npair, GDN_HEAD, LANES), lambda n: (0, n, 0, 0, 0)),
                  chunk_spec(GDN_QK_W), chunk_spec(GDN_V_W), chunk_spec(GDN_V_W), chunk_spec(LANES),
                  pl.BlockSpec((B, 1, nh, LANES), lambda n: (0, n, 0, 0)),
                  _const_spec((1, GDN_HEAD))],
        out_specs=chunk_spec(GDN_V_W),
        out_shape=jax.ShapeDtypeStruct((B, Lp, GDN_V_W), BF16),
        scratch_shapes=[pltpu.VMEM((B, nh, GDN_HEAD, GDN_HEAD), F32)],
        compiler_params=_params("arbitrary"),
        name="gdn_scan",
    )(tb, wk, aqk, kdt, q, v, zs, gcol, egl, row(gdn_out_norm[0]))

    h1 = pl.pallas_call(
        _out_proj_first_kernel,
        grid=(B, nt),
        in_specs=[tile_spec(GDN_V_W), _const_spec((GDN_V_W, D)), _const_spec((1, D)),
                  x_spec, _const_spec((T, D))],
        out_specs=tile_spec(D),
        out_shape=jax.ShapeDtypeStruct((B, Lp, D), F32),
        compiler_params=_params("parallel", "arbitrary"),
        name="gdn_out",
    )(o_gdn, gdn_w_out[0].astype(BF16), row(post_norm[0]), x, head)

    w_in1 = mla_w_in[0]
    zw = w_in1.shape[1] - MLA_Q_RANK
    half = MLA_ROPE // 2
    rot = lambda w: jnp.concatenate([-w[..., half:], w[..., :half]], axis=-1)
    lane_pad = lambda w: jnp.pad(w, [(0, 0)] * (w.ndim - 1) + [(0, LANES - w.shape[-1])])
    wq = mla_w_q_up[0].reshape(MLA_Q_RANK, MLA_HEADS, MLA_QK)
    w_qn = wq[..., :MLA_NOPE].reshape(MLA_Q_RANK, MLA_HEADS * MLA_NOPE).astype(BF16)
    w_qr = lane_pad(wq[..., MLA_NOPE:]).reshape(MLA_Q_RANK, MLA_HEADS * LANES).astype(BF16)
    w_qrr = lane_pad(rot(wq[..., MLA_NOPE:])).reshape(MLA_Q_RANK, MLA_HEADS * LANES).astype(BF16)
    wkd_r = kv_w_down[:, MLA_KV_RANK:]
    w_kd = jnp.concatenate([kv_w_down[:, :MLA_KV_RANK], lane_pad(wkd_r), lane_pad(rot(wkd_r))],
                           axis=-1).astype(BF16)
    wku = kv_w_up.reshape(MLA_KV_RANK, MLA_HEADS, MLA_NOPE + MLA_V)
    w_kvu = jnp.concatenate([wku[..., :MLA_NOPE].reshape(MLA_KV_RANK, -1),
                             wku[..., MLA_NOPE:].reshape(MLA_KV_RANK, -1)], axis=-1).astype(BF16)

    inv = ROPE_THETA ** (-jnp.arange(0, MLA_ROPE, 2, dtype=F32) / MLA_ROPE)
    pos = (jnp.arange(Lp, dtype=jnp.int32) - FRONT_PAD).astype(F32)
    ang = pos[:, None] * inv[None, :]
    zpad = jnp.zeros((Lp, LANES - MLA_ROPE), F32)
    cosp = jnp.concatenate([jnp.cos(ang), jnp.cos(ang), zpad], axis=-1)
    sinp = jnp.concatenate([jnp.sin(ang), jnp.sin(ang), zpad], axis=-1)

    head_tile = lambda w: pl.BlockSpec((1, MLA_HEADS, T, w), lambda b, i: (b, 0, i, 0))
    q1, k1, v1, zs1 = pl.pallas_call(
        _mla_in_kernel,
        grid=(B, nt),
        in_specs=[tile_spec(D), _const_spec((1, D)), _const_spec((1, D)),
                  _const_spec(w_in1.shape), _const_spec((1, MLA_Q_RANK)),
                  _const_spec((MLA_Q_RANK, MLA_HEADS * MLA_NOPE)),
                  _const_spec((MLA_Q_RANK, MLA_HEADS * LANES)),
                  _const_spec((MLA_Q_RANK, MLA_HEADS * LANES)),
                  _const_spec((D, 3 * LANES)), _const_spec((1, MLA_KV_RANK)),
                  _const_spec((MLA_KV_RANK, MLA_HEADS * (MLA_NOPE + MLA_V))),
                  pl.BlockSpec((T, LANES), lambda b, i: (i, 0)),
                  pl.BlockSpec((T, LANES), lambda b, i: (i, 0))],
        out_specs=[pl.BlockSpec((1, MLA_HEADS, T, MLA_QK), lambda b, i: (b, 0, jnp.maximum(i - 1, 0), 0)),
                   head_tile(MLA_QK), head_tile(MLA_V),
                   pl.BlockSpec((1, T, zw), lambda b, i: (b, jnp.maximum(i - 1, 0), 0))],
        out_shape=[jax.ShapeDtypeStruct((B, MLA_HEADS, S, MLA_QK), BF16),
                   jax.ShapeDtypeStruct((B, MLA_HEADS, Lp, MLA_QK), BF16),
                   jax.ShapeDtypeStruct((B, MLA_HEADS, Lp, MLA_V), BF16),
                   jax.ShapeDtypeStruct((B, S, zw), BF16)],
        compiler_params=_params("arbitrary", "arbitrary"),
        name="mla_in",
    )(h1, row(pre_norm[1]), row(kv_norm), w_in1.astype(BF16), row(mla_q_latent_norm[0]), w_qn, w_qr, w_qrr,
      w_kd, row(kv_latent_norm), w_kvu, cosp, sinp)

    o_attn = pl.pallas_call(
        _attn_kernel,
        grid=(B, MLA_HEADS, S // ATTN_TQ),
        in_specs=[pl.BlockSpec((1, 1, ATTN_TQ, MLA_QK), lambda b, h, i: (b, h, i, 0)),
                  pl.BlockSpec((1, 1, Lp, MLA_QK), lambda b, h, i: (b, h, 0, 0)),
                  pl.BlockSpec((1, 1, Lp, MLA_V), lambda b, h, i: (b, h, 0, 0)),
                  pl.BlockSpec((1, ATTN_TQ, MLA_V), lambda b, h, i: (b, i, h))],
        out_specs=pl.BlockSpec((1, ATTN_TQ, MLA_V), lambda b, h, i: (b, i, h)),
        out_shape=jax.ShapeDtypeStruct((B, S, MLA_HEADS * MLA_V), BF16),
        scratch_shapes=[pltpu.VMEM((ATTN_TQ, LANES), F32), pltpu.VMEM((ATTN_TQ, MLA_V + LANES), F32),
                        pltpu.VMEM((ATTN_TQ, LANES), F32),
                        pltpu.VMEM((ATTN_TQ // ATTN_SUB, ATTN_SUB, ATTN_TK), F32),
                        pltpu.VMEM((ATTN_TQ // ATTN_SUB, ATTN_SUB, ATTN_TK), F32),
                        pltpu.VMEM((ATTN_TQ // ATTN_SUB, ATTN_SUB, ATTN_TK), BF16)],
        compiler_params=_params("parallel", "parallel", "arbitrary"),
        name="mla_attn",
    )(q1, k1, v1, zs1)

    out = pl.pallas_call(
        _out_proj_last_kernel,
        grid=(B, S // T),
        in_specs=[pl.BlockSpec((1, T, MLA_HEADS * MLA_V), lambda b, i: (b, i, 0)),
                  _const_spec((MLA_HEADS * MLA_V, D)), _const_spec((1, D)),
                  pl.BlockSpec((1, T, D), lambda b, i: (b, i + 1, 0))],
        out_specs=pl.BlockSpec((1, T, D), lambda b, i: (b, i, 0)),
        out_shape=jax.ShapeDtypeStruct((B, S, D), x.dtype),
        compiler_params=_params("parallel", "arbitrary"),
        name="mla_out",
    )(o_attn, mla_w_out[0].astype(BF16), row(post_norm[1]), h1)
    return out
```

```python
import functools
import math

import jax
import jax.numpy as jnp
from jax import lax
from jax.experimental import pallas as pl
from jax.experimental.pallas import tpu as pltpu

NORM_EPS = 1e-6
N_META_ROWS = 16

GDN_QK_HEADS = 8
GDN_V_HEADS = 16
GDN_HEAD = 128
GDN_CONV_TAPS = 4
GDN_CHUNK = 64
GDN_QK_W = GDN_QK_HEADS * GDN_HEAD
GDN_V_W = GDN_V_HEADS * GDN_HEAD
GDN_CONV_W = 2 * GDN_QK_W + GDN_V_W

MLA_HEADS = 16
MLA_NOPE = 128
MLA_ROPE = 64
MLA_V = 128
MLA_Q_RANK = 256
MLA_KV_RANK = 128
MLA_QK = MLA_NOPE + MLA_ROPE
ROPE_THETA = 10000.0

LANES = 128
ROW_TILE = 256
FRONT_PAD = ROW_TILE - N_META_ROWS
WIDE_TILES = 3
OUT_TILES = 4
CONV_COLS = 512
HALO_ROWS = 8
ATTN_TQ = 2048
ATTN_SUB = 256
ATTN_TK = 1024
ATTN_GROUP = 16
VMEM_LIMIT = 56 * 1024 * 1024

F32 = jnp.float32
BF16 = jnp.bfloat16


def _dot(a, b):
    return jnp.dot(a, b, preferred_element_type=F32)


def _dot_nt(a, b):
    return lax.dot_general(a, b, (((1,), (1,)), ((), ())), preferred_element_type=F32)


def _dot_tn(a, b):
    return lax.dot_general(a, b, (((0,), (0,)), ((), ())), preferred_element_type=F32)


def _dot_exact(a, b):
    return jnp.dot(a, b, preferred_element_type=F32, precision=lax.Precision.HIGHEST)


def _silu(x):
    return x * jax.nn.sigmoid(x)


def _softplus(x):
    return jnp.maximum(x, 0.0) + jnp.log1p(jnp.exp(-jnp.abs(x)))


def _rms_scale(x):
    return lax.rsqrt(jnp.mean(x * x, axis=-1, keepdims=True) + NORM_EPS)


def _const_spec(shape):
    nd = len(shape)
    return pl.BlockSpec(shape, lambda *_: (0,) * nd, pipeline_mode=pl.Buffered(1))


def _gdn_in_kernel(head_ref, gain_ref, win_ref, wba_ref, wbat_ref, convw_ref,
                   arow_ref, dtrow_ref, acol_ref, dtcol_ref, *refs):
    x_refs = refs[:WIDE_TILES]
    outs = refs[WIDE_TILES:WIDE_TILES + 7]
    halo_scr, buf_scr = refs[WIDE_TILES + 7:]
    i = pl.program_id(1)

    @pl.when(i == 0)
    def _():
        halo_scr[...] = jnp.zeros_like(halo_scr)

    ri = lax.broadcasted_iota(jnp.int32, (ROW_TILE, ROW_TILE), 0)
    ci = lax.broadcasted_iota(jnp.int32, (ROW_TILE, ROW_TILE), 1)
    same = (ri // GDN_CHUNK) == (ci // GDN_CHUNK)
    prefix = (jnp.where(same & (ri >= ci), 1.0, 0.0).astype(F32),
              jnp.where(same & (ri <= ci), 1.0, 0.0).astype(F32),
              jnp.where(same, 1.0, 0.0).astype(F32))
    for t, x_ref in enumerate(x_refs):
        x = x_ref[0] if t else jnp.where(i == 0, head_ref[...], x_ref[0])
        _gdn_in_tile(t, x, gain_ref, win_ref, wba_ref, wbat_ref, convw_ref,
                     arow_ref, dtrow_ref, acol_ref, dtcol_ref, prefix, outs, halo_scr, buf_scr)


def _gdn_in_tile(t, x, gain_ref, win_ref, wba_ref, wbat_ref, convw_ref,
                 arow_ref, dtrow_ref, acol_ref, dtcol_ref, prefix, outs, halo_scr, buf_scr):
    q_ref, k_ref, v_ref, zs_ref, gcol_ref, grow_ref, egl_ref = outs
    lower, upper, block = prefix
    rt = slice(t * ROW_TILE, (t + 1) * ROW_TILE)
    c0 = t * (ROW_TILE // GDN_CHUNK)
    hn = (x * _rms_scale(x) * gain_ref[...]).astype(BF16)

    lo = HALO_ROWS - (GDN_CONV_TAPS - 1)
    for c in range(GDN_CONV_W // CONV_COLS):
        cs = slice(c * CONV_COLS, (c + 1) * CONV_COLS)
        p = _dot(hn, win_ref[:, cs])
        for hh in range(CONV_COLS // GDN_HEAD):
            col = c * CONV_COLS + hh * GDN_HEAD
            buf_scr[hh, 0:HALO_ROWS, :] = halo_scr[:, col:col + GDN_HEAD]
            buf_scr[hh, HALO_ROWS:HALO_ROWS + ROW_TILE, :] = p[:, hh * GDN_HEAD:(hh + 1) * GDN_HEAD]
            halo_scr[:, col:col + GDN_HEAD] = p[ROW_TILE - HALO_ROWS:, hh * GDN_HEAD:(hh + 1) * GDN_HEAD]
            yh = convw_ref[0:1, col:col + GDN_HEAD] * buf_scr[hh, lo:lo + ROW_TILE, :]
            for j in range(1, GDN_CONV_TAPS):
                yh = yh + convw_ref[j:j + 1, col:col + GDN_HEAD] * buf_scr[hh, lo + j:lo + j + ROW_TILE, :]
            yh = _silu(yh)
            if col < 2 * GDN_QK_W:
                r = lax.rsqrt(jnp.sum(yh * yh, axis=-1, keepdims=True) + NORM_EPS)
                if col < GDN_QK_W:
                    q_ref[0, rt, col:col + GDN_HEAD] = (yh * (r * (GDN_HEAD ** -0.5))).astype(BF16)
                else:
                    k_ref[0, rt, col - GDN_QK_W:col - GDN_QK_W + GDN_HEAD] = (yh * r).astype(BF16)
            else:
                v_ref[0, rt, col - 2 * GDN_QK_W:col - 2 * GDN_QK_W + GDN_HEAD] = yh.astype(BF16)

    for c in range(GDN_V_W // CONV_COLS):
        cs = slice(c * CONV_COLS, (c + 1) * CONV_COLS)
        ws = slice(GDN_CONV_W + c * CONV_COLS, GDN_CONV_W + (c + 1) * CONV_COLS)
        zs_ref[0, rt, cs] = _silu(_dot(hn, win_ref[:, ws])).astype(BF16)

    nh = GDN_V_HEADS
    ba = _dot(hn, wba_ref[...])
    bat = _dot_nt(wbat_ref[...], hn)
    beta_c = 1.0 / (1.0 + jnp.exp(-ba))
    g_c = -jnp.exp(arow_ref[...]) * _softplus(ba + dtrow_ref[...])
    beta_r = 1.0 / (1.0 + jnp.exp(-bat[0:nh, :]))
    g_r = -jnp.exp(acol_ref[...]) * _softplus(bat[nh:2 * nh, :] + dtcol_ref[...])

    gc_c = _dot_exact(lower, g_c)
    gl_c = _dot_exact(block, g_c)
    gc_r = _dot_exact(g_r, upper)
    grp = lax.broadcasted_iota(jnp.int32, (ROW_TILE, LANES), 1) // nh
    gcol_ref[0, rt, :] = jnp.where(grp == 0, beta_c,
                                   jnp.where(grp == 1, gc_c,
                                             jnp.where(grp == 2, jnp.exp(gc_c),
                                                       jnp.where(grp == 3, jnp.exp(gl_c - gc_c), 0.0))))
    npair = nh // 2
    for kind, xr in enumerate((gc_r, beta_r, beta_r * jnp.exp(gc_r))):
        for c in range(ROW_TILE // GDN_CHUNK):
            cs = slice(c * GDN_CHUNK, (c + 1) * GDN_CHUNK)
            grow_ref[0, c0 + c, kind * npair:(kind + 1) * npair, :] = jnp.concatenate(
                [xr[0:npair, cs], xr[npair:nh, cs]], axis=-1)
    for c in range(ROW_TILE // GDN_CHUNK):
        cs = slice(c * GDN_CHUNK, (c + 1) * GDN_CHUNK)
        last = (c + 1) * GDN_CHUNK - 1
        gl = jnp.broadcast_to(gc_r[:, last:last + 1], (nh, LANES))
        egl_ref[0, c0 + c] = jnp.exp(gl)
        to_end = jnp.exp(gl[:, 0:GDN_CHUNK] - gc_r[:, cs])
        grow_ref[0, c0 + c, 3 * npair:4 * npair, :] =jnp.concatenate([to_end[0:npair], to_end[npair:nh]], axis=-1)


def _gdn_prep_kernel(q_ref, k_ref, gcol_ref, grow_ref, tb_ref, w_ref, a_ref, kdt_ref):
    C = GDN_CHUNK
    nh = GDN_V_HEADS
    npair = nh // 2
    row = lax.broadcasted_iota(jnp.int32, (C, LANES), 0)
    lane = lax.broadcasted_iota(jnp.int32, (C, LANES), 1)
    left = lane < C
    col = jnp.where(left, lane, lane - C)
    incl = row >= col
    strict = row > col
    eye = jnp.where(row == col, 1.0, 0.0).astype(F32)
    diag8 = strict & ((row // 8) == (col // 8))
    merges = tuple(strict & ((row // (2 * s)) == (col // (2 * s))) & ((row // s) != (col // s))
                   for s in (8, 16, 32))
    zero_b = jnp.zeros((C, LANES), BF16)

    def blockdiag(x):
        xb = x.astype(BF16)
        return jnp.concatenate([jnp.where(left, xb, zero_b), jnp.where(left, zero_b, xb)], axis=0)

    def pair_cols(g, base, p_):
        return jnp.take_along_axis(g, jnp.where(left, base + 2 * p_, base + 2 * p_ + 1), axis=1)

    cpt = ROW_TILE // C
    streams = [(c, p_) for c in range(cpt) for p_ in range(npair)]
    zero_k = jnp.zeros((C, GDN_HEAD), BF16)

    def row_tile(j, carry):
        rows = lambda c: pl.ds(pl.multiple_of(j * ROW_TILE + c * C, C), C)
        ms, kps, bege, betar = [], [], [], []
        for c, p_ in streams:
            rs, cj = rows(c), j * cpt + c
            ps = slice(p_ * GDN_HEAD, (p_ + 1) * GDN_HEAD)
            qp = q_ref[0, rs, ps]
            kp = k_ref[0, rs, ps]
            g = gcol_ref[0, rs, :]
            both = _dot_nt(jnp.concatenate([qp, kp], axis=0), jnp.concatenate([kp, kp], axis=0))
            qk2, kk2 = both[0:C], both[C:2 * C]
            decay = jnp.exp(jnp.where(incl, pair_cols(g, nh, p_) - grow_ref[0, cj, p_:p_ + 1, :], -jnp.inf))
            ms.append(jnp.where(strict, pair_cols(g, 0, p_) * kk2 * decay, 0.0))
            a_ref[0, rs, ps] = (qk2 * decay).astype(BF16)
            kpf = kp.astype(F32)
            kdt_ref[0, cj, p_] = (jnp.concatenate([kpf, kpf], axis=0).T
                                  * grow_ref[0, cj, 3 * npair + p_:3 * npair + p_ + 1, :]).astype(BF16)
            betar.append(grow_ref[0, cj, npair + p_:npair + p_ + 1, :])
            kps.append(kp)
            bege.append(grow_ref[0, cj, 2 * npair + p_:2 * npair + p_ + 1, :])

        m8 = [jnp.where(diag8, m, 0.0) for m in ms]
        q2 = [_dot(x.astype(BF16), blockdiag(x)) for x in m8]
        pinv = [eye - x for x in m8]
        pinv = [p + _dot(p.astype(BF16), blockdiag(y)) for p, y in zip(pinv, q2)]
        q4 = [_dot(y.astype(BF16), blockdiag(y)) for y in q2]
        pinv = [p + _dot(p.astype(BF16), blockdiag(y)) for p, y in zip(pinv, q4)]
        for mask in merges:
            cp = [_dot(jnp.where(mask, m, 0.0).astype(BF16), blockdiag(p)) for m, p in zip(ms, pinv)]
            pinv = [p - _dot(p.astype(BF16), blockdiag(y)) for p, y in zip(pinv, cp)]

        for (c, p_), p, kp, bg, br in zip(streams, pinv, kps, bege, betar):
            rs = rows(c)
            ps = slice(p_ * GDN_HEAD, (p_ + 1) * GDN_HEAD)
            kbd = jnp.concatenate([jnp.concatenate([kp, zero_k], axis=1),
                                   jnp.concatenate([zero_k, kp], axis=1)], axis=0)
            w_ref[0, rs, 2 * p_ * GDN_HEAD:(2 * p_ + 2) * GDN_HEAD] = _dot((p * bg).astype(BF16), kbd).astype(BF16)
            tb_ref[0, rs, ps] = (p * br).astype(BF16)
        return carry

    lax.fori_loop(0, q_ref.shape[1] // ROW_TILE, row_tile, 0)


def _gdn_scan_kernel(tb_ref, w_ref, a_ref, kdt_ref, q_ref, v_ref, zs_ref, gcol_ref, egl_ref, onorm_ref, o_ref,
                     state_scr):
    @pl.when(pl.program_id(0) == 0)
    def _():
        state_scr[...] = jnp.zeros_like(state_scr)

    C = GDN_CHUNK
    nh = GDN_V_HEADS
    nb = q_ref.shape[0]
    onorm = onorm_ref[...]
    zero_v = jnp.zeros((C, GDN_HEAD), BF16)
    ones_sq = jnp.ones((GDN_HEAD, GDN_HEAD), BF16)

    def blockdiag(x0, x1):
        return jnp.concatenate([jnp.concatenate([x0, zero_v], axis=1),
                                jnp.concatenate([zero_v, x1], axis=1)], axis=0)

    pairs = [(b, p_) for b in range(nb) for p_ in range(nh // 2)]
    heads = [(b, h) for b, p_ in pairs for h in (2 * p_, 2 * p_ + 1)]
    hsl = lambda h: slice(h * GDN_HEAD, (h + 1) * GDN_HEAD)

    def chunk(c, carry):
        rs = pl.ds(pl.multiple_of(c * C, C), C)
        u2 = [_dot(tb_ref[b, rs, hsl(p_)], blockdiag(v_ref[b, rs, hsl(2 * p_)], v_ref[b, rs, hsl(2 * p_ + 1)]))
              for b, p_ in pairs]
        u = {(b, 2 * p_ + j): x[:, j * GDN_HEAD:(j + 1) * GDN_HEAD]
             for (b, p_), x in zip(pairs, u2) for j in (0, 1)}
        s_old = {bh: state_scr[bh[0], bh[1]] for bh in heads}
        wq = {(b, h): _dot(jnp.concatenate([w_ref[b, rs, hsl(h)], q_ref[b, rs, hsl(h // 2)]], axis=0),
                           s_old[(b, h)].astype(BF16)) for b, h in heads}
        vnb = {bh: (u[bh] - wq[bh][0:C]).astype(BF16) for bh in heads}
        od = [_dot(jnp.concatenate([a_ref[b, rs, hsl(p_)], kdt_ref[b, c, p_]], axis=0),
                   blockdiag(vnb[(b, 2 * p_)], vnb[(b, 2 * p_ + 1)])) for b, p_ in pairs]
        os = {}
        for (b, p_), x in zip(pairs, od):
            for j in (0, 1):
                h = 2 * p_ + j
                js = slice(j * GDN_HEAD, (j + 1) * GDN_HEAD)
                erow = (h % 2) * (nh // 2) + h // 2
                state_scr[b, h] = s_old[(b, h)] * egl_ref[b, c, erow:erow + 1, :] + x[C:, js]
                os[(b, h)] = gcol_ref[b, rs, 2 * nh + h:2 * nh + h + 1] * wq[(b, h)][C:2 * C] + x[0:C, js]
        sq = {bh: _dot((o * o).astype(BF16), ones_sq) for bh, o in os.items()}
        for (b, h), o in os.items():
            on = o * lax.rsqrt(sq[(b, h)] * (1.0 / GDN_HEAD) + NORM_EPS) * onorm
            o_ref[b, rs, hsl(h)] = (on * zs_ref[b, rs, hsl(h)].astype(F32)).astype(BF16)
        return carry

    lax.fori_loop(0, q_ref.shape[1] // C, chunk, 0)


def _out_proj_first_kernel(o_ref, w_ref, gain_ref, head_ref, *refs):
    x_refs, h_ref = refs[:-1], refs[-1]
    y = _dot(o_ref[0], w_ref[...])
    yn = y * _rms_scale(y) * gain_ref[...]
    for j, x_ref in enumerate(x_refs):
        res = x_ref[0] if j else jnp.where(pl.program_id(1) == 0, head_ref[...], x_ref[0])
        h_ref[0, j * ROW_TILE:(j + 1) * ROW_TILE, :] = res + yn[j * ROW_TILE:(j + 1) * ROW_TILE]


def _out_proj_last_kernel(o_ref, w_ref, gain_ref, *refs):
    h_refs, out_ref = refs[:-1], refs[-1]
    y = _dot(o_ref[0], w_ref[...])
    yn = y * _rms_scale(y) * gain_ref[...]
    for j, h_ref in enumerate(h_refs):
        out_ref[0, j * ROW_TILE:(j + 1) * ROW_TILE, :] = h_ref[0] + yn[j * ROW_TILE:(j + 1) * ROW_TILE]


def _mla_in_kernel(h_ref, pre_ref, kvn_ref, win_ref, qln_ref, wqn_ref, wqr_ref, wqrr_ref,
                   wkd_ref, kvln_ref, wkvu_ref, cos_ref, sin_ref,
                   q_ref, kk_ref, vv_ref, zs_ref):
    h = h_ref[0]
    hr = h * _rms_scale(h)
    hn = (hr * pre_ref[...]).astype(BF16)
    hk = (hr * kvn_ref[...]).astype(BF16)
    cosp = cos_ref[...]
    sinp = sin_ref[...]

    for c in range((win_ref.shape[1] - MLA_Q_RANK) // CONV_COLS):
        cs = slice(c * CONV_COLS, (c + 1) * CONV_COLS)
        ws = slice(MLA_Q_RANK + c * CONV_COLS, MLA_Q_RANK + (c + 1) * CONV_COLS)
        zs_ref[0, :, cs] = _silu(_dot(hn, win_ref[:, ws])).astype(BF16)

    cq = _dot(hn, win_ref[:, 0:MLA_Q_RANK])
    cq = (cq * _rms_scale(cq) * qln_ref[...]).astype(BF16)
    scale = MLA_QK ** -0.5 * math.log2(math.e)
    for hd in range(MLA_HEADS):
        hs = slice(hd * LANES, (hd + 1) * LANES)
        qn = _dot(cq, wqn_ref[:, hs])
        qr = _dot(cq, wqr_ref[:, hs]) * cosp + _dot(cq, wqrr_ref[:, hs]) * sinp
        q_ref[0, hd, :, 0:MLA_NOPE] = (qn * scale).astype(BF16)
        q_ref[0, hd, :, MLA_NOPE:MLA_QK] = (qr[:, 0:MLA_ROPE] * scale).astype(BF16)

    ckr = _dot(hk, wkd_ref[...])
    lat = ckr[:, 0:MLA_KV_RANK]
    ckv = (lat * _rms_scale(lat) * kvln_ref[...]).astype(BF16)
    kr = ckr[:, LANES:2 * LANES] * cosp + ckr[:, 2 * LANES:3 * LANES] * sinp
    krb = kr[:, 0:MLA_ROPE].astype(BF16)
    for hd in range(MLA_HEADS):
        kn = _dot(ckv, wkvu_ref[:, hd * MLA_NOPE:(hd + 1) * MLA_NOPE])
        vv = _dot(ckv, wkvu_ref[:, MLA_HEADS * MLA_NOPE + hd * MLA_V:MLA_HEADS * MLA_NOPE + (hd + 1) * MLA_V])
        kk_ref[0, hd, :, 0:MLA_NOPE] = kn.astype(BF16)
        kk_ref[0, hd, :, MLA_NOPE:MLA_QK] = krb
        vv_ref[0, hd] = vv.astype(BF16)


def _attn_kernel(q_ref, k_ref, v_ref, zs_ref, o_ref, m_scr, al_scr, alpha_scr, sa_scr, sb_scr, p_scr):
    i = pl.program_id(2)
    tq, sb, tk = ATTN_TQ, ATTN_SUB, ATTN_TK
    m_scr[...] = jnp.full_like(m_scr, -jnp.inf)
    al_scr[...] = jnp.zeros_like(al_scr)

    def rows(r):
        return slice(r * sb, (r + 1) * sb)

    def scores(r, k0, nk):
        return _dot_nt(q_ref[0, 0, rows(r), :], k_ref[0, 0, pl.ds(k0, nk), :])

    def values_and_ones(k0, nk):
        return jnp.concatenate([v_ref[0, 0, pl.ds(k0, nk), :], jnp.ones((nk, LANES), BF16)], axis=1)

    def absorb(r, parts):
        rs = rows(r)
        m_old = m_scr[rs, :]
        m_new = m_old
        for s, _, _ in parts:
            m_new = jnp.maximum(m_new, jnp.max(s, axis=-1, keepdims=True))
        alpha = jnp.exp2(m_old - m_new)
        al = jnp.concatenate([alpha, alpha], axis=1) * al_scr[rs, :]
        for s, k0, nk in parts:
            ps = [jnp.exp2(s[:, c * LANES:(c + 1) * LANES] - m_new) for c in range(nk // LANES)]
            al = al + _dot(jnp.concatenate(ps, axis=1).astype(BF16), values_and_ones(k0, nk))
        al_scr[rs, :] = al
        m_scr[rs, :] = m_new

    def absorb_block(r, s_buf, k0):
        for g in range(sb // ATTN_GROUP):
            gr = slice(g * ATTN_GROUP, (g + 1) * ATTN_GROUP)
            ar = slice(r * sb + g * ATTN_GROUP, r * sb + (g + 1) * ATTN_GROUP)
            s = s_buf[r, gr, :]
            m_old = m_scr[ar, :]
            m_new = jnp.maximum(m_old, jnp.max(s, axis=-1, keepdims=True))
            ps = [jnp.exp2(s[:, c * LANES:(c + 1) * LANES] - m_new) for c in range(tk // LANES)]
            p_scr[r, gr, :] = jnp.concatenate(ps, axis=1).astype(BF16)
            alpha_scr[ar, :] = jnp.exp2(m_old - m_new)
            m_scr[ar, :] = m_new
        rs = rows(r)
        alpha = alpha_scr[rs, :]
        al_scr[rs, :] = (jnp.concatenate([alpha, alpha], axis=1) * al_scr[rs, :]
                         + _dot(p_scr[r], values_and_ones(k0, tk)))

    nsub = tq // sb
    block0 = lambda j: pl.multiple_of(ROW_TILE + j * tk, ROW_TILE)

    ri = lax.broadcasted_iota(jnp.int32, (sb, sb), 0)
    ci = lax.broadcasted_iota(jnp.int32, (sb, sb), 1)
    base = pl.multiple_of(ROW_TILE + i * tq, ROW_TILE)
    meta0 = ROW_TILE - LANES
    is_meta = lax.broadcasted_iota(jnp.int32, (sb, LANES), 1) >= FRONT_PAD - meta0
    s_meta = [jnp.where(is_meta, scores(r, meta0, LANES), -jnp.inf) for r in range(nsub)]
    s_diag = [scores(r, base, (r + 1) * sb) for r in range(nsub)]
    for r in range(nsub):
        sa_scr[r] = scores(r, block0(0), tk)
    for r in range(nsub):
        tri = jnp.where(ci <= ri, s_diag[r][:, r * sb:], -jnp.inf)
        sd = tri if r == 0 else jnp.concatenate([s_diag[r][:, :r * sb], tri], axis=1)
        absorb(r, [(s_meta[r], meta0, LANES), (sd, base, (r + 1) * sb)])

    assert tq % (2 * tk) == 0

    def body(jj, carry):
        j = 2 * jj
        for cur, nxt, step in ((sa_scr, sb_scr, 0), (sb_scr, sa_scr, 1)):
            for r in range(nsub):
                nxt[r] = scores(r, block0(j + step + 1), tk)
                absorb_block(r, cur, block0(j + step))
        return carry

    lax.fori_loop(0, (i * tq) // (2 * tk), body, 0)

    o_ref[0] = (al_scr[:, 0:MLA_V] / al_scr[:, MLA_V:] * zs_ref[0].astype(F32)).astype(BF16)


def _params(*sem):
    return pltpu.CompilerParams(dimension_semantics=sem, vmem_limit_bytes=VMEM_LIMIT)


def kernel(x, meta_tokens, pre_norm, post_norm, gdn_w_in, gdn_conv_w, gdn_a_log, gdn_dt_bias, gdn_out_norm, gdn_w_out, kv_norm, kv_w_down, kv_latent_norm, kv_w_up, mla_w_in, mla_q_latent_norm, mla_w_q_up, mla_w_out):
    B, S, D = x.shape
    assert S % ATTN_TQ == 0 and S % (OUT_TILES * ROW_TILE) == 0 and (S // ROW_TILE + 1) % WIDE_TILES == 0
    assert gdn_w_in.shape[0] == 1 and mla_w_in.shape[0] == 1
    T = ROW_TILE
    nt = S // T + 1
    Lp = nt * T
    nch = Lp // GDN_CHUNK
    nh = GDN_V_HEADS

    head = jnp.concatenate([jnp.zeros((FRONT_PAD, D), F32), meta_tokens.astype(F32)], axis=0)
    row = lambda a: a.reshape(1, -1).astype(F32)

    w_in = gdn_w_in[0]
    w_ba_f = w_in[:, GDN_CONV_W + GDN_V_W:]
    w_b, w_a = w_ba_f[:, :nh], w_ba_f[:, nh:]
    w_ba = jnp.concatenate([w_b, w_a, w_a, w_a, jnp.zeros((D, LANES - 4 * nh), F32)], axis=-1).astype(BF16)
    eo = lambda a: jnp.concatenate([a[0::2], a[1::2]], axis=0)
    w_bat = jnp.concatenate([eo(w_b.T), eo(w_a.T)], axis=0).astype(BF16)
    a_log = gdn_a_log[0].astype(F32)
    dt_b = gdn_dt_bias[0].astype(F32)
    lane_groups = lambda a: jnp.concatenate([jnp.zeros((nh,), F32), a, a, a,
                                             jnp.zeros((LANES - 4 * nh,), F32)]).reshape(1, LANES)

    tile_spec = lambda w: pl.BlockSpec((1, T, w), lambda b, i: (b, i, 0))
    wide_spec = lambda w: pl.BlockSpec((1, WIDE_TILES * T, w), lambda b, i: (b, i, 0))

    x_specs = [pl.BlockSpec((1, T, D), lambda b, i, j=j: (b, jnp.maximum(WIDE_TILES * i + j - 1, 0), 0))
               for j in range(WIDE_TILES)]
    gate_spec = lambda r: pl.BlockSpec((1, WIDE_TILES * T // GDN_CHUNK, r, LANES), lambda b, i: (b, i, 0, 0))
    q, k, v, zs, gcol, grow, egl = pl.pallas_call(
        _gdn_in_kernel,
        grid=(B, nt // WIDE_TILES),
        in_specs=[_const_spec((T, D)), _const_spec((1, D)),
                  _const_spec(w_in.shape), _const_spec((D, LANES)), _const_spec((2 * nh, D)),
                  _const_spec((GDN_CONV_TAPS, GDN_CONV_W)),
                  _const_spec((1, LANES)), _const_spec((1, LANES)), _const_spec((nh, 1)), _const_spec((nh, 1))]
        + x_specs,
        out_specs=[wide_spec(GDN_QK_W), wide_spec(GDN_QK_W), wide_spec(GDN_V_W), wide_spec(GDN_V_W),
                   wide_spec(LANES), gate_spec(2 * nh), gate_spec(nh)],
        out_shape=[jax.ShapeDtypeStruct((B, Lp, GDN_QK_W), BF16),
                   jax.ShapeDtypeStruct((B, Lp, GDN_QK_W), BF16),
                   jax.ShapeDtypeStruct((B, Lp, GDN_V_W), BF16),
                   jax.ShapeDtypeStruct((B, Lp, GDN_V_W), BF16),
                   jax.ShapeDtypeStruct((B, Lp, LANES), F32),
                   jax.ShapeDtypeStruct((B, nch, 2 * nh, LANES), F32),
                   jax.ShapeDtypeStruct((B, nch, nh, LANES), F32)],
        scratch_shapes=[pltpu.VMEM((HALO_ROWS, GDN_CONV_W), F32),
                        pltpu.VMEM((CONV_COLS // GDN_HEAD, HALO_ROWS + T, GDN_HEAD), F32)],
        compiler_params=_params("arbitrary", "arbitrary"),
        name="gdn_in",
    )(head, row(pre_norm[0]), w_in.astype(BF16), w_ba, w_bat, gdn_conv_w[0].astype(F32),
      lane_groups(a_log), lane_groups(dt_b), eo(a_log).reshape(nh, 1), eo(dt_b).reshape(nh, 1),
      *([x] * WIDE_TILES))

    npair = nh // 2
    tb, wk, aqk, kdt = pl.pallas_call(
        _gdn_prep_kernel,
        grid=(B, nt // WIDE_TILES),
        in_specs=[wide_spec(GDN_QK_W), wide_spec(GDN_QK_W), wide_spec(LANES),
                  pl.BlockSpec((1, WIDE_TILES * T // GDN_CHUNK, 2 * nh, LANES), lambda b, i: (b, i, 0, 0))],
        out_specs=[wide_spec(GDN_QK_W), wide_spec(GDN_V_W), wide_spec(GDN_QK_W),
                   pl.BlockSpec((1, WIDE_TILES * T // GDN_CHUNK, npair, GDN_HEAD, LANES),
                                lambda b, i: (b, i, 0, 0, 0))],
        out_shape=[jax.ShapeDtypeStruct((B, Lp, GDN_QK_W), BF16),
                   jax.ShapeDtypeStruct((B, Lp, GDN_V_W), BF16),
                   jax.ShapeDtypeStruct((B, Lp, GDN_QK_W), BF16),
                   jax.ShapeDtypeStruct((B, nch, npair, GDN_HEAD, LANES), BF16)],
        compiler_params=_params("parallel", "parallel"),
        name="gdn_prep",
    )(q, k, gcol, grow)

    cpt = T // GDN_CHUNK
    chunk_spec = lambda w: pl.BlockSpec((B, T, w), lambda n: (0, n, 0))
    o_gdn = pl.pallas_call(
        _gdn_scan_kernel,
        grid=(nt,),
        in_specs=[chunk_spec(GDN_QK_W), chunk_spec(GDN_V_W), chunk_spec(GDN_QK_W),
                  pl.BlockSpec((B, cpt, npair, GDN_HEAD, LANES), lambda n: (0, n, 0, 0, 0)),
                  chunk_spec(GDN_QK_W), chunk_spec(GDN_V_W), chunk_spec(GDN_V_W), chunk_spec(LANES),
                  pl.BlockSpec((B, cpt, nh, LANES), lambda n: (0, n, 0, 0)),
                  _const_spec((1, GDN_HEAD))],
        out_specs=chunk_spec(GDN_V_W),
        out_shape=jax.ShapeDtypeStruct((B, Lp, GDN_V_W), BF16),
        scratch_shapes=[pltpu.VMEM((B, nh, GDN_HEAD, GDN_HEAD), F32)],
        compiler_params=_params("arbitrary"),
        name="gdn_scan",
    )(tb, wk, aqk, kdt, q, v, zs, gcol, egl, row(gdn_out_norm[0]))

    h1 = pl.pallas_call(
        _out_proj_first_kernel,
        grid=(B, nt // WIDE_TILES),
        in_specs=[wide_spec(GDN_V_W), _const_spec((GDN_V_W, D)), _const_spec((1, D)), _const_spec((T, D))]
        + x_specs,
        out_specs=wide_spec(D),
        out_shape=jax.ShapeDtypeStruct((B, Lp, D), F32),
        compiler_params=_params("parallel", "arbitrary"),
        name="gdn_out",
    )(o_gdn, gdn_w_out[0].astype(BF16), row(post_norm[0]), head, *([x] * WIDE_TILES))

    w_in1 = mla_w_in[0]
    zw = w_in1.shape[1] - MLA_Q_RANK
    half = MLA_ROPE // 2
    rot = lambda w: jnp.concatenate([-w[..., half:], w[..., :half]], axis=-1)
    lane_pad = lambda w: jnp.pad(w, [(0, 0)] * (w.ndim - 1) + [(0, LANES - w.shape[-1])])
    wq = mla_w_q_up[0].reshape(MLA_Q_RANK, MLA_HEADS, MLA_QK)
    w_qn = wq[..., :MLA_NOPE].reshape(MLA_Q_RANK, MLA_HEADS * MLA_NOPE).astype(BF16)
    w_qr = lane_pad(wq[..., MLA_NOPE:]).reshape(MLA_Q_RANK, MLA_HEADS * LANES).astype(BF16)
    w_qrr = lane_pad(rot(wq[..., MLA_NOPE:])).reshape(MLA_Q_RANK, MLA_HEADS * LANES).astype(BF16)
    wkd_r = kv_w_down[:, MLA_KV_RANK:]
    w_kd = jnp.concatenate([kv_w_down[:, :MLA_KV_RANK], lane_pad(wkd_r), lane_pad(rot(wkd_r))],
                           axis=-1).astype(BF16)
    wku = kv_w_up.reshape(MLA_KV_RANK, MLA_HEADS, MLA_NOPE + MLA_V)
    w_kvu = jnp.concatenate([wku[..., :MLA_NOPE].reshape(MLA_KV_RANK, -1),
                             wku[..., MLA_NOPE:].reshape(MLA_KV_RANK, -1)], axis=-1).astype(BF16)

    inv = ROPE_THETA ** (-jnp.arange(0, MLA_ROPE, 2, dtype=F32) / MLA_ROPE)
    pos = (jnp.arange(Lp, dtype=jnp.int32) - FRONT_PAD).astype(F32)
    ang = pos[:, None] * inv[None, :]
    zpad = jnp.zeros((Lp, LANES - MLA_ROPE), F32)
    cosp = jnp.concatenate([jnp.cos(ang), jnp.cos(ang), zpad], axis=-1)
    sinp = jnp.concatenate([jnp.sin(ang), jnp.sin(ang), zpad], axis=-1)

    head_tile = lambda w: pl.BlockSpec((1, MLA_HEADS, T, w), lambda b, i: (b, 0, i, 0))
    q1, k1, v1, zs1 = pl.pallas_call(
        _mla_in_kernel,
        grid=(B, nt),
        in_specs=[tile_spec(D), _const_spec((1, D)), _const_spec((1, D)),
                  _const_spec(w_in1.shape), _const_spec((1, MLA_Q_RANK)),
                  _const_spec((MLA_Q_RANK, MLA_HEADS * MLA_NOPE)),
                  _const_spec((MLA_Q_RANK, MLA_HEADS * LANES)),
                  _const_spec((MLA_Q_RANK, MLA_HEADS * LANES)),
                  _const_spec((D, 3 * LANES)), _const_spec((1, MLA_KV_RANK)),
                  _const_spec((MLA_KV_RANK, MLA_HEADS * (MLA_NOPE + MLA_V))),
                  pl.BlockSpec((T, LANES), lambda b, i: (i, 0)),
                  pl.BlockSpec((T, LANES), lambda b, i: (i, 0))],
        out_specs=[pl.BlockSpec((1, MLA_HEADS, T, MLA_QK), lambda b, i: (b, 0, jnp.maximum(i - 1, 0), 0)),
                   head_tile(MLA_QK), head_tile(MLA_V),
                   pl.BlockSpec((1, T, zw), lambda b, i: (b, jnp.maximum(i - 1, 0), 0))],
        out_shape=[jax.ShapeDtypeStruct((B, MLA_HEADS, S, MLA_QK), BF16),
                   jax.ShapeDtypeStruct((B, MLA_HEADS, Lp, MLA_QK), BF16),
                   jax.ShapeDtypeStruct((B, MLA_HEADS, Lp, MLA_V), BF16),
                   jax.ShapeDtypeStruct((B, S, zw), BF16)],
        compiler_params=_params("arbitrary", "arbitrary"),
        name="mla_in",
    )(h1, row(pre_norm[1]), row(kv_norm), w_in1.astype(BF16), row(mla_q_latent_norm[0]), w_qn, w_qr, w_qrr,
      w_kd, row(kv_latent_norm), w_kvu, cosp, sinp)

    o_attn = pl.pallas_call(
        _attn_kernel,
        grid=(B, MLA_HEADS, S // ATTN_TQ),
        in_specs=[pl.BlockSpec((1, 1, ATTN_TQ, MLA_QK), lambda b, h, i: (b, h, i, 0)),
                  pl.BlockSpec((1, 1, Lp, MLA_QK), lambda b, h, i: (b, h, 0, 0)),
                  pl.BlockSpec((1, 1, Lp, MLA_V), lambda b, h, i: (b, h, 0, 0)),
                  pl.BlockSpec((1, ATTN_TQ, MLA_V), lambda b, h, i: (b, i, h))],
        out_specs=pl.BlockSpec((1, ATTN_TQ, MLA_V), lambda b, h, i: (b, i, h)),
        out_shape=jax.ShapeDtypeStruct((B, S, MLA_HEADS * MLA_V), BF16),
        scratch_shapes=[pltpu.VMEM((ATTN_TQ, LANES), F32), pltpu.VMEM((ATTN_TQ, MLA_V + LANES), F32),
                        pltpu.VMEM((ATTN_TQ, LANES), F32),
                        pltpu.VMEM((ATTN_TQ // ATTN_SUB, ATTN_SUB, ATTN_TK), F32),
                        pltpu.VMEM((ATTN_TQ // ATTN_SUB, ATTN_SUB, ATTN_TK), F32),
                        pltpu.VMEM((ATTN_TQ // ATTN_SUB, ATTN_SUB, ATTN_TK), BF16)],
        compiler_params=_params("parallel", "parallel", "arbitrary"),
        name="mla_attn",
    )(q1, k1, v1, zs1)

    out = pl.pallas_call(
        _out_proj_last_kernel,
        grid=(B, S // (OUT_TILES * T)),
        in_specs=[pl.BlockSpec((1, OUT_TILES * T, MLA_HEADS * MLA_V), lambda b, i: (b, i, 0)),
                  _const_spec((MLA_HEADS * MLA_V, D)), _const_spec((1, D))]
        + [pl.BlockSpec((1, T, D), lambda b, i, j=j: (b, OUT_TILES * i + j + 1, 0)) for j in range(OUT_TILES)],
        out_specs=pl.BlockSpec((1, OUT_TILES * T, D), lambda b, i: (b, i, 0)),
        out_shape=jax.ShapeDtypeStruct((B, S, D), x.dtype),
        compiler_params=_params("parallel", "arbitrary"),
        name="mla_out",
    )(o_attn, mla_w_out[0].astype(BF16), row(post_norm[1]), *([h1] * OUT_TILES))
    return out
```

```python
import functools
import math

import jax
import jax.numpy as jnp
from jax import lax
from jax.experimental import pallas as pl
from jax.experimental.pallas import tpu as pltpu

NORM_EPS = 1e-6
N_META_ROWS = 16

GDN_QK_HEADS = 8
GDN_V_HEADS = 16
GDN_HEAD = 128
GDN_CONV_TAPS = 4
GDN_CHUNK = 64
GDN_QK_W = GDN_QK_HEADS * GDN_HEAD
GDN_V_W = GDN_V_HEADS * GDN_HEAD
GDN_CONV_W = 2 * GDN_QK_W + GDN_V_W

MLA_HEADS = 16
MLA_NOPE = 128
MLA_ROPE = 64
MLA_V = 128
MLA_Q_RANK = 256
MLA_KV_RANK = 128
MLA_QK = MLA_NOPE + MLA_ROPE
ROPE_THETA = 10000.0

LANES = 128
ROW_TILE = 256
FRONT_PAD = ROW_TILE - N_META_ROWS
WIDE_TILES = 3
OUT_TILES = 4
CONV_COLS = 512
HALO_ROWS = 8
ATTN_TQ = 2048
ATTN_SUB = 256
ATTN_TK = 1024
ATTN_GROUP = 16
VMEM_LIMIT = 60 * 1024 * 1024

F32 = jnp.float32
BF16 = jnp.bfloat16


def _dot(a, b):
    return jnp.dot(a, b, preferred_element_type=F32)


def _dot_nt(a, b):
    return lax.dot_general(a, b, (((1,), (1,)), ((), ())), preferred_element_type=F32)


def _dot_tn(a, b):
    return lax.dot_general(a, b, (((0,), (0,)), ((), ())), preferred_element_type=F32)


def _dot_exact(a, b):
    return jnp.dot(a, b, preferred_element_type=F32, precision=lax.Precision.HIGHEST)


def _silu(x):
    return x * jax.nn.sigmoid(x)


def _softplus(x):
    return jnp.maximum(x, 0.0) + jnp.log1p(jnp.exp(-jnp.abs(x)))


def _rms_scale(x):
    return lax.rsqrt(jnp.mean(x * x, axis=-1, keepdims=True) + NORM_EPS)


def _const_spec(shape):
    nd = len(shape)
    return pl.BlockSpec(shape, lambda *_: (0,) * nd, pipeline_mode=pl.Buffered(1))


def _gdn_in_kernel(head_ref, gain_ref, win_ref, wba_ref, wbat_ref, convw_ref,
                   arow_ref, dtrow_ref, acol_ref, dtcol_ref, *refs):
    x_refs = refs[:WIDE_TILES]
    outs = refs[WIDE_TILES:WIDE_TILES + 7]
    halo_scr, buf_scr = refs[WIDE_TILES + 7:]
    i = pl.program_id(1)

    @pl.when(i == 0)
    def _():
        halo_scr[...] = jnp.zeros_like(halo_scr)

    ri = lax.broadcasted_iota(jnp.int32, (ROW_TILE, ROW_TILE), 0)
    ci = lax.broadcasted_iota(jnp.int32, (ROW_TILE, ROW_TILE), 1)
    same = (ri // GDN_CHUNK) == (ci // GDN_CHUNK)
    prefix = (jnp.where(same & (ri >= ci), 1.0, 0.0).astype(F32),
              jnp.where(same & (ri <= ci), 1.0, 0.0).astype(F32),
              jnp.where(same, 1.0, 0.0).astype(F32))
    for t, x_ref in enumerate(x_refs):
        x = x_ref[0] if t else jnp.where(i == 0, head_ref[...], x_ref[0])
        _gdn_in_tile(t, x, gain_ref, win_ref, wba_ref, wbat_ref, convw_ref,
                     arow_ref, dtrow_ref, acol_ref, dtcol_ref, prefix, outs, halo_scr, buf_scr)


def _gdn_in_tile(t, x, gain_ref, win_ref, wba_ref, wbat_ref, convw_ref,
                 arow_ref, dtrow_ref, acol_ref, dtcol_ref, prefix, outs, halo_scr, buf_scr):
    q_ref, k_ref, v_ref, zs_ref, gcol_ref, grow_ref, egl_ref = outs
    lower, upper, block = prefix
    rt = slice(t * ROW_TILE, (t + 1) * ROW_TILE)
    c0 = t * (ROW_TILE // GDN_CHUNK)
    hn = (x * _rms_scale(x) * gain_ref[...]).astype(BF16)

    lo = HALO_ROWS - (GDN_CONV_TAPS - 1)
    for c in range(GDN_CONV_W // CONV_COLS):
        cs = slice(c * CONV_COLS, (c + 1) * CONV_COLS)
        p = _dot(hn, win_ref[:, cs])
        for hh in range(CONV_COLS // GDN_HEAD):
            col = c * CONV_COLS + hh * GDN_HEAD
            buf_scr[hh, 0:HALO_ROWS, :] = halo_scr[:, col:col + GDN_HEAD]
            buf_scr[hh, HALO_ROWS:HALO_ROWS + ROW_TILE, :] = p[:, hh * GDN_HEAD:(hh + 1) * GDN_HEAD]
            halo_scr[:, col:col + GDN_HEAD] = p[ROW_TILE - HALO_ROWS:, hh * GDN_HEAD:(hh + 1) * GDN_HEAD]
            yh = convw_ref[0:1, col:col + GDN_HEAD] * buf_scr[hh, lo:lo + ROW_TILE, :]
            for j in range(1, GDN_CONV_TAPS):
                yh = yh + convw_ref[j:j + 1, col:col + GDN_HEAD] * buf_scr[hh, lo + j:lo + j + ROW_TILE, :]
            yh = _silu(yh)
            if col < 2 * GDN_QK_W:
                r = lax.rsqrt(jnp.sum(yh * yh, axis=-1, keepdims=True) + NORM_EPS)
                if col < GDN_QK_W:
                    q_ref[0, rt, col:col + GDN_HEAD] = (yh * (r * (GDN_HEAD ** -0.5))).astype(BF16)
                else:
                    k_ref[0, rt, col - GDN_QK_W:col - GDN_QK_W + GDN_HEAD] = (yh * r).astype(BF16)
            else:
                v_ref[0, rt, col - 2 * GDN_QK_W:col - 2 * GDN_QK_W + GDN_HEAD] = yh.astype(BF16)

    for c in range(GDN_V_W // CONV_COLS):
        cs = slice(c * CONV_COLS, (c + 1) * CONV_COLS)
        ws = slice(GDN_CONV_W + c * CONV_COLS, GDN_CONV_W + (c + 1) * CONV_COLS)
        zs_ref[0, rt, cs] = _silu(_dot(hn, win_ref[:, ws])).astype(BF16)

    nh = GDN_V_HEADS
    ba = _dot(hn, wba_ref[...])
    bat = _dot_nt(wbat_ref[...], hn)
    beta_c = 1.0 / (1.0 + jnp.exp(-ba))
    g_c = -jnp.exp(arow_ref[...]) * _softplus(ba + dtrow_ref[...])
    beta_r = 1.0 / (1.0 + jnp.exp(-bat[0:nh, :]))
    g_r = -jnp.exp(acol_ref[...]) * _softplus(bat[nh:2 * nh, :] + dtcol_ref[...])

    gc_c = _dot_exact(lower, g_c)
    gl_c = _dot_exact(block, g_c)
    gc_r = _dot_exact(g_r, upper)
    grp = lax.broadcasted_iota(jnp.int32, (ROW_TILE, LANES), 1) // nh
    gcol_ref[0, rt, :] = jnp.where(grp == 0, beta_c,
                                   jnp.where(grp == 1, gc_c,
                                             jnp.where(grp == 2, jnp.exp(gc_c),
                                                       jnp.where(grp == 3, jnp.exp(gl_c - gc_c), 0.0))))
    npair = nh // 2
    for kind, xr in enumerate((gc_r, beta_r, beta_r * jnp.exp(gc_r))):
        for c in range(ROW_TILE // GDN_CHUNK):
            cs = slice(c * GDN_CHUNK, (c + 1) * GDN_CHUNK)
            grow_ref[0, c0 + c, kind * npair:(kind + 1) * npair, :] = jnp.concatenate(
                [xr[0:npair, cs], xr[npair:nh, cs]], axis=-1)
    for c in range(ROW_TILE // GDN_CHUNK):
        cs = slice(c * GDN_CHUNK, (c + 1) * GDN_CHUNK)
        last = (c + 1) * GDN_CHUNK - 1
        gl = jnp.broadcast_to(gc_r[:, last:last + 1], (nh, LANES))
        egl_ref[0, c0 + c] = jnp.exp(gl)
        to_end = jnp.exp(gl[:, 0:GDN_CHUNK] - gc_r[:, cs])
        grow_ref[0, c0 + c, 3 * npair:4 * npair, :] =jnp.concatenate([to_end[0:npair], to_end[npair:nh]], axis=-1)


def _gdn_prep_kernel(q_ref, k_ref, gcol_ref, grow_ref, tb_ref, w_ref, a_ref, kdt_ref):
    C = GDN_CHUNK
    nh = GDN_V_HEADS
    npair = nh // 2
    row = lax.broadcasted_iota(jnp.int32, (C, LANES), 0)
    lane = lax.broadcasted_iota(jnp.int32, (C, LANES), 1)
    left = lane < C
    col = jnp.where(left, lane, lane - C)
    incl = row >= col
    strict = row > col
    eye = jnp.where(row == col, 1.0, 0.0).astype(F32)
    diag8 = strict & ((row // 8) == (col // 8))
    merges = tuple(strict & ((row // (2 * s)) == (col // (2 * s))) & ((row // s) != (col // s))
                   for s in (8, 16, 32))
    zero_b = jnp.zeros((C, LANES), BF16)

    def blockdiag(x):
        xb = x.astype(BF16)
        return jnp.concatenate([jnp.where(left, xb, zero_b), jnp.where(left, zero_b, xb)], axis=0)

    def pair_cols(g, base, p_):
        return jnp.take_along_axis(g, jnp.where(left, base + 2 * p_, base + 2 * p_ + 1), axis=1)

    cpt = ROW_TILE // C
    streams = [(c, p_) for c in range(cpt) for p_ in range(npair)]
    zero_k = jnp.zeros((C, GDN_HEAD), BF16)

    def row_tile(j, carry):
        rows = lambda c: pl.ds(pl.multiple_of(j * ROW_TILE + c * C, C), C)
        ms, kps, bege, betar = [], [], [], []
        for c, p_ in streams:
            rs, cj = rows(c), j * cpt + c
            ps = slice(p_ * GDN_HEAD, (p_ + 1) * GDN_HEAD)
            qp = q_ref[0, rs, ps]
            kp = k_ref[0, rs, ps]
            g = gcol_ref[0, rs, :]
            both = _dot_nt(jnp.concatenate([qp, kp], axis=0), jnp.concatenate([kp, kp], axis=0))
            qk2, kk2 = both[0:C], both[C:2 * C]
            decay = jnp.exp(jnp.where(incl, pair_cols(g, nh, p_) - grow_ref[0, cj, p_:p_ + 1, :], -jnp.inf))
            ms.append(jnp.where(strict, pair_cols(g, 0, p_) * kk2 * decay, 0.0))
            a_ref[0, rs, ps] = (qk2 * decay).astype(BF16)
            kpf = kp.astype(F32)
            kdt_ref[0, cj, p_] = (jnp.concatenate([kpf, kpf], axis=0).T
                                  * grow_ref[0, cj, 3 * npair + p_:3 * npair + p_ + 1, :]).astype(BF16)
            betar.append(grow_ref[0, cj, npair + p_:npair + p_ + 1, :])
            kps.append(kp)
            bege.append(grow_ref[0, cj, 2 * npair + p_:2 * npair + p_ + 1, :])

        m8 = [jnp.where(diag8, m, 0.0) for m in ms]
        q2 = [_dot(x.astype(BF16), blockdiag(x)) for x in m8]
        pinv = [eye - x for x in m8]
        pinv = [p + _dot(p.astype(BF16), blockdiag(y)) for p, y in zip(pinv, q2)]
        q4 = [_dot(y.astype(BF16), blockdiag(y)) for y in q2]
        pinv = [p + _dot(p.astype(BF16), blockdiag(y)) for p, y in zip(pinv, q4)]
        for mask in merges:
            cp = [_dot(jnp.where(mask, m, 0.0).astype(BF16), blockdiag(p)) for m, p in zip(ms, pinv)]
            pinv = [p - _dot(p.astype(BF16), blockdiag(y)) for p, y in zip(pinv, cp)]

        for (c, p_), p, kp, bg, br in zip(streams, pinv, kps, bege, betar):
            rs = rows(c)
            ps = slice(p_ * GDN_HEAD, (p_ + 1) * GDN_HEAD)
            kbd = jnp.concatenate([jnp.concatenate([kp, zero_k], axis=1),
                                   jnp.concatenate([zero_k, kp], axis=1)], axis=0)
            w_ref[0, rs, 2 * p_ * GDN_HEAD:(2 * p_ + 2) * GDN_HEAD] = _dot((p * bg).astype(BF16), kbd).astype(BF16)
            tb_ref[0, rs, ps] = (p * br).astype(BF16)
        return carry

    lax.fori_loop(0, q_ref.shape[1] // ROW_TILE, row_tile, 0)


def _gdn_scan_kernel(tb_ref, w_ref, a_ref, kdt_ref, q_ref, v_ref, zs_ref, gcol_ref, egl_ref, onorm_ref, o_ref,
                     state_scr):
    @pl.when(pl.program_id(0) == 0)
    def _():
        state_scr[...] = jnp.zeros_like(state_scr)

    C = GDN_CHUNK
    nh = GDN_V_HEADS
    nb = q_ref.shape[0]
    onorm = onorm_ref[...]
    zero_v = jnp.zeros((C, GDN_HEAD), BF16)
    ones_sq = jnp.ones((GDN_HEAD, GDN_HEAD), BF16)

    def blockdiag(x0, x1):
        return jnp.concatenate([jnp.concatenate([x0, zero_v], axis=1),
                                jnp.concatenate([zero_v, x1], axis=1)], axis=0)

    pairs = [(b, p_) for b in range(nb) for p_ in range(nh // 2)]
    heads = [(b, h) for b, p_ in pairs for h in (2 * p_, 2 * p_ + 1)]
    hsl = lambda h: slice(h * GDN_HEAD, (h + 1) * GDN_HEAD)

    def chunk(c, carry):
        rs = pl.ds(pl.multiple_of(c * C, C), C)
        u2 = [_dot(tb_ref[b, rs, hsl(p_)], blockdiag(v_ref[b, rs, hsl(2 * p_)], v_ref[b, rs, hsl(2 * p_ + 1)]))
              for b, p_ in pairs]
        u = {(b, 2 * p_ + j): x[:, j * GDN_HEAD:(j + 1) * GDN_HEAD]
             for (b, p_), x in zip(pairs, u2) for j in (0, 1)}
        s_old = {bh: state_scr[bh[0], bh[1]] for bh in heads}
        wq = {(b, h): _dot(jnp.concatenate([w_ref[b, rs, hsl(h)], q_ref[b, rs, hsl(h // 2)]], axis=0),
                           s_old[(b, h)].astype(BF16)) for b, h in heads}
        vnb = {bh: (u[bh] - wq[bh][0:C]).astype(BF16) for bh in heads}
        od = [_dot(jnp.concatenate([a_ref[b, rs, hsl(p_)], kdt_ref[b, c, p_]], axis=0),
                   blockdiag(vnb[(b, 2 * p_)], vnb[(b, 2 * p_ + 1)])) for b, p_ in pairs]
        os = {}
        for (b, p_), x in zip(pairs, od):
            for j in (0, 1):
                h = 2 * p_ + j
                js = slice(j * GDN_HEAD, (j + 1) * GDN_HEAD)
                erow = (h % 2) * (nh // 2) + h // 2
                state_scr[b, h] = s_old[(b, h)] * egl_ref[b, c, erow:erow + 1, :] + x[C:, js]
                os[(b, h)] = gcol_ref[b, rs, 2 * nh + h:2 * nh + h + 1] * wq[(b, h)][C:2 * C] + x[0:C, js]
        sq = {bh: _dot((o * o).astype(BF16), ones_sq) for bh, o in os.items()}
        for (b, h), o in os.items():
            on = o * lax.rsqrt(sq[(b, h)] * (1.0 / GDN_HEAD) + NORM_EPS) * onorm
            o_ref[b, rs, hsl(h)] = (on * zs_ref[b, rs, hsl(h)].astype(F32)).astype(BF16)
        return carry

    lax.fori_loop(0, q_ref.shape[1] // C, chunk, 0)


def _out_proj_first_kernel(o_ref, w_ref, gain_ref, head_ref, *refs):
    x_refs, h_ref = refs[:-1], refs[-1]
    y = _dot(o_ref[0], w_ref[...])
    yn = y * _rms_scale(y) * gain_ref[...]
    for j, x_ref in enumerate(x_refs):
        res = x_ref[0] if j else jnp.where(pl.program_id(1) == 0, head_ref[...], x_ref[0])
        h_ref[0, j * ROW_TILE:(j + 1) * ROW_TILE, :] = res + yn[j * ROW_TILE:(j + 1) * ROW_TILE]


def _out_proj_last_kernel(o_ref, w_ref, gain_ref, *refs):
    h_refs, out_ref = refs[:-1], refs[-1]
    y = _dot(o_ref[0], w_ref[...])
    yn = y * _rms_scale(y) * gain_ref[...]
    for j, h_ref in enumerate(h_refs):
        out_ref[0, j * ROW_TILE:(j + 1) * ROW_TILE, :] = h_ref[0] + yn[j * ROW_TILE:(j + 1) * ROW_TILE]


def _mla_in_kernel(h_ref, pre_ref, kvn_ref, win_ref, qln_ref, wqn_ref, wqr_ref, wqrr_ref,
                   wkd_ref, kvln_ref, wkvu_ref, cos_ref, sin_ref,
                   q_ref, kk_ref, vv_ref, zs_ref):
    h = h_ref[0]
    hr = h * _rms_scale(h)
    hn = (hr * pre_ref[...]).astype(BF16)
    hk = (hr * kvn_ref[...]).astype(BF16)
    cosp = cos_ref[...]
    sinp = sin_ref[...]

    for c in range((win_ref.shape[1] - MLA_Q_RANK) // CONV_COLS):
        cs = slice(c * CONV_COLS, (c + 1) * CONV_COLS)
        ws = slice(MLA_Q_RANK + c * CONV_COLS, MLA_Q_RANK + (c + 1) * CONV_COLS)
        zs_ref[0, :, cs] = _silu(_dot(hn, win_ref[:, ws])).astype(BF16)

    cq = _dot(hn, win_ref[:, 0:MLA_Q_RANK])
    cq = (cq * _rms_scale(cq) * qln_ref[...]).astype(BF16)
    scale = MLA_QK ** -0.5 * math.log2(math.e)
    for hd in range(MLA_HEADS):
        hs = slice(hd * LANES, (hd + 1) * LANES)
        qn = _dot(cq, wqn_ref[:, hs])
        qr = _dot(cq, wqr_ref[:, hs]) * cosp + _dot(cq, wqrr_ref[:, hs]) * sinp
        q_ref[0, hd, :, 0:MLA_NOPE] = (qn * scale).astype(BF16)
        q_ref[0, hd, :, MLA_NOPE:MLA_QK] = (qr[:, 0:MLA_ROPE] * scale).astype(BF16)

    ckr = _dot(hk, wkd_ref[...])
    lat = ckr[:, 0:MLA_KV_RANK]
    ckv = (lat * _rms_scale(lat) * kvln_ref[...]).astype(BF16)
    kr = ckr[:, LANES:2 * LANES] * cosp + ckr[:, 2 * LANES:3 * LANES] * sinp
    krb = kr[:, 0:MLA_ROPE].astype(BF16)
    for hd in range(MLA_HEADS):
        kn = _dot(ckv, wkvu_ref[:, hd * MLA_NOPE:(hd + 1) * MLA_NOPE])
        vv = _dot(ckv, wkvu_ref[:, MLA_HEADS * MLA_NOPE + hd * MLA_V:MLA_HEADS * MLA_NOPE + (hd + 1) * MLA_V])
        kk_ref[0, hd, :, 0:MLA_NOPE] = kn.astype(BF16)
        kk_ref[0, hd, :, MLA_NOPE:MLA_QK] = krb
        vv_ref[0, hd] = vv.astype(BF16)


def _attn_kernel(q_ref, k_ref, v_ref, zs_ref, o_ref, m_scr, al_scr, alpha_scr, sa_scr, sb_scr, p_scr):
    i = pl.program_id(2)
    tq, sb, tk = ATTN_TQ, ATTN_SUB, ATTN_TK
    m_scr[...] = jnp.full_like(m_scr, -jnp.inf)
    al_scr[...] = jnp.zeros_like(al_scr)

    def rows(r):
        return slice(r * sb, (r + 1) * sb)

    def scores(r, k0, nk):
        return _dot_nt(q_ref[0, 0, rows(r), :], k_ref[0, 0, pl.ds(k0, nk), :])

    def values_and_ones(k0, nk):
        return jnp.concatenate([v_ref[0, 0, pl.ds(k0, nk), :], jnp.ones((nk, LANES), BF16)], axis=1)

    def absorb(r, parts):
        rs = rows(r)
        m_old = m_scr[rs, :]
        m_new = m_old
        for s, _, _ in parts:
            m_new = jnp.maximum(m_new, jnp.max(s, axis=-1, keepdims=True))
        alpha = jnp.exp2(m_old - m_new)
        al = jnp.concatenate([alpha, alpha], axis=1) * al_scr[rs, :]
        for s, k0, nk in parts:
            ps = [jnp.exp2(s[:, c * LANES:(c + 1) * LANES] - m_new) for c in range(nk // LANES)]
            al = al + _dot(jnp.concatenate(ps, axis=1).astype(BF16), values_and_ones(k0, nk))
        al_scr[rs, :] = al
        m_scr[rs, :] = m_new

    def absorb_block(r, s_buf, k0):
        for g in range(sb // ATTN_GROUP):
            gr = slice(g * ATTN_GROUP, (g + 1) * ATTN_GROUP)
            ar = slice(r * sb + g * ATTN_GROUP, r * sb + (g + 1) * ATTN_GROUP)
            s = s_buf[r, gr, :]
            m_old = m_scr[ar, :]
            m_new = jnp.maximum(m_old, jnp.max(s, axis=-1, keepdims=True))
            ps = [jnp.exp2(s[:, c * LANES:(c + 1) * LANES] - m_new) for c in range(tk // LANES)]
            p_scr[r, gr, :] = jnp.concatenate(ps, axis=1).astype(BF16)
            alpha_scr[ar, :] = jnp.exp2(m_old - m_new)
            m_scr[ar, :] = m_new
        rs = rows(r)
        alpha = alpha_scr[rs, :]
        al_scr[rs, :] = (jnp.concatenate([alpha, alpha], axis=1) * al_scr[rs, :]
                         + _dot(p_scr[r], values_and_ones(k0, tk)))

    nsub = tq // sb
    block0 = lambda j: pl.multiple_of(ROW_TILE + j * tk, ROW_TILE)

    ri = lax.broadcasted_iota(jnp.int32, (sb, sb), 0)
    ci = lax.broadcasted_iota(jnp.int32, (sb, sb), 1)
    base = pl.multiple_of(ROW_TILE + i * tq, ROW_TILE)
    meta0 = ROW_TILE - LANES
    is_meta = lax.broadcasted_iota(jnp.int32, (sb, LANES), 1) >= FRONT_PAD - meta0

    def mask_diag(s):
        tri = jnp.where(ci <= ri, s[:, -sb:], -jnp.inf)
        return tri if s.shape[1] == sb else jnp.concatenate([s[:, :-sb], tri], axis=1)

    base2 = pl.multiple_of(base + tk, ROW_TILE)

    def diag_scores_a(r):
        w = (r + 1) * sb
        if w <= tk:
            sa_scr[r, :, 0:w] = scores(r, base, w)
        else:
            sa_scr[r, :, 0:w - tk] = scores(r, base2, w - tk)

    def diag_scores_b(r):
        if (r + 1) * sb > tk:
            sb_scr[r] = scores(r, base, tk)

    assert tq == 2 * tk
    trips = i

    @pl.when(trips == 0)
    def _():
        for r in range(nsub):
            diag_scores_a(r)
            diag_scores_b(r)

    @pl.when(trips > 0)
    def _():
        for r in range(nsub):
            sa_scr[r] = scores(r, block0(0), tk)

        def body(jj, carry):
            j = 2 * jj
            for cur, nxt, step in ((sa_scr, sb_scr, 0), (sb_scr, sa_scr, 1)):
                for r in range(nsub):
                    nxt[r] = scores(r, block0(j + step + 1), tk)
                    absorb_block(r, cur, block0(j + step))
            return carry

        lax.fori_loop(0, trips - 1, body, 0)
        j = 2 * (trips - 1)
        for r in range(nsub):
            sb_scr[r] = scores(r, block0(j + 1), tk)
            absorb_block(r, sa_scr, block0(j))
        for r in range(nsub):
            diag_scores_a(r)
            absorb_block(r, sb_scr, block0(j + 1))
            diag_scores_b(r)

    s_meta =[jnp.where(is_meta, scores(r, meta0, LANES), -jnp.inf) for r in range(nsub)]
    for r in range(nsub):
        w = (r + 1) * sb
        if w <= tk:
            absorb(r, [(s_meta[r], meta0, LANES), (mask_diag(sa_scr[r, :, 0:w]), base, w)])
        else:
            absorb(r, [(s_meta[r], meta0, LANES), (sb_scr[r], base, tk),
                       (mask_diag(sa_scr[r, :, 0:w - tk]), base2, w - tk)])

    o_ref[0] = (al_scr[:, 0:MLA_V] / al_scr[:, MLA_V:] * zs_ref[0].astype(F32)).astype(BF16)


def _params(*sem):
    return pltpu.CompilerParams(dimension_semantics=sem, vmem_limit_bytes=VMEM_LIMIT)


def kernel(x, meta_tokens, pre_norm, post_norm, gdn_w_in, gdn_conv_w, gdn_a_log, gdn_dt_bias, gdn_out_norm, gdn_w_out, kv_norm, kv_w_down, kv_latent_norm, kv_w_up, mla_w_in, mla_q_latent_norm, mla_w_q_up, mla_w_out):
    B, S, D = x.shape
    assert S % ATTN_TQ == 0 and S % (OUT_TILES * ROW_TILE) == 0 and (S // ROW_TILE + 1) % WIDE_TILES == 0
    assert gdn_w_in.shape[0] == 1 and mla_w_in.shape[0] == 1
    T = ROW_TILE
    nt = S // T + 1
    Lp = nt * T
    nch = Lp // GDN_CHUNK
    nh = GDN_V_HEADS

    head = jnp.concatenate([jnp.zeros((FRONT_PAD, D), F32), meta_tokens.astype(F32)], axis=0)
    row = lambda a: a.reshape(1, -1).astype(F32)

    w_in = gdn_w_in[0]
    w_ba_f = w_in[:, GDN_CONV_W + GDN_V_W:]
    w_b, w_a = w_ba_f[:, :nh], w_ba_f[:, nh:]
    w_ba = jnp.concatenate([w_b, w_a, w_a, w_a, jnp.zeros((D, LANES - 4 * nh), F32)], axis=-1).astype(BF16)
    eo = lambda a: jnp.concatenate([a[0::2], a[1::2]], axis=0)
    w_bat = jnp.concatenate([eo(w_b.T), eo(w_a.T)], axis=0).astype(BF16)
    a_log = gdn_a_log[0].astype(F32)
    dt_b = gdn_dt_bias[0].astype(F32)
    lane_groups = lambda a: jnp.concatenate([jnp.zeros((nh,), F32), a, a, a,
                                             jnp.zeros((LANES - 4 * nh,), F32)]).reshape(1, LANES)

    tile_spec = lambda w: pl.BlockSpec((1, T, w), lambda b, i: (b, i, 0))
    wide_spec = lambda w: pl.BlockSpec((1, WIDE_TILES * T, w), lambda b, i: (b, i, 0))

    x_specs = [pl.BlockSpec((1, T, D), lambda b, i, j=j: (b, jnp.maximum(WIDE_TILES * i + j - 1, 0), 0))
               for j in range(WIDE_TILES)]
    gate_spec = lambda r: pl.BlockSpec((1, WIDE_TILES * T // GDN_CHUNK, r, LANES), lambda b, i: (b, i, 0, 0))
    q, k, v, zs, gcol, grow, egl = pl.pallas_call(
        _gdn_in_kernel,
        grid=(B, nt // WIDE_TILES),
        in_specs=[_const_spec((T, D)), _const_spec((1, D)),
                  _const_spec(w_in.shape), _const_spec((D, LANES)), _const_spec((2 * nh, D)),
                  _const_spec((GDN_CONV_TAPS, GDN_CONV_W)),
                  _const_spec((1, LANES)), _const_spec((1, LANES)), _const_spec((nh, 1)), _const_spec((nh, 1))]
        + x_specs,
        out_specs=[wide_spec(GDN_QK_W), wide_spec(GDN_QK_W), wide_spec(GDN_V_W), wide_spec(GDN_V_W),
                   wide_spec(LANES), gate_spec(2 * nh), gate_spec(nh)],
        out_shape=[jax.ShapeDtypeStruct((B, Lp, GDN_QK_W), BF16),
                   jax.ShapeDtypeStruct((B, Lp, GDN_QK_W), BF16),
                   jax.ShapeDtypeStruct((B, Lp, GDN_V_W), BF16),
                   jax.ShapeDtypeStruct((B, Lp, GDN_V_W), BF16),
                   jax.ShapeDtypeStruct((B, Lp, LANES), F32),
                   jax.ShapeDtypeStruct((B, nch, 2 * nh, LANES), F32),
                   jax.ShapeDtypeStruct((B, nch, nh, LANES), F32)],
        scratch_shapes=[pltpu.VMEM((HALO_ROWS, GDN_CONV_W), F32),
                        pltpu.VMEM((CONV_COLS // GDN_HEAD, HALO_ROWS + T, GDN_HEAD), F32)],
        compiler_params=_params("arbitrary", "arbitrary"),
        name="gdn_in",
    )(head, row(pre_norm[0]), w_in.astype(BF16), w_ba, w_bat, gdn_conv_w[0].astype(F32),
      lane_groups(a_log), lane_groups(dt_b), eo(a_log).reshape(nh, 1), eo(dt_b).reshape(nh, 1),
      *([x] * WIDE_TILES))

    npair = nh // 2
    tb, wk, aqk, kdt = pl.pallas_call(
        _gdn_prep_kernel,
        grid=(B, nt // WIDE_TILES),
        in_specs=[wide_spec(GDN_QK_W), wide_spec(GDN_QK_W), wide_spec(LANES),
                  pl.BlockSpec((1, WIDE_TILES * T // GDN_CHUNK, 2 * nh, LANES), lambda b, i: (b, i, 0, 0))],
        out_specs=[wide_spec(GDN_QK_W), wide_spec(GDN_V_W), wide_spec(GDN_QK_W),
                   pl.BlockSpec((1, WIDE_TILES * T // GDN_CHUNK, npair, GDN_HEAD, LANES),
                                lambda b, i: (b, i, 0, 0, 0))],
        out_shape=[jax.ShapeDtypeStruct((B, Lp, GDN_QK_W), BF16),
                   jax.ShapeDtypeStruct((B, Lp, GDN_V_W), BF16),
                   jax.ShapeDtypeStruct((B, Lp, GDN_QK_W), BF16),
                   jax.ShapeDtypeStruct((B, nch, npair, GDN_HEAD, LANES), BF16)],
        compiler_params=_params("parallel", "parallel"),
        name="gdn_prep",
    )(q, k, gcol, grow)

    cpt = T // GDN_CHUNK
    chunk_spec = lambda w: pl.BlockSpec((B, T, w), lambda n: (0, n, 0))
    o_gdn = pl.pallas_call(
        _gdn_scan_kernel,
        grid=(nt,),
        in_specs=[chunk_spec(GDN_QK_W), chunk_spec(GDN_V_W), chunk_spec(GDN_QK_W),
                  pl.BlockSpec((B, cpt, npair, GDN_HEAD, LANES), lambda n: (0, n, 0, 0, 0)),
                  chunk_spec(GDN_QK_W), chunk_spec(GDN_V_W), chunk_spec(GDN_V_W), chunk_spec(LANES),
                  pl.BlockSpec((B, cpt, nh, LANES), lambda n: (0, n, 0, 0)),
                  _const_spec((1, GDN_HEAD))],
        out_specs=chunk_spec(GDN_V_W),
        out_shape=jax.ShapeDtypeStruct((B, Lp, GDN_V_W), BF16),
        scratch_shapes=[pltpu.VMEM((B, nh, GDN_HEAD, GDN_HEAD), F32)],
        compiler_params=_params("arbitrary"),
        name="gdn_scan",
    )(tb, wk, aqk, kdt, q, v, zs, gcol, egl, row(gdn_out_norm[0]))

    h1 = pl.pallas_call(
        _out_proj_first_kernel,
        grid=(B, nt // WIDE_TILES),
        in_specs=[wide_spec(GDN_V_W), _const_spec((GDN_V_W, D)), _const_spec((1, D)), _const_spec((T, D))]
        + x_specs,
        out_specs=wide_spec(D),
        out_shape=jax.ShapeDtypeStruct((B, Lp, D), F32),
        compiler_params=_params("parallel", "arbitrary"),
        name="gdn_out",
    )(o_gdn, gdn_w_out[0].astype(BF16), row(post_norm[0]), head, *([x] * WIDE_TILES))

    w_in1 = mla_w_in[0]
    zw = w_in1.shape[1] - MLA_Q_RANK
    half = MLA_ROPE // 2
    rot = lambda w: jnp.concatenate([-w[..., half:], w[..., :half]], axis=-1)
    lane_pad = lambda w: jnp.pad(w, [(0, 0)] * (w.ndim - 1) + [(0, LANES - w.shape[-1])])
    wq = mla_w_q_up[0].reshape(MLA_Q_RANK, MLA_HEADS, MLA_QK)
    w_qn = wq[..., :MLA_NOPE].reshape(MLA_Q_RANK, MLA_HEADS * MLA_NOPE).astype(BF16)
    w_qr = lane_pad(wq[..., MLA_NOPE:]).reshape(MLA_Q_RANK, MLA_HEADS * LANES).astype(BF16)
    w_qrr = lane_pad(rot(wq[..., MLA_NOPE:])).reshape(MLA_Q_RANK, MLA_HEADS * LANES).astype(BF16)
    wkd_r = kv_w_down[:, MLA_KV_RANK:]
    w_kd = jnp.concatenate([kv_w_down[:, :MLA_KV_RANK], lane_pad(wkd_r), lane_pad(rot(wkd_r))],
                           axis=-1).astype(BF16)
    wku = kv_w_up.reshape(MLA_KV_RANK, MLA_HEADS, MLA_NOPE + MLA_V)
    w_kvu = jnp.concatenate([wku[..., :MLA_NOPE].reshape(MLA_KV_RANK, -1),
                             wku[..., MLA_NOPE:].reshape(MLA_KV_RANK, -1)], axis=-1).astype(BF16)

    inv = ROPE_THETA ** (-jnp.arange(0, MLA_ROPE, 2, dtype=F32) / MLA_ROPE)
    pos = (jnp.arange(Lp, dtype=jnp.int32) - FRONT_PAD).astype(F32)
    ang = pos[:, None] * inv[None, :]
    zpad = jnp.zeros((Lp, LANES - MLA_ROPE), F32)
    cosp = jnp.concatenate([jnp.cos(ang), jnp.cos(ang), zpad], axis=-1)
    sinp = jnp.concatenate([jnp.sin(ang), jnp.sin(ang), zpad], axis=-1)

    head_tile = lambda w: pl.BlockSpec((1, MLA_HEADS, T, w), lambda b, i: (b, 0, i, 0))
    q1, k1, v1, zs1 = pl.pallas_call(
        _mla_in_kernel,
        grid=(B, nt),
        in_specs=[tile_spec(D), _const_spec((1, D)), _const_spec((1, D)),
                  _const_spec(w_in1.shape), _const_spec((1, MLA_Q_RANK)),
                  _const_spec((MLA_Q_RANK, MLA_HEADS * MLA_NOPE)),
                  _const_spec((MLA_Q_RANK, MLA_HEADS * LANES)),
                  _const_spec((MLA_Q_RANK, MLA_HEADS * LANES)),
                  _const_spec((D, 3 * LANES)), _const_spec((1, MLA_KV_RANK)),
                  _const_spec((MLA_KV_RANK, MLA_HEADS * (MLA_NOPE + MLA_V))),
                  pl.BlockSpec((T, LANES), lambda b, i: (i, 0)),
                  pl.BlockSpec((T, LANES), lambda b, i: (i, 0))],
        out_specs=[pl.BlockSpec((1, MLA_HEADS, T, MLA_QK), lambda b, i: (b, 0, jnp.maximum(i - 1, 0), 0)),
                   head_tile(MLA_QK), head_tile(MLA_V),
                   pl.BlockSpec((1, T, zw), lambda b, i: (b, jnp.maximum(i - 1, 0), 0))],
        out_shape=[jax.ShapeDtypeStruct((B, MLA_HEADS, S, MLA_QK), BF16),
                   jax.ShapeDtypeStruct((B, MLA_HEADS, Lp, MLA_QK), BF16),
                   jax.ShapeDtypeStruct((B, MLA_HEADS, Lp, MLA_V), BF16),
                   jax.ShapeDtypeStruct((B, S, zw), BF16)],
        compiler_params=_params("arbitrary", "arbitrary"),
        name="mla_in",
    )(h1, row(pre_norm[1]), row(kv_norm), w_in1.astype(BF16), row(mla_q_latent_norm[0]), w_qn, w_qr, w_qrr,
      w_kd, row(kv_latent_norm), w_kvu, cosp, sinp)

    o_attn = pl.pallas_call(
        _attn_kernel,
        grid=(B, MLA_HEADS, S // ATTN_TQ),
        in_specs=[pl.BlockSpec((1, 1, ATTN_TQ, MLA_QK), lambda b, h, i: (b, h, i, 0)),
                  pl.BlockSpec((1, 1, Lp, MLA_QK), lambda b, h, i: (b, h, 0, 0)),
                  pl.BlockSpec((1, 1, Lp, MLA_V), lambda b, h, i: (b, h, 0, 0)),
                  pl.BlockSpec((1, ATTN_TQ, MLA_V), lambda b, h, i: (b, i, h))],
        out_specs=pl.BlockSpec((1, ATTN_TQ, MLA_V), lambda b, h, i: (b, i, h)),
        out_shape=jax.ShapeDtypeStruct((B, S, MLA_HEADS * MLA_V), BF16),
        scratch_shapes=[pltpu.VMEM((ATTN_TQ, LANES), F32), pltpu.VMEM((ATTN_TQ, MLA_V + LANES), F32),
                        pltpu.VMEM((ATTN_TQ, LANES), F32),
                        pltpu.VMEM((ATTN_TQ // ATTN_SUB, ATTN_SUB, ATTN_TK), F32),
                        pltpu.VMEM((ATTN_TQ // ATTN_SUB, ATTN_SUB, ATTN_TK), F32),
                        pltpu.VMEM((ATTN_TQ // ATTN_SUB, ATTN_SUB, ATTN_TK), BF16)],
        compiler_params=_params("parallel", "parallel", "arbitrary"),
        name="mla_attn",
    )(q1, k1, v1, zs1)

    out = pl.pallas_call(
        _out_proj_last_kernel,
        grid=(B, S // (OUT_TILES * T)),
        in_specs=[pl.BlockSpec((1, OUT_TILES * T, MLA_HEADS * MLA_V), lambda b, i: (b, i, 0)),
                  _const_spec((MLA_HEADS * MLA_V, D)), _const_spec((1, D))]
        + [pl.BlockSpec((1, T, D), lambda b, i, j=j: (b, OUT_TILES * i + j + 1, 0)) for j in range(OUT_TILES)],
        out_specs=pl.BlockSpec((1, OUT_TILES * T, D), lambda b, i: (b, i, 0)),
        out_shape=jax.ShapeDtypeStruct((B, S, D), x.dtype),
        compiler_params=_params("parallel", "arbitrary"),
        name="mla_out",
    )(o_attn, mla_w_out[0].astype(BF16), row(post_norm[1]), *([h1] * OUT_TILES))
    return out
```

```python
import functools
import math

import jax
import jax.numpy as jnp
from jax import lax
from jax.experimental import pallas as pl
from jax.experimental.pallas import tpu as pltpu

NORM_EPS = 1e-6
N_META_ROWS = 16

GDN_QK_HEADS = 8
GDN_V_HEADS = 16
GDN_HEAD = 128
GDN_CONV_TAPS = 4
GDN_CHUNK = 64
GDN_QK_W = GDN_QK_HEADS * GDN_HEAD
GDN_V_W = GDN_V_HEADS * GDN_HEAD
GDN_CONV_W = 2 * GDN_QK_W + GDN_V_W

MLA_HEADS = 16
MLA_NOPE = 128
MLA_ROPE = 64
MLA_V = 128
MLA_Q_RANK = 256
MLA_KV_RANK = 128
MLA_QK = MLA_NOPE + MLA_ROPE
ROPE_THETA = 10000.0

LANES = 128
ROW_TILE = 256
FRONT_PAD = ROW_TILE - N_META_ROWS
WIDE_TILES = 3
OUT_TILES = 4
CONV_COLS = 512
HALO_ROWS = 8
ATTN_TQ = 2048
ATTN_SUB = 256
ATTN_TK = 1024
ATTN_GROUP = 128
ATTN_ONES_ROWS = 16
VMEM_LIMIT = 60 * 1024 * 1024

F32 = jnp.float32
BF16 = jnp.bfloat16


def _dot(a, b):
    return jnp.dot(a, b, preferred_element_type=F32)


def _dot_nt(a, b):
    return lax.dot_general(a, b, (((1,), (1,)), ((), ())), preferred_element_type=F32)


def _dot_tn(a, b):
    return lax.dot_general(a, b, (((0,), (0,)), ((), ())), preferred_element_type=F32)


def _dot_exact(a, b):
    return jnp.dot(a, b, preferred_element_type=F32, precision=lax.Precision.HIGHEST)


def _silu(x):
    return x * jax.nn.sigmoid(x)


def _softplus(x):
    return jnp.maximum(x, 0.0) + jnp.log1p(jnp.exp(-jnp.abs(x)))


def _rms_scale(x):
    return lax.rsqrt(jnp.mean(x * x, axis=-1, keepdims=True) + NORM_EPS)


def _const_spec(shape):
    nd = len(shape)
    return pl.BlockSpec(shape, lambda *_: (0,) * nd, pipeline_mode=pl.Buffered(1))


def _gdn_in_kernel(head_ref, gain_ref, win_ref, wba_ref, wbat_ref, convw_ref,
                   arow_ref, dtrow_ref, acol_ref, dtcol_ref, *refs):
    x_refs = refs[:WIDE_TILES]
    outs = refs[WIDE_TILES:WIDE_TILES + 7]
    halo_scr, buf_scr = refs[WIDE_TILES + 7:]
    i = pl.program_id(1)

    @pl.when(i == 0)
    def _():
        halo_scr[...] = jnp.zeros_like(halo_scr)

    ri = lax.broadcasted_iota(jnp.int32, (ROW_TILE, ROW_TILE), 0)
    ci = lax.broadcasted_iota(jnp.int32, (ROW_TILE, ROW_TILE), 1)
    same = (ri // GDN_CHUNK) == (ci // GDN_CHUNK)
    prefix = (jnp.where(same & (ri >= ci), 1.0, 0.0).astype(F32),
              jnp.where(same & (ri <= ci), 1.0, 0.0).astype(F32),
              jnp.where(same, 1.0, 0.0).astype(F32))
    for t, x_ref in enumerate(x_refs):
        x = x_ref[0] if t else jnp.where(i == 0, head_ref[...], x_ref[0])
        _gdn_in_tile(t, x, gain_ref, win_ref, wba_ref, wbat_ref, convw_ref,
                     arow_ref, dtrow_ref, acol_ref, dtcol_ref, prefix, outs, halo_scr, buf_scr)


def _gdn_in_tile(t, x, gain_ref, win_ref, wba_ref, wbat_ref, convw_ref,
                 arow_ref, dtrow_ref, acol_ref, dtcol_ref, prefix, outs, halo_scr, buf_scr):
    q_ref, k_ref, v_ref, zs_ref, gcol_ref, grow_ref, egl_ref = outs
    lower, upper, block = prefix
    rt = slice(t * ROW_TILE, (t + 1) * ROW_TILE)
    c0 = t * (ROW_TILE // GDN_CHUNK)
    hn = (x * _rms_scale(x) * gain_ref[...]).astype(BF16)

    lo = HALO_ROWS - (GDN_CONV_TAPS - 1)
    for c in range(GDN_CONV_W // CONV_COLS):
        cs = slice(c * CONV_COLS, (c + 1) * CONV_COLS)
        p = _dot(hn, win_ref[:, cs])
        for hh in range(CONV_COLS // GDN_HEAD):
            col = c * CONV_COLS + hh * GDN_HEAD
            buf_scr[hh, 0:HALO_ROWS, :] = halo_scr[:, col:col + GDN_HEAD]
            buf_scr[hh, HALO_ROWS:HALO_ROWS + ROW_TILE, :] = p[:, hh * GDN_HEAD:(hh + 1) * GDN_HEAD]
            halo_scr[:, col:col + GDN_HEAD] = p[ROW_TILE - HALO_ROWS:, hh * GDN_HEAD:(hh + 1) * GDN_HEAD]
            yh = convw_ref[0:1, col:col + GDN_HEAD] * buf_scr[hh, lo:lo + ROW_TILE, :]
            for j in range(1, GDN_CONV_TAPS):
                yh = yh + convw_ref[j:j + 1, col:col + GDN_HEAD] * buf_scr[hh, lo + j:lo + j + ROW_TILE, :]
            yh = _silu(yh)
            if col < 2 * GDN_QK_W:
                r = lax.rsqrt(jnp.sum(yh * yh, axis=-1, keepdims=True) + NORM_EPS)
                if col < GDN_QK_W:
                    q_ref[0, rt, col:col + GDN_HEAD] = (yh * (r * (GDN_HEAD ** -0.5))).astype(BF16)
                else:
                    k_ref[0, rt, col - GDN_QK_W:col - GDN_QK_W + GDN_HEAD] = (yh * r).astype(BF16)
            else:
                v_ref[0, rt, col - 2 * GDN_QK_W:col - 2 * GDN_QK_W + GDN_HEAD] = yh.astype(BF16)

    for c in range(GDN_V_W // CONV_COLS):
        cs = slice(c * CONV_COLS, (c + 1) * CONV_COLS)
        ws = slice(GDN_CONV_W + c * CONV_COLS, GDN_CONV_W + (c + 1) * CONV_COLS)
        zs_ref[0, rt, cs] = _silu(_dot(hn, win_ref[:, ws])).astype(BF16)

    nh = GDN_V_HEADS
    ba = _dot(hn, wba_ref[...])
    bat = _dot_nt(wbat_ref[...], hn)
    beta_c = 1.0 / (1.0 + jnp.exp(-ba))
    g_c = -jnp.exp(arow_ref[...]) * _softplus(ba + dtrow_ref[...])
    beta_r = 1.0 / (1.0 + jnp.exp(-bat[0:nh, :]))
    g_r = -jnp.exp(acol_ref[...]) * _softplus(bat[nh:2 * nh, :] + dtcol_ref[...])

    gc_c = _dot_exact(lower, g_c)
    gl_c = _dot_exact(block, g_c)
    gc_r = _dot_exact(g_r, upper)
    grp = lax.broadcasted_iota(jnp.int32, (ROW_TILE, LANES), 1) // nh
    gcol_ref[0, rt, :] = jnp.where(grp == 0, beta_c,
                                   jnp.where(grp == 1, gc_c,
                                             jnp.where(grp == 2, jnp.exp(gc_c),
                                                       jnp.where(grp == 3, jnp.exp(gl_c - gc_c), 0.0))))
    npair = nh // 2
    for kind, xr in enumerate((gc_r, beta_r, beta_r * jnp.exp(gc_r))):
        for c in range(ROW_TILE // GDN_CHUNK):
            cs = slice(c * GDN_CHUNK, (c + 1) * GDN_CHUNK)
            grow_ref[0, c0 + c, kind * npair:(kind + 1) * npair, :] = jnp.concatenate(
                [xr[0:npair, cs], xr[npair:nh, cs]], axis=-1)
    for c in range(ROW_TILE // GDN_CHUNK):
        cs = slice(c * GDN_CHUNK, (c + 1) * GDN_CHUNK)
        last = (c + 1) * GDN_CHUNK - 1
        gl = jnp.broadcast_to(gc_r[:, last:last + 1], (nh, LANES))
        egl_ref[0, c0 + c] = jnp.exp(gl)
        to_end = jnp.exp(gl[:, 0:GDN_CHUNK] - gc_r[:, cs])
        grow_ref[0, c0 + c, 3 * npair:4 * npair, :] =jnp.concatenate([to_end[0:npair], to_end[npair:nh]], axis=-1)


def _gdn_prep_kernel(q_ref, k_ref, gcol_ref, grow_ref, tb_ref, w_ref, a_ref, kdt_ref):
    C = GDN_CHUNK
    nh = GDN_V_HEADS
    npair = nh // 2
    row = lax.broadcasted_iota(jnp.int32, (C, LANES), 0)
    lane = lax.broadcasted_iota(jnp.int32, (C, LANES), 1)
    left = lane < C
    col = jnp.where(left, lane, lane - C)
    incl = row >= col
    strict = row > col
    eye = jnp.where(row == col, 1.0, 0.0).astype(F32)
    diag8 = strict & ((row // 8) == (col // 8))
    merges = tuple(strict & ((row // (2 * s)) == (col // (2 * s))) & ((row // s) != (col // s))
                   for s in (8, 16, 32))
    zero_b = jnp.zeros((C, LANES), BF16)

    def blockdiag(x):
        xb = x.astype(BF16)
        return jnp.concatenate([jnp.where(left, xb, zero_b), jnp.where(left, zero_b, xb)], axis=0)

    def pair_cols(g, base, p_):
        return jnp.take_along_axis(g, jnp.where(left, base + 2 * p_, base + 2 * p_ + 1), axis=1)

    cpt = ROW_TILE // C
    streams = [(c, p_) for c in range(cpt) for p_ in range(npair)]
    zero_k = jnp.zeros((C, GDN_HEAD), BF16)

    def row_tile(j, carry):
        rows = lambda c: pl.ds(pl.multiple_of(j * ROW_TILE + c * C, C), C)
        ms, kps, bege, betar = [], [], [], []
        for c, p_ in streams:
            rs, cj = rows(c), j * cpt + c
            ps = slice(p_ * GDN_HEAD, (p_ + 1) * GDN_HEAD)
            qp = q_ref[0, rs, ps]
            kp = k_ref[0, rs, ps]
            g = gcol_ref[0, rs, :]
            both = _dot_nt(jnp.concatenate([qp, kp], axis=0), jnp.concatenate([kp, kp], axis=0))
            qk2, kk2 = both[0:C], both[C:2 * C]
            decay = jnp.exp(jnp.where(incl, pair_cols(g, nh, p_) - grow_ref[0, cj, p_:p_ + 1, :], -jnp.inf))
            ms.append(jnp.where(strict, pair_cols(g, 0, p_) * kk2 * decay, 0.0))
            a_ref[0, rs, ps] = (qk2 * decay).astype(BF16)
            kpf = kp.astype(F32)
            kdt_ref[0, cj, p_] = (jnp.concatenate([kpf, kpf], axis=0).T
                                  * grow_ref[0, cj, 3 * npair + p_:3 * npair + p_ + 1, :]).astype(BF16)
            betar.append(grow_ref[0, cj, npair + p_:npair + p_ + 1, :])
            kps.append(kp)
            bege.append(grow_ref[0, cj, 2 * npair + p_:2 * npair + p_ + 1, :])

        m8 = [jnp.where(diag8, m, 0.0) for m in ms]
        q2 = [_dot(x.astype(BF16), blockdiag(x)) for x in m8]
        pinv = [eye - x for x in m8]
        pinv = [p + _dot(p.astype(BF16), blockdiag(y)) for p, y in zip(pinv, q2)]
        q4 = [_dot(y.astype(BF16), blockdiag(y)) for y in q2]
        pinv = [p + _dot(p.astype(BF16), blockdiag(y)) for p, y in zip(pinv, q4)]
        for mask in merges:
            cp = [_dot(jnp.where(mask, m, 0.0).astype(BF16), blockdiag(p)) for m, p in zip(ms, pinv)]
            pinv = [p - _dot(p.astype(BF16), blockdiag(y)) for p, y in zip(pinv, cp)]

        for (c, p_), p, kp, bg, br in zip(streams, pinv, kps, bege, betar):
            rs = rows(c)
            ps = slice(p_ * GDN_HEAD, (p_ + 1) * GDN_HEAD)
            kbd = jnp.concatenate([jnp.concatenate([kp, zero_k], axis=1),
                                   jnp.concatenate([zero_k, kp], axis=1)], axis=0)
            w_ref[0, rs, 2 * p_ * GDN_HEAD:(2 * p_ + 2) * GDN_HEAD] = _dot((p * bg).astype(BF16), kbd).astype(BF16)
            tb_ref[0, rs, ps] = (p * br).astype(BF16)
        return carry

    lax.fori_loop(0, q_ref.shape[1] // ROW_TILE, row_tile, 0)


def _gdn_scan_kernel(tb_ref, w_ref, a_ref, kdt_ref, q_ref, v_ref, zs_ref, gcol_ref, egl_ref, onorm_ref, o_ref,
                     state_scr):
    @pl.when(pl.program_id(0) == 0)
    def _():
        state_scr[...] = jnp.zeros_like(state_scr)

    C = GDN_CHUNK
    nh = GDN_V_HEADS
    nb = q_ref.shape[0]
    onorm = onorm_ref[...]
    zero_v = jnp.zeros((C, GDN_HEAD), BF16)
    ones_sq = jnp.ones((GDN_HEAD, GDN_HEAD), BF16)

    def blockdiag(x0, x1):
        return jnp.concatenate([jnp.concatenate([x0, zero_v], axis=1),
                                jnp.concatenate([zero_v, x1], axis=1)], axis=0)

    pairs = [(b, p_) for b in range(nb) for p_ in range(nh // 2)]
    heads = [(b, h) for b, p_ in pairs for h in (2 * p_, 2 * p_ + 1)]
    hsl = lambda h: slice(h * GDN_HEAD, (h + 1) * GDN_HEAD)

    def chunk(c, carry):
        rs = pl.ds(pl.multiple_of(c * C, C), C)
        u2 = [_dot(tb_ref[b, rs, hsl(p_)], blockdiag(v_ref[b, rs, hsl(2 * p_)], v_ref[b, rs, hsl(2 * p_ + 1)]))
              for b, p_ in pairs]
        u = {(b, 2 * p_ + j): x[:, j * GDN_HEAD:(j + 1) * GDN_HEAD]
             for (b, p_), x in zip(pairs, u2) for j in (0, 1)}
        s_old = {bh: state_scr[bh[0], bh[1]] for bh in heads}
        wq = {(b, h): _dot(jnp.concatenate([w_ref[b, rs, hsl(h)], q_ref[b, rs, hsl(h // 2)]], axis=0),
                           s_old[(b, h)].astype(BF16)) for b, h in heads}
        vnb = {bh: (u[bh] - wq[bh][0:C]).astype(BF16) for bh in heads}
        od = [_dot(jnp.concatenate([a_ref[b, rs, hsl(p_)], kdt_ref[b, c, p_]], axis=0),
                   blockdiag(vnb[(b, 2 * p_)], vnb[(b, 2 * p_ + 1)])) for b, p_ in pairs]
        os = {}
        for (b, p_), x in zip(pairs, od):
            for j in (0, 1):
                h = 2 * p_ + j
                js = slice(j * GDN_HEAD, (j + 1) * GDN_HEAD)
                erow = (h % 2) * (nh // 2) + h // 2
                state_scr[b, h] = s_old[(b, h)] * egl_ref[b, c, erow:erow + 1, :] + x[C:, js]
                os[(b, h)] = gcol_ref[b, rs, 2 * nh + h:2 * nh + h + 1] * wq[(b, h)][C:2 * C] + x[0:C, js]
        sq = {bh: _dot((o * o).astype(BF16), ones_sq) for bh, o in os.items()}
        for (b, h), o in os.items():
            on = o * lax.rsqrt(sq[(b, h)] * (1.0 / GDN_HEAD) + NORM_EPS) * onorm
            o_ref[b, rs, hsl(h)] = (on * zs_ref[b, rs, hsl(h)].astype(F32)).astype(BF16)
        return carry

    lax.fori_loop(0, q_ref.shape[1] // C, chunk, 0)


def _out_proj_first_kernel(o_ref, w_ref, gain_ref, head_ref, *refs):
    x_refs, h_ref = refs[:-1], refs[-1]
    y = _dot(o_ref[0], w_ref[...])
    yn = y * _rms_scale(y) * gain_ref[...]
    for j, x_ref in enumerate(x_refs):
        res = x_ref[0] if j else jnp.where(pl.program_id(1) == 0, head_ref[...], x_ref[0])
        h_ref[0, j * ROW_TILE:(j + 1) * ROW_TILE, :] = res + yn[j * ROW_TILE:(j + 1) * ROW_TILE]


def _out_proj_last_kernel(o_ref, w_ref, gain_ref, *refs):
    h_refs, out_ref = refs[:-1], refs[-1]
    y = _dot(o_ref[0], w_ref[...])
    yn = y * _rms_scale(y) * gain_ref[...]
    for j, h_ref in enumerate(h_refs):
        out_ref[0, j * ROW_TILE:(j + 1) * ROW_TILE, :] = h_ref[0] + yn[j * ROW_TILE:(j + 1) * ROW_TILE]


def _mla_in_kernel(h_ref, pre_ref, kvn_ref, win_ref, qln_ref, wqn_ref, wqr_ref, wqrr_ref,
                   wkd_ref, kvln_ref, wku_ref, wvt_ref, cos_ref, sin_ref,
                   q_ref, kk_ref, vt_ref, zs_ref):
    h = h_ref[0]
    hr = h * _rms_scale(h)
    hn = (hr * pre_ref[...]).astype(BF16)
    hk = (hr * kvn_ref[...]).astype(BF16)
    cosp = cos_ref[...]
    sinp = sin_ref[...]

    for c in range((win_ref.shape[1] - MLA_Q_RANK) // CONV_COLS):
        cs = slice(c * CONV_COLS, (c + 1) * CONV_COLS)
        ws = slice(MLA_Q_RANK + c * CONV_COLS, MLA_Q_RANK + (c + 1) * CONV_COLS)
        zs_ref[0, :, cs] = _silu(_dot(hn, win_ref[:, ws])).astype(BF16)

    cq = _dot(hn, win_ref[:, 0:MLA_Q_RANK])
    cq = (cq * _rms_scale(cq) * qln_ref[...]).astype(BF16)
    scale = MLA_QK ** -0.5 * math.log2(math.e)
    for hd in range(MLA_HEADS):
        hs = slice(hd * LANES, (hd + 1) * LANES)
        qn = _dot(cq, wqn_ref[:, hs])
        qr = _dot(cq, wqr_ref[:, hs]) * cosp + _dot(cq, wqrr_ref[:, hs]) * sinp
        q_ref[0, hd, :, 0:MLA_NOPE] = (qn * scale).astype(BF16)
        q_ref[0, hd, :, MLA_NOPE:MLA_QK] = (qr[:, 0:MLA_ROPE] * scale).astype(BF16)

    ckr = _dot(hk, wkd_ref[...])
    lat = ckr[:, 0:MLA_KV_RANK]
    ckv = (lat * _rms_scale(lat) * kvln_ref[...]).astype(BF16)
    kr = ckr[:, LANES:2 * LANES] * cosp + ckr[:, 2 * LANES:3 * LANES] * sinp
    krb = kr[:, 0:MLA_ROPE].astype(BF16)
    ones_rows = jnp.ones((ATTN_ONES_ROWS, ROW_TILE), BF16)
    for hd in range(MLA_HEADS):
        kn = _dot(ckv, wku_ref[:, hd * MLA_NOPE:(hd + 1) * MLA_NOPE])
        kk_ref[0, hd, :, 0:MLA_NOPE] = kn.astype(BF16)
        kk_ref[0, hd, :, MLA_NOPE:MLA_QK] = krb
        vt = _dot_nt(wvt_ref[hd * MLA_V:(hd + 1) * MLA_V, :], ckv)
        vt_ref[0, hd, 0, 0:MLA_V, :] = vt.astype(BF16)
        vt_ref[0, hd, 0, MLA_V:MLA_V + ATTN_ONES_ROWS, :] = ones_rows


def _attn_kernel(q_ref, k_ref, vt_ref, zs_ref, o_ref, m_scr, al_scr, sa_scr, sb_scr, ma_scr, mb_scr, p_scr):
    i = pl.program_id(2)
    tq, sb, tk = ATTN_TQ, ATTN_SUB, ATTN_TK
    m_scr[...] = jnp.full_like(m_scr, -jnp.inf)
    al_scr[...] = jnp.zeros_like(al_scr)

    def rows(r):
        return slice(r * sb, (r + 1) * sb)

    def scores(r, k0, nk):
        return _dot_nt(k_ref[0, 0, pl.ds(k0, nk), :], q_ref[0, 0, rows(r), :])

    def weighted_values(p, k0, nk):
        if nk < ROW_TILE:
            return _dot(vt_ref[0, 0, 0, :, k0:k0 + nk], p[...])
        kb = k0 // ROW_TILE
        out = _dot(vt_ref[0, 0, kb], p[0:ROW_TILE, :])
        for jb in range(1, nk // ROW_TILE):
            out = out + _dot(vt_ref[0, 0, kb + jb], p[jb * ROW_TILE:(jb + 1) * ROW_TILE, :])
        return out

    def absorb(r, parts):
        m_old = m_scr[r]
        m_new = m_old
        for s, _, _ in parts:
            m_new = jnp.maximum(m_new, jnp.max(s, axis=0, keepdims=True))
        al = jnp.exp2(m_old - m_new)[0:1, :] * al_scr[r]
        for s, k0, nk in parts:
            al = al + weighted_values(jnp.exp2(s - m_new[0:1, :]).astype(BF16), k0, nk)
        al_scr[r] = al
        m_scr[r] = m_new

    def stage_block(r, bufs, k0):
        s_buf, mx_buf = bufs
        s = scores(r, k0, tk)
        s_buf[r] = s
        mx_buf[r] = jnp.broadcast_to(jnp.max(s, axis=0, keepdims=True), mx_buf.shape[1:])

    def absorb_block(r, bufs, k0):
        s_buf, mx_buf = bufs
        m_old = m_scr[r]
        m_new = jnp.maximum(m_old, mx_buf[r])
        for g in range(tk // ATTN_GROUP):
            gr = slice(g * ATTN_GROUP, (g + 1) * ATTN_GROUP)
            p_scr[r, gr, :] = jnp.exp2(s_buf[r, gr, :] - m_new[0:1, :]).astype(BF16)
        al_scr[r] = jnp.exp2(m_old - m_new)[0:1, :] * al_scr[r] + weighted_values(p_scr.at[r], k0, tk)
        m_scr[r] = m_new

    nsub = tq // sb
    block0 = lambda j: pl.multiple_of(ROW_TILE + j * tk, ROW_TILE)

    ki = lax.broadcasted_iota(jnp.int32, (sb, sb), 0)
    qi = lax.broadcasted_iota(jnp.int32, (sb, sb), 1)
    base = pl.multiple_of(ROW_TILE + i * tq, ROW_TILE)
    meta0 = ROW_TILE - LANES
    is_meta = lax.broadcasted_iota(jnp.int32, (LANES, sb), 0) >= FRONT_PAD - meta0

    def mask_diag(s):
        tri = jnp.where(ki <= qi, s[-sb:, :], -jnp.inf)
        return tri if s.shape[0] == sb else jnp.concatenate([s[:-sb, :], tri], axis=0)

    base2 = pl.multiple_of(base + tk, ROW_TILE)

    def diag_scores_a(r):
        w = (r + 1) * sb
        if w <= tk:
            sa_scr[r, 0:w, :] = scores(r, base, w)
        else:
            sa_scr[r, 0:w - tk, :] = scores(r, base2, w - tk)

    def diag_scores_b(r):
        if (r + 1) * sb > tk:
            sb_scr[r] = scores(r, base, tk)

    assert tq == 2 * tk
    trips = i

    @pl.when(trips == 0)
    def _():
        for r in range(nsub):
            diag_scores_a(r)
            diag_scores_b(r)

    @pl.when(trips > 0)
    def _():
        buf_a, buf_b = (sa_scr, ma_scr), (sb_scr, mb_scr)
        for r in range(nsub):
            stage_block(r, buf_a, block0(0))

        def body(jj, carry):
            j = 2 * jj
            for cur, nxt, step in ((buf_a, buf_b, 0), (buf_b, buf_a, 1)):
                for r in range(nsub):
                    stage_block(r, nxt, block0(j + step + 1))
                    absorb_block(r, cur, block0(j + step))
            return carry

        lax.fori_loop(0, trips - 1, body, 0)
        j = 2 * (trips - 1)
        for r in range(nsub):
            stage_block(r, buf_b, block0(j + 1))
            absorb_block(r, buf_a, block0(j))
        for r in range(nsub):
            diag_scores_a(r)
            absorb_block(r, buf_b, block0(j + 1))
            diag_scores_b(r)

    s_meta = [jnp.where(is_meta, scores(r, meta0, LANES), -jnp.inf) for r in range(nsub)]
    for r in range(nsub):
        w = (r + 1) * sb
        if w <= tk:
            absorb(r, [(s_meta[r], meta0, LANES), (mask_diag(sa_scr[r, 0:w, :]), base, w)])
        else:
            absorb(r, [(s_meta[r], meta0, LANES), (sb_scr[r], base, tk),
                       (mask_diag(sa_scr[r, 0:w - tk, :]), base2, w - tk)])

    for r in range(nsub):
        o_t = al_scr[r, 0:MLA_V, :] / al_scr[r, MLA_V:MLA_V + 1, :]
        o_ref[0, rows(r), :] = (o_t.T * zs_ref[0, rows(r), :].astype(F32)).astype(BF16)


def _params(*sem):
    return pltpu.CompilerParams(dimension_semantics=sem, vmem_limit_bytes=VMEM_LIMIT)


def kernel(x, meta_tokens, pre_norm, post_norm, gdn_w_in, gdn_conv_w, gdn_a_log, gdn_dt_bias, gdn_out_norm, gdn_w_out, kv_norm, kv_w_down, kv_latent_norm, kv_w_up, mla_w_in, mla_q_latent_norm, mla_w_q_up, mla_w_out):
    B, S, D = x.shape
    assert S % ATTN_TQ == 0 and S % (OUT_TILES * ROW_TILE) == 0 and (S // ROW_TILE + 1) % WIDE_TILES == 0
    assert gdn_w_in.shape[0] == 1 and mla_w_in.shape[0] == 1
    T = ROW_TILE
    nt = S // T + 1
    Lp = nt * T
    nch = Lp // GDN_CHUNK
    nh = GDN_V_HEADS

    head = jnp.concatenate([jnp.zeros((FRONT_PAD, D), F32), meta_tokens.astype(F32)], axis=0)
    row = lambda a: a.reshape(1, -1).astype(F32)

    w_in = gdn_w_in[0]
    w_ba_f = w_in[:, GDN_CONV_W + GDN_V_W:]
    w_b, w_a = w_ba_f[:, :nh], w_ba_f[:, nh:]
    w_ba = jnp.concatenate([w_b, w_a, w_a, w_a, jnp.zeros((D, LANES - 4 * nh), F32)], axis=-1).astype(BF16)
    eo = lambda a: jnp.concatenate([a[0::2], a[1::2]], axis=0)
    w_bat = jnp.concatenate([eo(w_b.T), eo(w_a.T)], axis=0).astype(BF16)
    a_log = gdn_a_log[0].astype(F32)
    dt_b = gdn_dt_bias[0].astype(F32)
    lane_groups = lambda a: jnp.concatenate([jnp.zeros((nh,), F32), a, a, a,
                                             jnp.zeros((LANES - 4 * nh,), F32)]).reshape(1, LANES)

    tile_spec = lambda w: pl.BlockSpec((1, T, w), lambda b, i: (b, i, 0))
    wide_spec = lambda w: pl.BlockSpec((1, WIDE_TILES * T, w), lambda b, i: (b, i, 0))

    x_specs = [pl.BlockSpec((1, T, D), lambda b, i, j=j: (b, jnp.maximum(WIDE_TILES * i + j - 1, 0), 0))
               for j in range(WIDE_TILES)]
    gate_spec = lambda r: pl.BlockSpec((1, WIDE_TILES * T // GDN_CHUNK, r, LANES), lambda b, i: (b, i, 0, 0))
    q, k, v, zs, gcol, grow, egl = pl.pallas_call(
        _gdn_in_kernel,
        grid=(B, nt // WIDE_TILES),
        in_specs=[_const_spec((T, D)), _const_spec((1, D)),
                  _const_spec(w_in.shape), _const_spec((D, LANES)), _const_spec((2 * nh, D)),
                  _const_spec((GDN_CONV_TAPS, GDN_CONV_W)),
                  _const_spec((1, LANES)), _const_spec((1, LANES)), _const_spec((nh, 1)), _const_spec((nh, 1))]
        + x_specs,
        out_specs=[wide_spec(GDN_QK_W), wide_spec(GDN_QK_W), wide_spec(GDN_V_W), wide_spec(GDN_V_W),
                   wide_spec(LANES), gate_spec(2 * nh), gate_spec(nh)],
        out_shape=[jax.ShapeDtypeStruct((B, Lp, GDN_QK_W), BF16),
                   jax.ShapeDtypeStruct((B, Lp, GDN_QK_W), BF16),
                   jax.ShapeDtypeStruct((B, Lp, GDN_V_W), BF16),
                   jax.ShapeDtypeStruct((B, Lp, GDN_V_W), BF16),
                   jax.ShapeDtypeStruct((B, Lp, LANES), F32),
                   jax.ShapeDtypeStruct((B, nch, 2 * nh, LANES), F32),
                   jax.ShapeDtypeStruct((B, nch, nh, LANES), F32)],
        scratch_shapes=[pltpu.VMEM((HALO_ROWS, GDN_CONV_W), F32),
                        pltpu.VMEM((CONV_COLS // GDN_HEAD, HALO_ROWS + T, GDN_HEAD), F32)],
        compiler_params=_params("arbitrary", "arbitrary"),
        name="gdn_in",
    )(head, row(pre_norm[0]), w_in.astype(BF16), w_ba, w_bat, gdn_conv_w[0].astype(F32),
      lane_groups(a_log), lane_groups(dt_b), eo(a_log).reshape(nh, 1), eo(dt_b).reshape(nh, 1),
      *([x] * WIDE_TILES))

    npair = nh // 2
    tb, wk, aqk, kdt = pl.pallas_call(
        _gdn_prep_kernel,
        grid=(B, nt // WIDE_TILES),
        in_specs=[wide_spec(GDN_QK_W), wide_spec(GDN_QK_W), wide_spec(LANES),
                  pl.BlockSpec((1, WIDE_TILES * T // GDN_CHUNK, 2 * nh, LANES), lambda b, i: (b, i, 0, 0))],
        out_specs=[wide_spec(GDN_QK_W), wide_spec(GDN_V_W), wide_spec(GDN_QK_W),
                   pl.BlockSpec((1, WIDE_TILES * T // GDN_CHUNK, npair, GDN_HEAD, LANES),
                                lambda b, i: (b, i, 0, 0, 0))],
        out_shape=[jax.ShapeDtypeStruct((B, Lp, GDN_QK_W), BF16),
                   jax.ShapeDtypeStruct((B, Lp, GDN_V_W), BF16),
                   jax.ShapeDtypeStruct((B, Lp, GDN_QK_W), BF16),
                   jax.ShapeDtypeStruct((B, nch, npair, GDN_HEAD, LANES), BF16)],
        compiler_params=_params("parallel", "parallel"),
        name="gdn_prep",
    )(q, k, gcol, grow)

    cpt = T // GDN_CHUNK
    chunk_spec = lambda w: pl.BlockSpec((B, T, w), lambda n: (0, n, 0))
    o_gdn = pl.pallas_call(
        _gdn_scan_kernel,
        grid=(nt,),
        in_specs=[chunk_spec(GDN_QK_W), chunk_spec(GDN_V_W), chunk_spec(GDN_QK_W),
                  pl.BlockSpec((B, cpt, npair, GDN_HEAD, LANES), lambda n: (0, n, 0, 0, 0)),
                  chunk_spec(GDN_QK_W), chunk_spec(GDN_V_W), chunk_spec(GDN_V_W), chunk_spec(LANES),
                  pl.BlockSpec((B, cpt, nh, LANES), lambda n: (0, n, 0, 0)),
                  _const_spec((1, GDN_HEAD))],
        out_specs=chunk_spec(GDN_V_W),
        out_shape=jax.ShapeDtypeStruct((B, Lp, GDN_V_W), BF16),
        scratch_shapes=[pltpu.VMEM((B, nh, GDN_HEAD, GDN_HEAD), F32)],
        compiler_params=_params("arbitrary"),
        name="gdn_scan",
    )(tb, wk, aqk, kdt, q, v, zs, gcol, egl, row(gdn_out_norm[0]))

    h1 = pl.pallas_call(
        _out_proj_first_kernel,
        grid=(B, nt // WIDE_TILES),
        in_specs=[wide_spec(GDN_V_W), _const_spec((GDN_V_W, D)), _const_spec((1, D)), _const_spec((T, D))]
        + x_specs,
        out_specs=wide_spec(D),
        out_shape=jax.ShapeDtypeStruct((B, Lp, D), F32),
        compiler_params=_params("parallel", "arbitrary"),
        name="gdn_out",
    )(o_gdn, gdn_w_out[0].astype(BF16), row(post_norm[0]), head, *([x] * WIDE_TILES))

    w_in1 = mla_w_in[0]
    zw = w_in1.shape[1] - MLA_Q_RANK
    half = MLA_ROPE // 2
    rot = lambda w: jnp.concatenate([-w[..., half:], w[..., :half]], axis=-1)
    lane_pad = lambda w: jnp.pad(w, [(0, 0)] * (w.ndim - 1) + [(0, LANES - w.shape[-1])])
    wq = mla_w_q_up[0].reshape(MLA_Q_RANK, MLA_HEADS, MLA_QK)
    w_qn = wq[..., :MLA_NOPE].reshape(MLA_Q_RANK, MLA_HEADS * MLA_NOPE).astype(BF16)
    w_qr = lane_pad(wq[..., MLA_NOPE:]).reshape(MLA_Q_RANK, MLA_HEADS * LANES).astype(BF16)
    w_qrr = lane_pad(rot(wq[..., MLA_NOPE:])).reshape(MLA_Q_RANK, MLA_HEADS * LANES).astype(BF16)
    wkd_r = kv_w_down[:, MLA_KV_RANK:]
    w_kd = jnp.concatenate([kv_w_down[:, :MLA_KV_RANK], lane_pad(wkd_r), lane_pad(rot(wkd_r))],
                           axis=-1).astype(BF16)
    wku = kv_w_up.reshape(MLA_KV_RANK, MLA_HEADS, MLA_NOPE + MLA_V)
    w_ku = wku[..., :MLA_NOPE].reshape(MLA_KV_RANK, MLA_HEADS * MLA_NOPE).astype(BF16)
    w_vt = jnp.transpose(wku[..., MLA_NOPE:], (1, 2, 0)).reshape(MLA_HEADS * MLA_V, MLA_KV_RANK).astype(BF16)
    vt_rows = MLA_V + ATTN_ONES_ROWS

    inv = ROPE_THETA ** (-jnp.arange(0, MLA_ROPE, 2, dtype=F32) / MLA_ROPE)
    pos = (jnp.arange(Lp, dtype=jnp.int32) - FRONT_PAD).astype(F32)
    ang = pos[:, None] * inv[None, :]
    zpad = jnp.zeros((Lp, LANES - MLA_ROPE), F32)
    cosp = jnp.concatenate([jnp.cos(ang), jnp.cos(ang), zpad], axis=-1)
    sinp = jnp.concatenate([jnp.sin(ang), jnp.sin(ang), zpad], axis=-1)

    head_tile = lambda w: pl.BlockSpec((1, MLA_HEADS, T, w), lambda b, i: (b, 0, i, 0))
    q1, k1, v1, zs1 = pl.pallas_call(
        _mla_in_kernel,
        grid=(B, nt),
        in_specs=[tile_spec(D), _const_spec((1, D)), _const_spec((1, D)),
                  _const_spec(w_in1.shape), _const_spec((1, MLA_Q_RANK)),
                  _const_spec((MLA_Q_RANK, MLA_HEADS * MLA_NOPE)),
                  _const_spec((MLA_Q_RANK, MLA_HEADS * LANES)),
                  _const_spec((MLA_Q_RANK, MLA_HEADS * LANES)),
                  _const_spec((D, 3 * LANES)), _const_spec((1, MLA_KV_RANK)),
                  _const_spec((MLA_KV_RANK, MLA_HEADS * MLA_NOPE)), _const_spec((MLA_HEADS * MLA_V, MLA_KV_RANK)),
                  pl.BlockSpec((T, LANES), lambda b, i: (i, 0)),
                  pl.BlockSpec((T, LANES), lambda b, i: (i, 0))],
        out_specs=[pl.BlockSpec((1, MLA_HEADS, T, MLA_QK), lambda b, i: (b, 0, jnp.maximum(i - 1, 0), 0)),
                   head_tile(MLA_QK),
                   pl.BlockSpec((1, MLA_HEADS, 1, vt_rows, T), lambda b, i: (b, 0, i, 0, 0)),
                   pl.BlockSpec((1, T, zw), lambda b, i: (b, jnp.maximum(i - 1, 0), 0))],
        out_shape=[jax.ShapeDtypeStruct((B, MLA_HEADS, S, MLA_QK), BF16),
                   jax.ShapeDtypeStruct((B, MLA_HEADS, Lp, MLA_QK), BF16),
                   jax.ShapeDtypeStruct((B, MLA_HEADS, nt, vt_rows, T), BF16),
                   jax.ShapeDtypeStruct((B, S, zw), BF16)],
        compiler_params=_params("arbitrary", "arbitrary"),
        name="mla_in",
    )(h1, row(pre_norm[1]), row(kv_norm), w_in1.astype(BF16), row(mla_q_latent_norm[0]), w_qn, w_qr, w_qrr,
      w_kd, row(kv_latent_norm), w_ku, w_vt, cosp, sinp)

    o_attn = pl.pallas_call(
        _attn_kernel,
        grid=(B, MLA_HEADS, S // ATTN_TQ),
        in_specs=[pl.BlockSpec((1, 1, ATTN_TQ, MLA_QK), lambda b, h, i: (b, h, i, 0)),
                  pl.BlockSpec((1, 1, Lp, MLA_QK), lambda b, h, i: (b, h, 0, 0)),
                  pl.BlockSpec((1, 1, nt, vt_rows, T), lambda b, h, i: (b, h, 0, 0, 0)),
                  pl.BlockSpec((1, ATTN_TQ, MLA_V), lambda b, h, i: (b, i, h))],
        out_specs=pl.BlockSpec((1, ATTN_TQ, MLA_V), lambda b, h, i: (b, i, h)),
        out_shape=jax.ShapeDtypeStruct((B, S, MLA_HEADS * MLA_V), BF16),
        scratch_shapes=[pltpu.VMEM((ATTN_TQ // ATTN_SUB, 8, ATTN_SUB), F32),
                        pltpu.VMEM((ATTN_TQ // ATTN_SUB, vt_rows, ATTN_SUB), F32),
                        pltpu.VMEM((ATTN_TQ // ATTN_SUB, ATTN_TK, ATTN_SUB), F32),
                        pltpu.VMEM((ATTN_TQ // ATTN_SUB, ATTN_TK, ATTN_SUB), F32),
                        pltpu.VMEM((ATTN_TQ // ATTN_SUB, 8, ATTN_SUB), F32),
                        pltpu.VMEM((ATTN_TQ // ATTN_SUB, 8, ATTN_SUB), F32),
                        pltpu.VMEM((ATTN_TQ // ATTN_SUB, ATTN_TK, ATTN_SUB), BF16)],
        compiler_params=_params("parallel", "parallel", "arbitrary"),
        name="mla_attn",
    )(q1, k1, v1, zs1)

    out = pl.pallas_call(
        _out_proj_last_kernel,
        grid=(B, S // (OUT_TILES * T)),
        in_specs=[pl.BlockSpec((1, OUT_TILES * T, MLA_HEADS * MLA_V), lambda b, i: (b, i, 0)),
                  _const_spec((MLA_HEADS * MLA_V, D)), _const_spec((1, D))]
        + [pl.BlockSpec((1, T, D), lambda b, i, j=j: (b, OUT_TILES * i + j + 1, 0)) for j in range(OUT_TILES)],
        out_specs=pl.BlockSpec((1, OUT_TILES * T, D), lambda b, i: (b, i, 0)),
        out_shape=jax.ShapeDtypeStruct((B, S, D), x.dtype),
        compiler_params=_params("parallel", "arbitrary"),
        name="mla_out",
    )(o_attn, mla_w_out[0].astype(BF16), row(post_norm[1]), *([h1] * OUT_TILES))
    return out
```

```python
import functools
import math

import jax
import jax.numpy as jnp
from jax import lax
from jax.experimental import pallas as pl
from jax.experimental.pallas import tpu as pltpu

NORM_EPS = 1e-6
N_META_ROWS = 16

GDN_QK_HEADS = 8
GDN_V_HEADS = 16
GDN_HEAD = 128
GDN_CONV_TAPS = 4
GDN_CHUNK = 64
GDN_QK_W = GDN_QK_HEADS * GDN_HEAD
GDN_V_W = GDN_V_HEADS * GDN_HEAD
GDN_CONV_W = 2 * GDN_QK_W + GDN_V_W

MLA_HEADS = 16
MLA_NOPE = 128
MLA_ROPE = 64
MLA_V = 128
MLA_Q_RANK = 256
MLA_KV_RANK = 128
MLA_QK = MLA_NOPE + MLA_ROPE
ROPE_THETA = 10000.0

LANES = 128
ROW_TILE = 256
FRONT_PAD = ROW_TILE - N_META_ROWS
WIDE_TILES = 3
OUT_TILES = 4
CONV_COLS = 512
HALO_ROWS = 8
ATTN_TQ = 2048
ATTN_SUB = 256
ATTN_TK = 1024
ATTN_GROUP = 128
ATTN_ONES_ROWS = 16
VMEM_LIMIT = 60 * 1024 * 1024

F32 = jnp.float32
BF16 = jnp.bfloat16


def _dot(a, b):
    return jnp.dot(a, b, preferred_element_type=F32)


def _dot_nt(a, b):
    return lax.dot_general(a, b, (((1,), (1,)), ((), ())), preferred_element_type=F32)


def _dot_tn(a, b):
    return lax.dot_general(a, b, (((0,), (0,)), ((), ())), preferred_element_type=F32)


def _dot_exact(a, b):
    return jnp.dot(a, b, preferred_element_type=F32, precision=lax.Precision.HIGHEST)


def _silu(x):
    return x * jax.nn.sigmoid(x)


def _softplus(x):
    return jnp.maximum(x, 0.0) + jnp.log1p(jnp.exp(-jnp.abs(x)))


def _rms_scale(x):
    return lax.rsqrt(jnp.mean(x * x, axis=-1, keepdims=True) + NORM_EPS)


def _const_spec(shape):
    nd = len(shape)
    return pl.BlockSpec(shape, lambda *_: (0,) * nd, pipeline_mode=pl.Buffered(1))


def _gdn_in_kernel(head_ref, gain_ref, win_ref, wba_ref, wbat_ref, convw_ref,
                   arow_ref, dtrow_ref, acol_ref, dtcol_ref, *refs):
    x_refs = refs[:WIDE_TILES]
    outs = refs[WIDE_TILES:WIDE_TILES + 7]
    halo_scr, buf_scr = refs[WIDE_TILES + 7:]
    i = pl.program_id(1)

    @pl.when(i == 0)
    def _():
        halo_scr[...] = jnp.zeros_like(halo_scr)

    ri = lax.broadcasted_iota(jnp.int32, (ROW_TILE, ROW_TILE), 0)
    ci = lax.broadcasted_iota(jnp.int32, (ROW_TILE, ROW_TILE), 1)
    same = (ri // GDN_CHUNK) == (ci // GDN_CHUNK)
    prefix = (jnp.where(same & (ri >= ci), 1.0, 0.0).astype(F32),
              jnp.where(same & (ri <= ci), 1.0, 0.0).astype(F32),
              jnp.where(same, 1.0, 0.0).astype(F32))
    for t, x_ref in enumerate(x_refs):
        x = x_ref[0] if t else jnp.where(i == 0, head_ref[...], x_ref[0])
        _gdn_in_tile(t, x, gain_ref, win_ref, wba_ref, wbat_ref, convw_ref,
                     arow_ref, dtrow_ref, acol_ref, dtcol_ref, prefix, outs, halo_scr, buf_scr)


def _gdn_in_tile(t, x, gain_ref, win_ref, wba_ref, wbat_ref, convw_ref,
                 arow_ref, dtrow_ref, acol_ref, dtcol_ref, prefix, outs, halo_scr, buf_scr):
    q_ref, k_ref, v_ref, zs_ref, gcol_ref, grow_ref, egl_ref = outs
    lower, upper, block = prefix
    rt = slice(t * ROW_TILE, (t + 1) * ROW_TILE)
    c0 = t * (ROW_TILE // GDN_CHUNK)
    hn = (x * _rms_scale(x) * gain_ref[...]).astype(BF16)

    lo = HALO_ROWS - (GDN_CONV_TAPS - 1)
    for c in range(GDN_CONV_W // CONV_COLS):
        cs = slice(c * CONV_COLS, (c + 1) * CONV_COLS)
        p = _dot(hn, win_ref[:, cs])
        for hh in range(CONV_COLS // GDN_HEAD):
            col = c * CONV_COLS + hh * GDN_HEAD
            buf_scr[hh, 0:HALO_ROWS, :] = halo_scr[:, col:col + GDN_HEAD]
            buf_scr[hh, HALO_ROWS:HALO_ROWS + ROW_TILE, :] = p[:, hh * GDN_HEAD:(hh + 1) * GDN_HEAD]
            halo_scr[:, col:col + GDN_HEAD] = p[ROW_TILE - HALO_ROWS:, hh * GDN_HEAD:(hh + 1) * GDN_HEAD]
            yh = convw_ref[0:1, col:col + GDN_HEAD] * buf_scr[hh, lo:lo + ROW_TILE, :]
            for j in range(1, GDN_CONV_TAPS):
                yh = yh + convw_ref[j:j + 1, col:col + GDN_HEAD] * buf_scr[hh, lo + j:lo + j + ROW_TILE, :]
            yh = _silu(yh)
            if col < 2 * GDN_QK_W:
                r = lax.rsqrt(jnp.sum(yh * yh, axis=-1, keepdims=True) + NORM_EPS)
                if col < GDN_QK_W:
                    q_ref[0, rt, col:col + GDN_HEAD] = (yh * (r * (GDN_HEAD ** -0.5))).astype(BF16)
                else:
                    k_ref[0, rt, col - GDN_QK_W:col - GDN_QK_W + GDN_HEAD] = (yh * r).astype(BF16)
            else:
                v_ref[0, rt, col - 2 * GDN_QK_W:col - 2 * GDN_QK_W + GDN_HEAD] = yh.astype(BF16)

    for c in range(GDN_V_W // CONV_COLS):
        cs = slice(c * CONV_COLS, (c + 1) * CONV_COLS)
        ws = slice(GDN_CONV_W + c * CONV_COLS, GDN_CONV_W + (c + 1) * CONV_COLS)
        zs_ref[0, rt, cs] = _silu(_dot(hn, win_ref[:, ws])).astype(BF16)

    nh = GDN_V_HEADS
    ba = _dot(hn, wba_ref[...])
    bat = _dot_nt(wbat_ref[...], hn)
    beta_c = 1.0 / (1.0 + jnp.exp(-ba))
    g_c = -jnp.exp(arow_ref[...]) * _softplus(ba + dtrow_ref[...])
    beta_r = 1.0 / (1.0 + jnp.exp(-bat[0:nh, :]))
    g_r = -jnp.exp(acol_ref[...]) * _softplus(bat[nh:2 * nh, :] + dtcol_ref[...])

    gc_c = _dot_exact(lower, g_c)
    gl_c = _dot_exact(block, g_c)
    gc_r = _dot_exact(g_r, upper)
    grp = lax.broadcasted_iota(jnp.int32, (ROW_TILE, LANES), 1) // nh
    gcol_ref[0, rt, :] = jnp.where(grp == 0, beta_c,
                                   jnp.where(grp == 1, gc_c,
                                             jnp.where(grp == 2, jnp.exp(gc_c),
                                                       jnp.where(grp == 3, jnp.exp(gl_c - gc_c), 0.0))))
    npair = nh // 2
    for kind, xr in enumerate((gc_r, beta_r, beta_r * jnp.exp(gc_r))):
        for c in range(ROW_TILE // GDN_CHUNK):
            cs = slice(c * GDN_CHUNK, (c + 1) * GDN_CHUNK)
            grow_ref[0, c0 + c, kind * npair:(kind + 1) * npair, :] = jnp.concatenate(
                [xr[0:npair, cs], xr[npair:nh, cs]], axis=-1)
    for c in range(ROW_TILE // GDN_CHUNK):
        cs = slice(c * GDN_CHUNK, (c + 1) * GDN_CHUNK)
        last = (c + 1) * GDN_CHUNK - 1
        gl = jnp.broadcast_to(gc_r[:, last:last + 1], (nh, LANES))
        egl_ref[0, c0 + c] = jnp.exp(gl)
        to_end = jnp.exp(gl[:, 0:GDN_CHUNK] - gc_r[:, cs])
        grow_ref[0, c0 + c, 3 * npair:4 * npair, :] =jnp.concatenate([to_end[0:npair], to_end[npair:nh]], axis=-1)


def _gdn_prep_kernel(q_ref, k_ref, gcol_ref, grow_ref, tb_ref, w_ref, a_ref, kdt_ref):
    C = GDN_CHUNK
    nh = GDN_V_HEADS
    npair = nh // 2
    row = lax.broadcasted_iota(jnp.int32, (C, LANES), 0)
    lane = lax.broadcasted_iota(jnp.int32, (C, LANES), 1)
    left = lane < C
    col = jnp.where(left, lane, lane - C)
    incl = row >= col
    strict = row > col
    eye = jnp.where(row == col, 1.0, 0.0).astype(F32)
    diag8 = strict & ((row // 8) == (col // 8))
    merges = tuple(strict & ((row // (2 * s)) == (col // (2 * s))) & ((row // s) != (col // s))
                   for s in (8, 16, 32))
    zero_b = jnp.zeros((C, LANES), BF16)

    def blockdiag(x):
        xb = x.astype(BF16)
        return jnp.concatenate([jnp.where(left, xb, zero_b), jnp.where(left, zero_b, xb)], axis=0)

    def pair_cols(g, base, p_):
        return jnp.take_along_axis(g, jnp.where(left, base + 2 * p_, base + 2 * p_ + 1), axis=1)

    cpt = ROW_TILE // C
    streams = [(c, p_) for c in range(cpt) for p_ in range(npair)]
    zero_k = jnp.zeros((C, GDN_HEAD), BF16)

    def row_tile(j, carry):
        rows = lambda c: pl.ds(pl.multiple_of(j * ROW_TILE + c * C, C), C)
        ms, kps, bege, betar = [], [], [], []
        for c, p_ in streams:
            rs, cj = rows(c), j * cpt + c
            ps = slice(p_ * GDN_HEAD, (p_ + 1) * GDN_HEAD)
            qp = q_ref[0, rs, ps]
            kp = k_ref[0, rs, ps]
            g = gcol_ref[0, rs, :]
            both = _dot_nt(jnp.concatenate([qp, kp], axis=0), jnp.concatenate([kp, kp], axis=0))
            qk2, kk2 = both[0:C], both[C:2 * C]
            decay = jnp.exp(jnp.where(incl, pair_cols(g, nh, p_) - grow_ref[0, cj, p_:p_ + 1, :], -jnp.inf))
            ms.append(jnp.where(strict, pair_cols(g, 0, p_) * kk2 * decay, 0.0))
            a_ref[0, rs, ps] = (qk2 * decay).astype(BF16)
            kpf = kp.astype(F32)
            kdt_ref[0, cj, p_] = (jnp.concatenate([kpf, kpf], axis=0).T
                                  * grow_ref[0, cj, 3 * npair + p_:3 * npair + p_ + 1, :]).astype(BF16)
            betar.append(grow_ref[0, cj, npair + p_:npair + p_ + 1, :])
            kps.append(kp)
            bege.append(grow_ref[0, cj, 2 * npair + p_:2 * npair + p_ + 1, :])

        m8 = [jnp.where(diag8, m, 0.0) for m in ms]
        q2 = [_dot(x.astype(BF16), blockdiag(x)) for x in m8]
        pinv = [eye - x for x in m8]
        pinv = [p + _dot(p.astype(BF16), blockdiag(y)) for p, y in zip(pinv, q2)]
        q4 = [_dot(y.astype(BF16), blockdiag(y)) for y in q2]
        pinv = [p + _dot(p.astype(BF16), blockdiag(y)) for p, y in zip(pinv, q4)]
        for mask in merges:
            cp = [_dot(jnp.where(mask, m, 0.0).astype(BF16), blockdiag(p)) for m, p in zip(ms, pinv)]
            pinv = [p - _dot(p.astype(BF16), blockdiag(y)) for p, y in zip(pinv, cp)]

        for (c, p_), p, kp, bg, br in zip(streams, pinv, kps, bege, betar):
            rs = rows(c)
            ps = slice(p_ * GDN_HEAD, (p_ + 1) * GDN_HEAD)
            kbd = jnp.concatenate([jnp.concatenate([kp, zero_k], axis=1),
                                   jnp.concatenate([zero_k, kp], axis=1)], axis=0)
            w_ref[0, rs, 2 * p_ * GDN_HEAD:(2 * p_ + 2) * GDN_HEAD] = _dot((p * bg).astype(BF16), kbd).astype(BF16)
            tb_ref[0, rs, ps] = (p * br).astype(BF16)
        return carry

    lax.fori_loop(0, q_ref.shape[1] // ROW_TILE, row_tile, 0)


def _gdn_scan_kernel(tb_ref, w_ref, a_ref, kdt_ref, q_ref, v_ref, zs_ref, gcol_ref, egl_ref, onorm_ref, o_ref,
                     state_scr):
    @pl.when(pl.program_id(0) == 0)
    def _():
        state_scr[...] = jnp.zeros_like(state_scr)

    C = GDN_CHUNK
    nh = GDN_V_HEADS
    nb = q_ref.shape[0]
    onorm = onorm_ref[...]
    zero_v = jnp.zeros((C, GDN_HEAD), BF16)
    ones_sq = jnp.ones((GDN_HEAD, GDN_HEAD), BF16)

    def blockdiag(x0, x1):
        return jnp.concatenate([jnp.concatenate([x0, zero_v], axis=1),
                                jnp.concatenate([zero_v, x1], axis=1)], axis=0)

    pairs = [(b, p_) for b in range(nb) for p_ in range(nh // 2)]
    heads = [(b, h) for b, p_ in pairs for h in (2 * p_, 2 * p_ + 1)]
    hsl = lambda h: slice(h * GDN_HEAD, (h + 1) * GDN_HEAD)

    def chunk(c, carry):
        rs = pl.ds(pl.multiple_of(c * C, C), C)
        u2 = [_dot(tb_ref[b, rs, hsl(p_)], blockdiag(v_ref[b, rs, hsl(2 * p_)], v_ref[b, rs, hsl(2 * p_ + 1)]))
              for b, p_ in pairs]
        u = {(b, 2 * p_ + j): x[:, j * GDN_HEAD:(j + 1) * GDN_HEAD]
             for (b, p_), x in zip(pairs, u2) for j in (0, 1)}
        s_old = {bh: state_scr[bh[0], bh[1]] for bh in heads}
        wq = {(b, h): _dot(jnp.concatenate([w_ref[b, rs, hsl(h)], q_ref[b, rs, hsl(h // 2)]], axis=0),
                           s_old[(b, h)].astype(BF16)) for b, h in heads}
        vnb = {bh: (u[bh] - wq[bh][0:C]).astype(BF16) for bh in heads}
        od = [_dot(jnp.concatenate([a_ref[b, rs, hsl(p_)], kdt_ref[b, c, p_]], axis=0),
                   blockdiag(vnb[(b, 2 * p_)], vnb[(b, 2 * p_ + 1)])) for b, p_ in pairs]
        os = {}
        for (b, p_), x in zip(pairs, od):
            for j in (0, 1):
                h = 2 * p_ + j
                js = slice(j * GDN_HEAD, (j + 1) * GDN_HEAD)
                erow = (h % 2) * (nh // 2) + h // 2
                state_scr[b, h] = s_old[(b, h)] * egl_ref[b, c, erow:erow + 1, :] + x[C:, js]
                os[(b, h)] = gcol_ref[b, rs, 2 * nh + h:2 * nh + h + 1] * wq[(b, h)][C:2 * C] + x[0:C, js]
        sq = {bh: _dot((o * o).astype(BF16), ones_sq) for bh, o in os.items()}
        for (b, h), o in os.items():
            on = o * lax.rsqrt(sq[(b, h)] * (1.0 / GDN_HEAD) + NORM_EPS) * onorm
            o_ref[b, rs, hsl(h)] = (on * zs_ref[b, rs, hsl(h)].astype(F32)).astype(BF16)
        return carry

    lax.fori_loop(0, q_ref.shape[1] // C, chunk, 0, unroll=2)


def _out_proj_first_kernel(o_ref, w_ref, gain_ref, head_ref, *refs):
    x_refs, h_ref = refs[:-1], refs[-1]
    y = _dot(o_ref[0], w_ref[...])
    yn = y * _rms_scale(y) * gain_ref[...]
    for j, x_ref in enumerate(x_refs):
        res = x_ref[0] if j else jnp.where(pl.program_id(1) == 0, head_ref[...], x_ref[0])
        h_ref[0, j * ROW_TILE:(j + 1) * ROW_TILE, :] = res + yn[j * ROW_TILE:(j + 1) * ROW_TILE]


def _out_proj_last_kernel(o_ref, w_ref, gain_ref, *refs):
    h_refs, out_ref = refs[:-1], refs[-1]
    y = _dot(o_ref[0], w_ref[...])
    yn = y * _rms_scale(y) * gain_ref[...]
    for j, h_ref in enumerate(h_refs):
        out_ref[0, j * ROW_TILE:(j + 1) * ROW_TILE, :] = h_ref[0] + yn[j * ROW_TILE:(j + 1) * ROW_TILE]


def _mla_in_kernel(h_ref, pre_ref, kvn_ref, win_ref, qln_ref, wqn_ref, wqr_ref, wqrr_ref,
                   wkd_ref, kvln_ref, wku_ref, wvt_ref, cos_ref, sin_ref,
                   q_ref, kk_ref, vt_ref, zs_ref):
    h = h_ref[0]
    hr = h * _rms_scale(h)
    hn = (hr * pre_ref[...]).astype(BF16)
    hk = (hr * kvn_ref[...]).astype(BF16)
    cosp = cos_ref[...]
    sinp = sin_ref[...]

    for c in range((win_ref.shape[1] - MLA_Q_RANK) // CONV_COLS):
        cs = slice(c * CONV_COLS, (c + 1) * CONV_COLS)
        ws = slice(MLA_Q_RANK + c * CONV_COLS, MLA_Q_RANK + (c + 1) * CONV_COLS)
        zs_ref[0, :, cs] = _silu(_dot(hn, win_ref[:, ws])).astype(BF16)

    cq = _dot(hn, win_ref[:, 0:MLA_Q_RANK])
    cq = (cq * _rms_scale(cq) * qln_ref[...]).astype(BF16)
    scale = MLA_QK ** -0.5 * math.log2(math.e)
    for hd in range(MLA_HEADS):
        hs = slice(hd * LANES, (hd + 1) * LANES)
        q_ref[0, hd, :, 0:MLA_NOPE] = (_dot(cq, wqn_ref[:, hs]) * scale).astype(BF16)
    per_tile = LANES // MLA_ROPE
    for t in range(MLA_HEADS // per_tile):
        ts = slice(t * LANES, (t + 1) * LANES)
        qr = ((_dot(cq, wqr_ref[:, ts]) * cosp + _dot(cq, wqrr_ref[:, ts]) * sinp) * scale).astype(BF16)
        for u in range(per_tile):
            q_ref[0, t * per_tile + u, :, MLA_NOPE:MLA_QK] = qr[:, u * MLA_ROPE:(u + 1) * MLA_ROPE]

    ckr = _dot(hk, wkd_ref[...])
    lat = ckr[:, 0:MLA_KV_RANK]
    ckv = (lat * _rms_scale(lat) * kvln_ref[...]).astype(BF16)
    kr = ckr[:, LANES:2 * LANES] * cosp + ckr[:, 2 * LANES:3 * LANES] * sinp
    krb = kr[:, 0:MLA_ROPE].astype(BF16)
    ones_rows = jnp.ones((ATTN_ONES_ROWS, ROW_TILE), BF16)
    for hd in range(MLA_HEADS):
        kn = _dot(ckv, wku_ref[:, hd * MLA_NOPE:(hd + 1) * MLA_NOPE])
        kk_ref[0, hd, :, 0:MLA_NOPE] = kn.astype(BF16)
        kk_ref[0, hd, :, MLA_NOPE:MLA_QK] = krb
        vt = _dot_nt(wvt_ref[hd * MLA_V:(hd + 1) * MLA_V, :], ckv)
        vt_ref[0, hd, 0, 0:MLA_V, :] = vt.astype(BF16)
        vt_ref[0, hd, 0, MLA_V:MLA_V + ATTN_ONES_ROWS, :] = ones_rows


def _attn_kernel(q_ref, k_ref, vt_ref, zs_ref, o_ref, m_scr, al_scr, sa_scr, sb_scr, ma_scr, mb_scr, p_scr):
    i = pl.program_id(2)
    tq, sb, tk = ATTN_TQ, ATTN_SUB, ATTN_TK
    m_scr[...] = jnp.full_like(m_scr, -jnp.inf)
    al_scr[...] = jnp.zeros_like(al_scr)

    def rows(r):
        return slice(r * sb, (r + 1) * sb)

    def scores(r, k0, nk):
        return _dot_nt(k_ref[0, 0, pl.ds(k0, nk), :], q_ref[0, 0, rows(r), :])

    def weighted_values(p, k0, nk):
        if nk < ROW_TILE:
            return _dot(vt_ref[0, 0, 0, :, k0:k0 + nk], p[...])
        kb = k0 // ROW_TILE
        out = _dot(vt_ref[0, 0, kb], p[0:ROW_TILE, :])
        for jb in range(1, nk // ROW_TILE):
            out = out + _dot(vt_ref[0, 0, kb + jb], p[jb * ROW_TILE:(jb + 1) * ROW_TILE, :])
        return out

    def absorb(r, parts):
        m_old = m_scr[r]
        m_new = m_old
        for s, _, _ in parts:
            m_new = jnp.maximum(m_new, jnp.max(s, axis=0, keepdims=True))
        al = jnp.exp2(m_old - m_new)[0:1, :] * al_scr[r]
        for s, k0, nk in parts:
            al = al + weighted_values(jnp.exp2(s - m_new[0:1, :]).astype(BF16), k0, nk)
        al_scr[r] = al
        m_scr[r] = m_new

    def stage_block(r, bufs, k0):
        s_buf, mx_buf = bufs
        s = scores(r, k0, tk)
        s_buf[r] = s
        mx_buf[r] = jnp.broadcast_to(jnp.max(s, axis=0, keepdims=True), mx_buf.shape[1:])

    def absorb_block(r, bufs, k0):
        s_buf, mx_buf = bufs
        m_old = m_scr[r]
        m_new = jnp.maximum(m_old, mx_buf[r])
        for g in range(tk // ATTN_GROUP):
            gr = slice(g * ATTN_GROUP, (g + 1) * ATTN_GROUP)
            p_scr[r, gr, :] = jnp.exp2(s_buf[r, gr, :] - m_new[0:1, :]).astype(BF16)
        al_scr[r] = jnp.exp2(m_old - m_new)[0:1, :] * al_scr[r] + weighted_values(p_scr.at[r], k0, tk)
        m_scr[r] = m_new

    nsub = tq // sb
    block0 = lambda j: pl.multiple_of(ROW_TILE + j * tk, ROW_TILE)

    ki = lax.broadcasted_iota(jnp.int32, (sb, sb), 0)
    qi = lax.broadcasted_iota(jnp.int32, (sb, sb), 1)
    base = pl.multiple_of(ROW_TILE + i * tq, ROW_TILE)
    meta0 = ROW_TILE - LANES
    is_meta = lax.broadcasted_iota(jnp.int32, (LANES, sb), 0) >= FRONT_PAD - meta0

    def mask_diag(s):
        tri = jnp.where(ki <= qi, s[-sb:, :], -jnp.inf)
        return tri if s.shape[0] == sb else jnp.concatenate([s[:-sb, :], tri], axis=0)

    base2 = pl.multiple_of(base + tk, ROW_TILE)

    def diag_scores_a(r):
        w = (r + 1) * sb
        if w <= tk:
            sa_scr[r, 0:w, :] = scores(r, base, w)
        else:
            sa_scr[r, 0:w - tk, :] = scores(r, base2, w - tk)

    def diag_scores_b(r):
        if (r + 1) * sb > tk:
            sb_scr[r] = scores(r, base, tk)

    assert tq == 2 * tk
    trips = i

    @pl.when(trips == 0)
    def _():
        for r in range(nsub):
            diag_scores_a(r)
            diag_scores_b(r)

    @pl.when(trips > 0)
    def _():
        buf_a, buf_b = (sa_scr, ma_scr), (sb_scr, mb_scr)
        for r in range(nsub):
            stage_block(r, buf_a, block0(0))

        def body(jj, carry):
            j = 2 * jj
            for cur, nxt, step in ((buf_a, buf_b, 0), (buf_b, buf_a, 1)):
                for r in range(nsub):
                    stage_block(r, nxt, block0(j + step + 1))
                    absorb_block(r, cur, block0(j + step))
            return carry

        lax.fori_loop(0, trips - 1, body, 0)
        j = 2 * (trips - 1)
        for r in range(nsub):
            stage_block(r, buf_b, block0(j + 1))
            absorb_block(r, buf_a, block0(j))
        for r in range(nsub):
            diag_scores_a(r)
            absorb_block(r, buf_b, block0(j + 1))
            diag_scores_b(r)

    s_meta = [jnp.where(is_meta, scores(r, meta0, LANES), -jnp.inf) for r in range(nsub)]
    for r in range(nsub):
        w = (r + 1) * sb
        if w <= tk:
            absorb(r, [(s_meta[r], meta0, LANES), (mask_diag(sa_scr[r, 0:w, :]), base, w)])
        else:
            absorb(r, [(s_meta[r], meta0, LANES), (sb_scr[r], base, tk),
                       (mask_diag(sa_scr[r, 0:w - tk, :]), base2, w - tk)])

    for r in range(nsub):
        o_t = al_scr[r, 0:MLA_V, :] / al_scr[r, MLA_V:MLA_V + 1, :]
        o_ref[0, rows(r), :] = (o_t.T * zs_ref[0, rows(r), :].astype(F32)).astype(BF16)


def _params(*sem):
    return pltpu.CompilerParams(dimension_semantics=sem, vmem_limit_bytes=VMEM_LIMIT)


def kernel(x, meta_tokens, pre_norm, post_norm, gdn_w_in, gdn_conv_w, gdn_a_log, gdn_dt_bias, gdn_out_norm, gdn_w_out, kv_norm, kv_w_down, kv_latent_norm, kv_w_up, mla_w_in, mla_q_latent_norm, mla_w_q_up, mla_w_out):
    B, S, D = x.shape
    assert S % ATTN_TQ == 0 and S % (OUT_TILES * ROW_TILE) == 0 and (S // ROW_TILE + 1) % WIDE_TILES == 0
    assert gdn_w_in.shape[0] == 1 and mla_w_in.shape[0] == 1
    T = ROW_TILE
    nt = S // T + 1
    Lp = nt * T
    nch = Lp // GDN_CHUNK
    nh = GDN_V_HEADS

    head = jnp.concatenate([jnp.zeros((FRONT_PAD, D), F32), meta_tokens.astype(F32)], axis=0)
    row = lambda a: a.reshape(1, -1).astype(F32)

    w_in = gdn_w_in[0]
    w_ba_f = w_in[:, GDN_CONV_W + GDN_V_W:]
    w_b, w_a = w_ba_f[:, :nh], w_ba_f[:, nh:]
    w_ba = jnp.concatenate([w_b, w_a, w_a, w_a, jnp.zeros((D, LANES - 4 * nh), F32)], axis=-1).astype(BF16)
    eo = lambda a: jnp.concatenate([a[0::2], a[1::2]], axis=0)
    w_bat = jnp.concatenate([eo(w_b.T), eo(w_a.T)], axis=0).astype(BF16)
    a_log = gdn_a_log[0].astype(F32)
    dt_b = gdn_dt_bias[0].astype(F32)
    lane_groups = lambda a: jnp.concatenate([jnp.zeros((nh,), F32), a, a, a,
                                             jnp.zeros((LANES - 4 * nh,), F32)]).reshape(1, LANES)

    tile_spec = lambda w: pl.BlockSpec((1, T, w), lambda b, i: (b, i, 0))
    wide_spec = lambda w: pl.BlockSpec((1, WIDE_TILES * T, w), lambda b, i: (b, i, 0))

    x_specs = [pl.BlockSpec((1, T, D), lambda b, i, j=j: (b, jnp.maximum(WIDE_TILES * i + j - 1, 0), 0))
               for j in range(WIDE_TILES)]
    gate_spec = lambda r: pl.BlockSpec((1, WIDE_TILES * T // GDN_CHUNK, r, LANES), lambda b, i: (b, i, 0, 0))
    q, k, v, zs, gcol, grow, egl = pl.pallas_call(
        _gdn_in_kernel,
        grid=(B, nt // WIDE_TILES),
        in_specs=[_const_spec((T, D)), _const_spec((1, D)),
                  _const_spec(w_in.shape), _const_spec((D, LANES)), _const_spec((2 * nh, D)),
                  _const_spec((GDN_CONV_TAPS, GDN_CONV_W)),
                  _const_spec((1, LANES)), _const_spec((1, LANES)), _const_spec((nh, 1)), _const_spec((nh, 1))]
        + x_specs,
        out_specs=[wide_spec(GDN_QK_W), wide_spec(GDN_QK_W), wide_spec(GDN_V_W), wide_spec(GDN_V_W),
                   wide_spec(LANES), gate_spec(2 * nh), gate_spec(nh)],
        out_shape=[jax.ShapeDtypeStruct((B, Lp, GDN_QK_W), BF16),
                   jax.ShapeDtypeStruct((B, Lp, GDN_QK_W), BF16),
                   jax.ShapeDtypeStruct((B, Lp, GDN_V_W), BF16),
                   jax.ShapeDtypeStruct((B, Lp, GDN_V_W), BF16),
                   jax.ShapeDtypeStruct((B, Lp, LANES), F32),
                   jax.ShapeDtypeStruct((B, nch, 2 * nh, LANES), F32),
                   jax.ShapeDtypeStruct((B, nch, nh, LANES), F32)],
        scratch_shapes=[pltpu.VMEM((HALO_ROWS, GDN_CONV_W), F32),
                        pltpu.VMEM((CONV_COLS // GDN_HEAD, HALO_ROWS + T, GDN_HEAD), F32)],
        compiler_params=_params("arbitrary", "arbitrary"),
        name="gdn_in",
    )(head, row(pre_norm[0]), w_in.astype(BF16), w_ba, w_bat, gdn_conv_w[0].astype(F32),
      lane_groups(a_log), lane_groups(dt_b), eo(a_log).reshape(nh, 1), eo(dt_b).reshape(nh, 1),
      *([x] * WIDE_TILES))

    npair = nh // 2
    tb, wk, aqk, kdt = pl.pallas_call(
        _gdn_prep_kernel,
        grid=(B, nt // WIDE_TILES),
        in_specs=[wide_spec(GDN_QK_W), wide_spec(GDN_QK_W), wide_spec(LANES),
                  pl.BlockSpec((1, WIDE_TILES * T // GDN_CHUNK, 2 * nh, LANES), lambda b, i: (b, i, 0, 0))],
        out_specs=[wide_spec(GDN_QK_W), wide_spec(GDN_V_W), wide_spec(GDN_QK_W),
                   pl.BlockSpec((1, WIDE_TILES * T // GDN_CHUNK, npair, GDN_HEAD, LANES),
                                lambda b, i: (b, i, 0, 0, 0))],
        out_shape=[jax.ShapeDtypeStruct((B, Lp, GDN_QK_W), BF16),
                   jax.ShapeDtypeStruct((B, Lp, GDN_V_W), BF16),
                   jax.ShapeDtypeStruct((B, Lp, GDN_QK_W), BF16),
                   jax.ShapeDtypeStruct((B, nch, npair, GDN_HEAD, LANES), BF16)],
        compiler_params=_params("parallel", "parallel"),
        name="gdn_prep",
    )(q, k, gcol, grow)

    cpt = T // GDN_CHUNK
    chunk_spec = lambda w: pl.BlockSpec((B, T, w), lambda n: (0, n, 0))
    o_gdn = pl.pallas_call(
        _gdn_scan_kernel,
        grid=(nt,),
        in_specs=[chunk_spec(GDN_QK_W), chunk_spec(GDN_V_W), chunk_spec(GDN_QK_W),
                  pl.BlockSpec((B, cpt, npair, GDN_HEAD, LANES), lambda n: (0, n, 0, 0, 0)),
                  chunk_spec(GDN_QK_W), chunk_spec(GDN_V_W), chunk_spec(GDN_V_W), chunk_spec(LANES),
                  pl.BlockSpec((B, cpt, nh, LANES), lambda n: (0, n, 0, 0)),
                  _const_spec((1, GDN_HEAD))],
        out_specs=chunk_spec(GDN_V_W),
        out_shape=jax.ShapeDtypeStruct((B, Lp, GDN_V_W), BF16),
        scratch_shapes=[pltpu.VMEM((B, nh, GDN_HEAD, GDN_HEAD), F32)],
        compiler_params=_params("arbitrary"),
        name="gdn_scan",
    )(tb, wk, aqk, kdt, q, v, zs, gcol, egl, row(gdn_out_norm[0]))

    h1 = pl.pallas_call(
        _out_proj_first_kernel,
        grid=(B, nt // WIDE_TILES),
        in_specs=[wide_spec(GDN_V_W), _const_spec((GDN_V_W, D)), _const_spec((1, D)), _const_spec((T, D))]
        + x_specs,
        out_specs=wide_spec(D),
        out_shape=jax.ShapeDtypeStruct((B, Lp, D), F32),
        compiler_params=_params("parallel", "arbitrary"),
        name="gdn_out",
    )(o_gdn, gdn_w_out[0].astype(BF16), row(post_norm[0]), head, *([x] * WIDE_TILES))

    w_in1 = mla_w_in[0]
    zw = w_in1.shape[1] - MLA_Q_RANK
    half = MLA_ROPE // 2
    rot = lambda w: jnp.concatenate([-w[..., half:], w[..., :half]], axis=-1)
    lane_pad = lambda w: jnp.pad(w, [(0, 0)] * (w.ndim - 1) + [(0, LANES - w.shape[-1])])
    wq = mla_w_q_up[0].reshape(MLA_Q_RANK, MLA_HEADS, MLA_QK)
    w_qn = wq[..., :MLA_NOPE].reshape(MLA_Q_RANK, MLA_HEADS * MLA_NOPE).astype(BF16)
    w_qr = wq[..., MLA_NOPE:].reshape(MLA_Q_RANK, MLA_HEADS * MLA_ROPE).astype(BF16)
    w_qrr = rot(wq[..., MLA_NOPE:]).reshape(MLA_Q_RANK, MLA_HEADS * MLA_ROPE).astype(BF16)
    wkd_r = kv_w_down[:, MLA_KV_RANK:]
    w_kd = jnp.concatenate([kv_w_down[:, :MLA_KV_RANK], lane_pad(wkd_r), lane_pad(rot(wkd_r))],
                           axis=-1).astype(BF16)
    wku = kv_w_up.reshape(MLA_KV_RANK, MLA_HEADS, MLA_NOPE + MLA_V)
    w_ku = wku[..., :MLA_NOPE].reshape(MLA_KV_RANK, MLA_HEADS * MLA_NOPE).astype(BF16)
    w_vt = jnp.transpose(wku[..., MLA_NOPE:], (1, 2, 0)).reshape(MLA_HEADS * MLA_V, MLA_KV_RANK).astype(BF16)
    vt_rows = MLA_V + ATTN_ONES_ROWS

    inv = ROPE_THETA ** (-jnp.arange(0, MLA_ROPE, 2, dtype=F32) / MLA_ROPE)
    pos = (jnp.arange(Lp, dtype=jnp.int32) - FRONT_PAD).astype(F32)
    ang = pos[:, None] * inv[None, :]
    cosp = jnp.tile(jnp.cos(ang), (1, LANES // half))
    sinp = jnp.tile(jnp.sin(ang), (1, LANES // half))

    head_tile = lambda w: pl.BlockSpec((1, MLA_HEADS, T, w), lambda b, i: (b, 0, i, 0))
    q1, k1, v1, zs1 = pl.pallas_call(
        _mla_in_kernel,
        grid=(B, nt),
        in_specs=[tile_spec(D), _const_spec((1, D)), _const_spec((1, D)),
                  _const_spec(w_in1.shape), _const_spec((1, MLA_Q_RANK)),
                  _const_spec((MLA_Q_RANK, MLA_HEADS * MLA_NOPE)),
                  _const_spec((MLA_Q_RANK, MLA_HEADS * MLA_ROPE)),
                  _const_spec((MLA_Q_RANK, MLA_HEADS * MLA_ROPE)),
                  _const_spec((D, 3 * LANES)), _const_spec((1, MLA_KV_RANK)),
                  _const_spec((MLA_KV_RANK, MLA_HEADS * MLA_NOPE)), _const_spec((MLA_HEADS * MLA_V, MLA_KV_RANK)),
                  pl.BlockSpec((T, LANES), lambda b, i: (i, 0)),
                  pl.BlockSpec((T, LANES), lambda b, i: (i, 0))],
        out_specs=[pl.BlockSpec((1, MLA_HEADS, T, MLA_QK), lambda b, i: (b, 0, jnp.maximum(i - 1, 0), 0)),
                   head_tile(MLA_QK),
                   pl.BlockSpec((1, MLA_HEADS, 1, vt_rows, T), lambda b, i: (b, 0, i, 0, 0)),
                   pl.BlockSpec((1, T, zw), lambda b, i: (b, jnp.maximum(i - 1, 0), 0))],
        out_shape=[jax.ShapeDtypeStruct((B, MLA_HEADS, S, MLA_QK), BF16),
                   jax.ShapeDtypeStruct((B, MLA_HEADS, Lp, MLA_QK), BF16),
                   jax.ShapeDtypeStruct((B, MLA_HEADS, nt, vt_rows, T), BF16),
                   jax.ShapeDtypeStruct((B, S, zw), BF16)],
        compiler_params=_params("arbitrary", "arbitrary"),
        name="mla_in",
    )(h1, row(pre_norm[1]), row(kv_norm), w_in1.astype(BF16), row(mla_q_latent_norm[0]), w_qn, w_qr, w_qrr,
      w_kd, row(kv_latent_norm), w_ku, w_vt, cosp, sinp)

    o_attn = pl.pallas_call(
        _attn_kernel,
        grid=(B, MLA_HEADS, S // ATTN_TQ),
        in_specs=[pl.BlockSpec((1, 1, ATTN_TQ, MLA_QK), lambda b, h, i: (b, h, i, 0)),
                  pl.BlockSpec((1, 1, Lp, MLA_QK), lambda b, h, i: (b, h, 0, 0)),
                  pl.BlockSpec((1, 1, nt, vt_rows, T), lambda b, h, i: (b, h, 0, 0, 0)),
                  pl.BlockSpec((1, ATTN_TQ, MLA_V), lambda b, h, i: (b, i, h))],
        out_specs=pl.BlockSpec((1, ATTN_TQ, MLA_V), lambda b, h, i: (b, i, h)),
        out_shape=jax.ShapeDtypeStruct((B, S, MLA_HEADS * MLA_V), BF16),
        scratch_shapes=[pltpu.VMEM((ATTN_TQ // ATTN_SUB, 8, ATTN_SUB), F32),
                        pltpu.VMEM((ATTN_TQ // ATTN_SUB, vt_rows, ATTN_SUB), F32),
                        pltpu.VMEM((ATTN_TQ // ATTN_SUB, ATTN_TK, ATTN_SUB), F32),
                        pltpu.VMEM((ATTN_TQ // ATTN_SUB, ATTN_TK, ATTN_SUB), F32),
                        pltpu.VMEM((ATTN_TQ // ATTN_SUB, 8, ATTN_SUB), F32),
                        pltpu.VMEM((ATTN_TQ // ATTN_SUB, 8, ATTN_SUB), F32),
                        pltpu.VMEM((ATTN_TQ // ATTN_SUB, ATTN_TK, ATTN_SUB), BF16)],
        compiler_params=_params("parallel", "parallel", "arbitrary"),
        name="mla_attn",
    )(q1, k1, v1, zs1)

    out = pl.pallas_call(
        _out_proj_last_kernel,
        grid=(B, S // (OUT_TILES * T)),
        in_specs=[pl.BlockSpec((1, OUT_TILES * T, MLA_HEADS * MLA_V), lambda b, i: (b, i, 0)),
                  _const_spec((MLA_HEADS * MLA_V, D)), _const_spec((1, D))]
        + [pl.BlockSpec((1, T, D), lambda b, i, j=j: (b, OUT_TILES * i + j + 1, 0)) for j in range(OUT_TILES)],
        out_specs=pl.BlockSpec((1, OUT_TILES * T, D), lambda b, i: (b, i, 0)),
        out_shape=jax.ShapeDtypeStruct((B, S, D), x.dtype),
        compiler_params=_params("parallel", "arbitrary"),
        name="mla_out",
    )(o_attn, mla_w_out[0].astype(BF16), row(post_norm[1]), *([h1] * OUT_TILES))
    return out
```

```python
import math

import jax
import jax.numpy as jnp
from jax import lax
from jax.experimental import pallas as pl
from jax.experimental.pallas import tpu as pltpu

NORM_EPS = 1e-6
N_META_ROWS = 16

GDN_QK_HEADS = 8
GDN_V_HEADS = 16
GDN_HEAD = 128
GDN_CONV_TAPS = 4
GDN_CHUNK = 64
GDN_QK_W = GDN_QK_HEADS * GDN_HEAD
GDN_V_W = GDN_V_HEADS * GDN_HEAD
GDN_CONV_W = 2 * GDN_QK_W + GDN_V_W

MLA_HEADS = 16
MLA_NOPE = 128
MLA_ROPE = 64
MLA_V = 128
MLA_Q_RANK = 256
MLA_KV_RANK = 128
MLA_QK = MLA_NOPE + MLA_ROPE
ROPE_THETA = 10000.0

LANES = 128
ROW_TILE = 256
FRONT_PAD = ROW_TILE - N_META_ROWS
WIDE_TILES = 3
OUT_TILES = 4
CONV_COLS = 512
HALO_ROWS = 8
ATTN_TQ = 2048
ATTN_SUB = 256
ATTN_TK = 1024
ATTN_GROUP = 128
ATTN_ONES_ROWS = 16
V7X_VMEM_BYTES = 64 * 1024 * 1024
VMEM_LIMIT = V7X_VMEM_BYTES - 4 * 1024 * 1024

F32 = jnp.float32
BF16 = jnp.bfloat16


def _dot(a, b):
    return jnp.dot(a, b, preferred_element_type=F32)


def _dot_nt(a, b):
    return lax.dot_general(a, b, (((1,), (1,)), ((), ())), preferred_element_type=F32)


def _dot_exact(a, b):
    return jnp.dot(a, b, preferred_element_type=F32, precision=lax.Precision.HIGHEST)


def _silu(x):
    return x * jax.nn.sigmoid(x)


def _softplus(x):
    return jnp.maximum(x, 0.0) + jnp.log1p(jnp.exp(-jnp.abs(x)))


def _rms_scale(x):
    return lax.rsqrt(jnp.mean(x * x, axis=-1, keepdims=True) + NORM_EPS)


def _const_spec(shape):
    nd = len(shape)
    return pl.BlockSpec(shape, lambda *_: (0,) * nd, pipeline_mode=pl.Buffered(1))


def _gdn_in_kernel(head_ref, gain_ref, win_ref, wba_ref, wbat_ref, convw_ref,
                   arow_ref, dtrow_ref, acol_ref, dtcol_ref, *refs):
    x_refs = refs[:WIDE_TILES]
    outs = refs[WIDE_TILES:WIDE_TILES + 7]
    halo_scr, buf_scr = refs[WIDE_TILES + 7:]
    i = pl.program_id(1)

    @pl.when(i == 0)
    def _():
        halo_scr[...] = jnp.zeros_like(halo_scr)

    ri = lax.broadcasted_iota(jnp.int32, (ROW_TILE, ROW_TILE), 0)
    ci = lax.broadcasted_iota(jnp.int32, (ROW_TILE, ROW_TILE), 1)
    same = (ri // GDN_CHUNK) == (ci // GDN_CHUNK)
    prefix = (jnp.where(same & (ri >= ci), 1.0, 0.0).astype(F32),
              jnp.where(same & (ri <= ci), 1.0, 0.0).astype(F32),
              jnp.where(same, 1.0, 0.0).astype(F32))
    for t, x_ref in enumerate(x_refs):
        x = x_ref[0] if t else jnp.where(i == 0, head_ref[...], x_ref[0])
        _gdn_in_tile(t, x, gain_ref, win_ref, wba_ref, wbat_ref, convw_ref,
                     arow_ref, dtrow_ref, acol_ref, dtcol_ref, prefix, outs, halo_scr, buf_scr)


def _gdn_in_tile(t, x, gain_ref, win_ref, wba_ref, wbat_ref, convw_ref,
                 arow_ref, dtrow_ref, acol_ref, dtcol_ref, prefix, outs, halo_scr, buf_scr):
    q_ref, k_ref, v_ref, zs_ref, gcol_ref, grow_ref, egl_ref = outs
    lower, upper, block = prefix
    rt = slice(t * ROW_TILE, (t + 1) * ROW_TILE)
    c0 = t * (ROW_TILE // GDN_CHUNK)
    hn = (x * _rms_scale(x) * gain_ref[...]).astype(BF16)

    lo = HALO_ROWS - (GDN_CONV_TAPS - 1)
    for c in range(GDN_CONV_W // CONV_COLS):
        cs = slice(c * CONV_COLS, (c + 1) * CONV_COLS)
        p = _dot(hn, win_ref[:, cs])
        for hh in range(CONV_COLS // GDN_HEAD):
            col = c * CONV_COLS + hh * GDN_HEAD
            buf_scr[hh, 0:HALO_ROWS, :] = halo_scr[:, col:col + GDN_HEAD]
            buf_scr[hh, HALO_ROWS:HALO_ROWS + ROW_TILE, :] = p[:, hh * GDN_HEAD:(hh + 1) * GDN_HEAD]
            halo_scr[:, col:col + GDN_HEAD] = p[ROW_TILE - HALO_ROWS:, hh * GDN_HEAD:(hh + 1) * GDN_HEAD]
            yh = convw_ref[0:1, col:col + GDN_HEAD] * buf_scr[hh, lo:lo + ROW_TILE, :]
            for j in range(1, GDN_CONV_TAPS):
                yh = yh + convw_ref[j:j + 1, col:col + GDN_HEAD] * buf_scr[hh, lo + j:lo + j + ROW_TILE, :]
            yh = _silu(yh)
            if col < 2 * GDN_QK_W:
                r = lax.rsqrt(jnp.sum(yh * yh, axis=-1, keepdims=True) + NORM_EPS)
                if col < GDN_QK_W:
                    q_ref[0, rt, col:col + GDN_HEAD] = (yh * (r * (GDN_HEAD ** -0.5))).astype(BF16)
                else:
                    k_ref[0, rt, col - GDN_QK_W:col - GDN_QK_W + GDN_HEAD] = (yh * r).astype(BF16)
            else:
                v_ref[0, rt, col - 2 * GDN_QK_W:col - 2 * GDN_QK_W + GDN_HEAD] = yh.astype(BF16)

    for c in range(GDN_V_W // CONV_COLS):
        cs = slice(c * CONV_COLS, (c + 1) * CONV_COLS)
        ws = slice(GDN_CONV_W + c * CONV_COLS, GDN_CONV_W + (c + 1) * CONV_COLS)
        zs_ref[0, rt, cs] = _silu(_dot(hn, win_ref[:, ws])).astype(BF16)

    nh = GDN_V_HEADS
    ba = _dot(hn, wba_ref[...])
    bat = _dot_nt(wbat_ref[...], hn)
    beta_c = 1.0 / (1.0 + jnp.exp(-ba))
    g_c = -jnp.exp(arow_ref[...]) * _softplus(ba + dtrow_ref[...])
    beta_r = 1.0 / (1.0 + jnp.exp(-bat[0:nh, :]))
    g_r = -jnp.exp(acol_ref[...]) * _softplus(bat[nh:2 * nh, :] + dtcol_ref[...])

    gc_c = _dot_exact(lower, g_c)
    gl_c = _dot_exact(block, g_c)
    gc_r = _dot_exact(g_r, upper)
    grp = lax.broadcasted_iota(jnp.int32, (ROW_TILE, LANES), 1) // nh
    gcol_ref[0, rt, :] = jnp.where(grp == 0, beta_c,
                                   jnp.where(grp == 1, gc_c,
                                             jnp.where(grp == 2, jnp.exp(gc_c),
                                                       jnp.where(grp == 3, jnp.exp(gl_c - gc_c), 0.0))))
    npair = nh // 2
    for kind, xr in enumerate((gc_r, beta_r, beta_r * jnp.exp(gc_r))):
        for c in range(ROW_TILE // GDN_CHUNK):
            cs = slice(c * GDN_CHUNK, (c + 1) * GDN_CHUNK)
            grow_ref[0, c0 + c, kind * npair:(kind + 1) * npair, :] = jnp.concatenate(
                [xr[0:npair, cs], xr[npair:nh, cs]], axis=-1)
    for c in range(ROW_TILE // GDN_CHUNK):
        cs = slice(c * GDN_CHUNK, (c + 1) * GDN_CHUNK)
        last = (c + 1) * GDN_CHUNK - 1
        gl = jnp.broadcast_to(gc_r[:, last:last + 1], (nh, LANES))
        egl_ref[0, c0 + c] = jnp.exp(gl)
        to_end = jnp.exp(gl[:, 0:GDN_CHUNK] - gc_r[:, cs])
        grow_ref[0, c0 + c, 3 * npair:4 * npair, :] =jnp.concatenate([to_end[0:npair], to_end[npair:nh]], axis=-1)


def _gdn_prep_kernel(q_ref, k_ref, gcol_ref, grow_ref, tb_ref, w_ref, a_ref, kdt_ref):
    C = GDN_CHUNK
    nh = GDN_V_HEADS
    npair = nh // 2
    row = lax.broadcasted_iota(jnp.int32, (C, LANES), 0)
    lane = lax.broadcasted_iota(jnp.int32, (C, LANES), 1)
    left = lane < C
    col = jnp.where(left, lane, lane - C)
    incl = row >= col
    strict = row > col
    eye = jnp.where(row == col, 1.0, 0.0).astype(F32)
    diag8 = strict & ((row // 8) == (col // 8))
    merges = tuple(strict & ((row // (2 * s)) == (col // (2 * s))) & ((row // s) != (col // s))
                   for s in (8, 16, 32))
    zero_b = jnp.zeros((C, LANES), BF16)

    def blockdiag(x):
        xb = x.astype(BF16)
        return jnp.concatenate([jnp.where(left, xb, zero_b), jnp.where(left, zero_b, xb)], axis=0)

    def pair_cols(g, base, p_):
        return jnp.take_along_axis(g, jnp.where(left, base + 2 * p_, base + 2 * p_ + 1), axis=1)

    cpt = ROW_TILE // C
    streams = [(c, p_) for c in range(cpt) for p_ in range(npair)]
    zero_k = jnp.zeros((C, GDN_HEAD), BF16)

    def row_tile(j, carry):
        rows = lambda c: pl.ds(pl.multiple_of(j * ROW_TILE + c * C, C), C)
        ms, kps, bege, betar = [], [], [], []
        for c, p_ in streams:
            rs, cj = rows(c), j * cpt + c
            ps = slice(p_ * GDN_HEAD, (p_ + 1) * GDN_HEAD)
            qp = q_ref[0, rs, ps]
            kp = k_ref[0, rs, ps]
            g = gcol_ref[0, rs, :]
            both = _dot_nt(jnp.concatenate([qp, kp], axis=0), jnp.concatenate([kp, kp], axis=0))
            qk2, kk2 = both[0:C], both[C:2 * C]
            decay = jnp.exp(jnp.where(incl, pair_cols(g, nh, p_) - grow_ref[0, cj, p_:p_ + 1, :], -jnp.inf))
            ms.append(jnp.where(strict, pair_cols(g, 0, p_) * kk2 * decay, 0.0))
            a_ref[0, rs, ps] = (qk2 * decay).astype(BF16)
            kpf = kp.astype(F32)
            kdt_ref[0, cj, p_] = (jnp.concatenate([kpf, kpf], axis=0).T
                                  * grow_ref[0, cj, 3 * npair + p_:3 * npair + p_ + 1, :]).astype(BF16)
            betar.append(grow_ref[0, cj, npair + p_:npair + p_ + 1, :])
            kps.append(kp)
            bege.append(grow_ref[0, cj, 2 * npair + p_:2 * npair + p_ + 1, :])

        m8 = [jnp.where(diag8, m, 0.0) for m in ms]
        q2 = [_dot(x.astype(BF16), blockdiag(x)) for x in m8]
        pinv = [eye - x for x in m8]
        pinv = [p + _dot(p.astype(BF16), blockdiag(y)) for p, y in zip(pinv, q2)]
        q4 = [_dot(y.astype(BF16), blockdiag(y)) for y in q2]
        pinv = [p + _dot(p.astype(BF16), blockdiag(y)) for p, y in zip(pinv, q4)]
        for mask in merges:
            cp = [_dot(jnp.where(mask, m, 0.0).astype(BF16), blockdiag(p)) for m, p in zip(ms, pinv)]
            pinv = [p - _dot(p.astype(BF16), blockdiag(y)) for p, y in zip(pinv, cp)]

        for (c, p_), p, kp, bg, br in zip(streams, pinv, kps, bege, betar):
            rs = rows(c)
            ps = slice(p_ * GDN_HEAD, (p_ + 1) * GDN_HEAD)
            kbd = jnp.concatenate([jnp.concatenate([kp, zero_k], axis=1),
                                   jnp.concatenate([zero_k, kp], axis=1)], axis=0)
            w_ref[0, rs, 2 * p_ * GDN_HEAD:(2 * p_ + 2) * GDN_HEAD] = _dot((p * bg).astype(BF16), kbd).astype(BF16)
            tb_ref[0, rs, ps] = (p * br).astype(BF16)
        return carry

    lax.fori_loop(0, q_ref.shape[1] // ROW_TILE, row_tile, 0)


def _gdn_scan_kernel(tb_ref, w_ref, a_ref, kdt_ref, q_ref, v_ref, zs_ref, gcol_ref, egl_ref, onorm_ref, o_ref,
                     state_scr):
    @pl.when(pl.program_id(0) == 0)
    def _():
        state_scr[...] = jnp.zeros_like(state_scr)

    C = GDN_CHUNK
    nh = GDN_V_HEADS
    nb = q_ref.shape[0]
    onorm = onorm_ref[...]
    zero_v = jnp.zeros((C, GDN_HEAD), BF16)
    ones_sq = jnp.ones((GDN_HEAD, GDN_HEAD), BF16)

    def blockdiag(x0, x1):
        return jnp.concatenate([jnp.concatenate([x0, zero_v], axis=1),
                                jnp.concatenate([zero_v, x1], axis=1)], axis=0)

    pairs = [(b, p_) for b in range(nb) for p_ in range(nh // 2)]
    heads = [(b, h) for b, p_ in pairs for h in (2 * p_, 2 * p_ + 1)]
    hsl = lambda h: slice(h * GDN_HEAD, (h + 1) * GDN_HEAD)

    def chunk(c, carry):
        rs = pl.ds(pl.multiple_of(c * C, C), C)
        u2 = [_dot(tb_ref[b, rs, hsl(p_)], blockdiag(v_ref[b, rs, hsl(2 * p_)], v_ref[b, rs, hsl(2 * p_ + 1)]))
              for b, p_ in pairs]
        u = {(b, 2 * p_ + j): x[:, j * GDN_HEAD:(j + 1) * GDN_HEAD]
             for (b, p_), x in zip(pairs, u2) for j in (0, 1)}
        s_old = {bh: state_scr[bh[0], bh[1]] for bh in heads}
        wq = {(b, h): _dot(jnp.concatenate([w_ref[b, rs, hsl(h)], q_ref[b, rs, hsl(h // 2)]], axis=0),
                           s_old[(b, h)].astype(BF16)) for b, h in heads}
        vnb = {bh: (u[bh] - wq[bh][0:C]).astype(BF16) for bh in heads}
        od = [_dot(jnp.concatenate([a_ref[b, rs, hsl(p_)], kdt_ref[b, c, p_]], axis=0),
                   blockdiag(vnb[(b, 2 * p_)], vnb[(b, 2 * p_ + 1)])) for b, p_ in pairs]
        os = {}
        for (b, p_), x in zip(pairs, od):
            for j in (0, 1):
                h = 2 * p_ + j
                js = slice(j * GDN_HEAD, (j + 1) * GDN_HEAD)
                erow = (h % 2) * (nh // 2) + h // 2
                state_scr[b, h] = s_old[(b, h)] * egl_ref[b, c, erow:erow + 1, :] + x[C:, js]
                os[(b, h)] = gcol_ref[b, rs, 2 * nh + h:2 * nh + h + 1] * wq[(b, h)][C:2 * C] + x[0:C, js]
        sq = {bh: _dot((o * o).astype(BF16), ones_sq) for bh, o in os.items()}
        for (b, h), o in os.items():
            on = o * lax.rsqrt(sq[(b, h)] * (1.0 / GDN_HEAD) + NORM_EPS) * onorm
            o_ref[b, rs, hsl(h)] = (on * zs_ref[b, rs, hsl(h)].astype(F32)).astype(BF16)
        return carry

    lax.fori_loop(0, q_ref.shape[1] // C, chunk, 0, unroll=2)


def _out_proj_first_kernel(o_ref, w_ref, gain_ref, head_ref, *refs):
    x_refs, h_ref = refs[:-1], refs[-1]
    y = _dot(o_ref[0], w_ref[...])
    yn = y * _rms_scale(y) * gain_ref[...]
    for j, x_ref in enumerate(x_refs):
        res = x_ref[0] if j else jnp.where(pl.program_id(1) == 0, head_ref[...], x_ref[0])
        h_ref[0, j * ROW_TILE:(j + 1) * ROW_TILE, :] = res + yn[j * ROW_TILE:(j + 1) * ROW_TILE]


def _out_proj_last_kernel(o_ref, w_ref, gain_ref, *refs):
    h_refs, out_ref = refs[:-1], refs[-1]
    y = _dot(o_ref[0], w_ref[...])
    yn = y * _rms_scale(y) * gain_ref[...]
    for j, h_ref in enumerate(h_refs):
        out_ref[0, j * ROW_TILE:(j + 1) * ROW_TILE, :] = h_ref[0] + yn[j * ROW_TILE:(j + 1) * ROW_TILE]


def _mla_in_kernel(h_ref, pre_ref, kvn_ref, win_ref, qln_ref, wqn_ref, wqr_ref, wqrr_ref,
                   wkd_ref, kvln_ref, wku_ref, wvt_ref, cos_ref, sin_ref,
                   q_ref, kk_ref, vt_ref, zs_ref):
    h = h_ref[0]
    hr = h * _rms_scale(h)
    hn = (hr * pre_ref[...]).astype(BF16)
    hk = (hr * kvn_ref[...]).astype(BF16)
    cosp = cos_ref[...]
    sinp = sin_ref[...]

    for c in range((win_ref.shape[1] - MLA_Q_RANK) // CONV_COLS):
        cs = slice(c * CONV_COLS, (c + 1) * CONV_COLS)
        ws = slice(MLA_Q_RANK + c * CONV_COLS, MLA_Q_RANK + (c + 1) * CONV_COLS)
        zs_ref[0, :, cs] = _silu(_dot(hn, win_ref[:, ws])).astype(BF16)

    cq = _dot(hn, win_ref[:, 0:MLA_Q_RANK])
    cq = (cq * _rms_scale(cq) * qln_ref[...]).astype(BF16)
    scale = MLA_QK ** -0.5 * math.log2(math.e)
    for hd in range(MLA_HEADS):
        hs = slice(hd * LANES, (hd + 1) * LANES)
        q_ref[0, hd, :, 0:MLA_NOPE] = (_dot(cq, wqn_ref[:, hs]) * scale).astype(BF16)
    per_tile = LANES // MLA_ROPE
    for t in range(MLA_HEADS // per_tile):
        ts = slice(t * LANES, (t + 1) * LANES)
        qr = ((_dot(cq, wqr_ref[:, ts]) * cosp + _dot(cq, wqrr_ref[:, ts]) * sinp) * scale).astype(BF16)
        for u in range(per_tile):
            q_ref[0, t * per_tile + u, :, MLA_NOPE:MLA_QK] = qr[:, u * MLA_ROPE:(u + 1) * MLA_ROPE]

    ckr = _dot(hk, wkd_ref[...])
    lat = ckr[:, 0:MLA_KV_RANK]
    ckv = (lat * _rms_scale(lat) * kvln_ref[...]).astype(BF16)
    kr = ckr[:, LANES:2 * LANES] * cosp + ckr[:, 2 * LANES:3 * LANES] * sinp
    krb = kr[:, 0:MLA_ROPE].astype(BF16)
    ones_rows = jnp.ones((ATTN_ONES_ROWS, ROW_TILE), BF16)
    for hd in range(MLA_HEADS):
        kn = _dot(ckv, wku_ref[:, hd * MLA_NOPE:(hd + 1) * MLA_NOPE])
        kk_ref[0, hd, :, 0:MLA_NOPE] = kn.astype(BF16)
        kk_ref[0, hd, :, MLA_NOPE:MLA_QK] = krb
        vt = _dot_nt(wvt_ref[hd * MLA_V:(hd + 1) * MLA_V, :], ckv)
        vt_ref[0, hd, 0, 0:MLA_V, :] = vt.astype(BF16)
        vt_ref[0, hd, 0, MLA_V:MLA_V + ATTN_ONES_ROWS, :] = ones_rows


def _attn_kernel(q_ref, k_ref, vt_ref, zs_ref, o_ref, m_scr, al_scr, sa_scr, sb_scr, ma_scr, mb_scr, p_scr):
    i = pl.program_id(2)
    tq, sb, tk = ATTN_TQ, ATTN_SUB, ATTN_TK
    m_scr[...] = jnp.full_like(m_scr, -jnp.inf)
    al_scr[...] = jnp.zeros_like(al_scr)

    def rows(r):
        return slice(r * sb, (r + 1) * sb)

    def scores(r, k0, nk):
        return _dot_nt(k_ref[0, 0, pl.ds(k0, nk), :], q_ref[0, 0, rows(r), :])

    def weighted_values(p, k0, nk):
        if nk < ROW_TILE:
            return _dot(vt_ref[0, 0, 0, :, k0:k0 + nk], p[...])
        kb = k0 // ROW_TILE
        out = _dot(vt_ref[0, 0, kb], p[0:ROW_TILE, :])
        for jb in range(1, nk // ROW_TILE):
            out = out + _dot(vt_ref[0, 0, kb + jb], p[jb * ROW_TILE:(jb + 1) * ROW_TILE, :])
        return out

    def absorb(r, parts):
        m_old = m_scr[r]
        m_new = m_old
        for s, _, _ in parts:
            m_new = jnp.maximum(m_new, jnp.max(s, axis=0, keepdims=True))
        al = jnp.exp2(m_old - m_new)[0:1, :] * al_scr[r]
        for s, k0, nk in parts:
            al = al + weighted_values(jnp.exp2(s - m_new[0:1, :]).astype(BF16), k0, nk)
        al_scr[r] = al
        m_scr[r] = m_new

    def stage_block(r, bufs, k0):
        s_buf, mx_buf = bufs
        s = scores(r, k0, tk)
        s_buf[r] = s
        mx_buf[r] = jnp.broadcast_to(jnp.max(s, axis=0, keepdims=True), mx_buf.shape[1:])

    def absorb_block(r, bufs, k0):
        s_buf, mx_buf = bufs
        m_old = m_scr[r]
        m_new = jnp.maximum(m_old, mx_buf[r])
        for g in range(tk // ATTN_GROUP):
            gr = slice(g * ATTN_GROUP, (g + 1) * ATTN_GROUP)
            p_scr[r, gr, :] = jnp.exp2(s_buf[r, gr, :] - m_new[0:1, :]).astype(BF16)
        al_scr[r] = jnp.exp2(m_old - m_new)[0:1, :] * al_scr[r] + weighted_values(p_scr.at[r], k0, tk)
        m_scr[r] = m_new

    nsub = tq // sb
    block0 = lambda j: pl.multiple_of(ROW_TILE + j * tk, ROW_TILE)

    ki = lax.broadcasted_iota(jnp.int32, (sb, sb), 0)
    qi = lax.broadcasted_iota(jnp.int32, (sb, sb), 1)
    base = pl.multiple_of(ROW_TILE + i * tq, ROW_TILE)
    meta0 = ROW_TILE - LANES
    is_meta = lax.broadcasted_iota(jnp.int32, (LANES, sb), 0) >= FRONT_PAD - meta0

    def mask_diag(s):
        tri = jnp.where(ki <= qi, s[-sb:, :], -jnp.inf)
        return tri if s.shape[0] == sb else jnp.concatenate([s[:-sb, :], tri], axis=0)

    base2 = pl.multiple_of(base + tk, ROW_TILE)

    def diag_scores_a(r):
        w = (r + 1) * sb
        if w <= tk:
            sa_scr[r, 0:w, :] = scores(r, base, w)
        else:
            sa_scr[r, 0:w - tk, :] = scores(r, base2, w - tk)

    def diag_scores_b(r):
        if (r + 1) * sb > tk:
            sb_scr[r] = scores(r, base, tk)

    assert tq == 2 * tk
    trips = i

    @pl.when(trips == 0)
    def _():
        for r in range(nsub):
            diag_scores_a(r)
            diag_scores_b(r)

    @pl.when(trips > 0)
    def _():
        buf_a, buf_b = (sa_scr, ma_scr), (sb_scr, mb_scr)
        for r in range(nsub):
            stage_block(r, buf_a, block0(0))

        def body(jj, carry):
            j = 2 * jj
            for cur, nxt, step in ((buf_a, buf_b, 0), (buf_b, buf_a, 1)):
                for r in range(nsub):
                    stage_block(r, nxt, block0(j + step + 1))
                    absorb_block(r, cur, block0(j + step))
            return carry

        lax.fori_loop(0, trips - 1, body, 0)
        j = 2 * (trips - 1)
        for r in range(nsub):
            stage_block(r, buf_b, block0(j + 1))
            absorb_block(r, buf_a, block0(j))
        for r in range(nsub):
            diag_scores_a(r)
            absorb_block(r, buf_b, block0(j + 1))
            diag_scores_b(r)

    s_meta = [jnp.where(is_meta, scores(r, meta0, LANES), -jnp.inf) for r in range(nsub)]
    for r in range(nsub):
        w = (r + 1) * sb
        if w <= tk:
            absorb(r, [(s_meta[r], meta0, LANES), (mask_diag(sa_scr[r, 0:w, :]), base, w)])
        else:
            absorb(r, [(s_meta[r], meta0, LANES), (sb_scr[r], base, tk),
                       (mask_diag(sa_scr[r, 0:w - tk, :]), base2, w - tk)])

    for r in range(nsub):
        o_t = al_scr[r, 0:MLA_V, :] / al_scr[r, MLA_V:MLA_V + 1, :]
        o_ref[0, rows(r), :] = (o_t.T * zs_ref[0, rows(r), :].astype(F32)).astype(BF16)


def _params(*sem):
    return pltpu.CompilerParams(dimension_semantics=sem, vmem_limit_bytes=VMEM_LIMIT)


def kernel(x, meta_tokens, pre_norm, post_norm, gdn_w_in, gdn_conv_w, gdn_a_log, gdn_dt_bias, gdn_out_norm, gdn_w_out, kv_norm, kv_w_down, kv_latent_norm, kv_w_up, mla_w_in, mla_q_latent_norm, mla_w_q_up, mla_w_out):
    B, S, D = x.shape
    assert S % ATTN_TQ == 0 and S % (OUT_TILES * ROW_TILE) == 0 and (S // ROW_TILE + 1) % WIDE_TILES == 0
    assert gdn_w_in.shape[0] == 1 and mla_w_in.shape[0] == 1
    T = ROW_TILE
    nt = S // T + 1
    Lp = nt * T
    nch = Lp // GDN_CHUNK
    nh = GDN_V_HEADS

    head = jnp.concatenate([jnp.zeros((FRONT_PAD, D), F32), meta_tokens.astype(F32)], axis=0)
    row = lambda a: a.reshape(1, -1).astype(F32)

    w_in = gdn_w_in[0]
    w_ba_f = w_in[:, GDN_CONV_W + GDN_V_W:]
    w_b, w_a = w_ba_f[:, :nh], w_ba_f[:, nh:]
    w_ba = jnp.concatenate([w_b, w_a, w_a, w_a, jnp.zeros((D, LANES - 4 * nh), F32)], axis=-1).astype(BF16)
    eo = lambda a: jnp.concatenate([a[0::2], a[1::2]], axis=0)
    w_bat = jnp.concatenate([eo(w_b.T), eo(w_a.T)], axis=0).astype(BF16)
    a_log = gdn_a_log[0].astype(F32)
    dt_b = gdn_dt_bias[0].astype(F32)
    lane_groups = lambda a: jnp.concatenate([jnp.zeros((nh,), F32), a, a, a,
                                             jnp.zeros((LANES - 4 * nh,), F32)]).reshape(1, LANES)

    tile_spec = lambda w: pl.BlockSpec((1, T, w), lambda b, i: (b, i, 0))
    wide_spec = lambda w: pl.BlockSpec((1, WIDE_TILES * T, w), lambda b, i: (b, i, 0))

    x_specs = [pl.BlockSpec((1, T, D), lambda b, i, j=j: (b, jnp.maximum(WIDE_TILES * i + j - 1, 0), 0))
               for j in range(WIDE_TILES)]
    gate_spec = lambda r: pl.BlockSpec((1, WIDE_TILES * T // GDN_CHUNK, r, LANES), lambda b, i: (b, i, 0, 0))
    q, k, v, zs, gcol, grow, egl = pl.pallas_call(
        _gdn_in_kernel,
        grid=(B, nt // WIDE_TILES),
        in_specs=[_const_spec((T, D)), _const_spec((1, D)),
                  _const_spec(w_in.shape), _const_spec((D, LANES)), _const_spec((2 * nh, D)),
                  _const_spec((GDN_CONV_TAPS, GDN_CONV_W)),
                  _const_spec((1, LANES)), _const_spec((1, LANES)), _const_spec((nh, 1)), _const_spec((nh, 1))]
        + x_specs,
        out_specs=[wide_spec(GDN_QK_W), wide_spec(GDN_QK_W), wide_spec(GDN_V_W), wide_spec(GDN_V_W),
                   wide_spec(LANES), gate_spec(2 * nh), gate_spec(nh)],
        out_shape=[jax.ShapeDtypeStruct((B, Lp, GDN_QK_W), BF16),
                   jax.ShapeDtypeStruct((B, Lp, GDN_QK_W), BF16),
                   jax.ShapeDtypeStruct((B, Lp, GDN_V_W), BF16),
                   jax.ShapeDtypeStruct((B, Lp, GDN_V_W), BF16),
                   jax.ShapeDtypeStruct((B, Lp, LANES), F32),
                   jax.ShapeDtypeStruct((B, nch, 2 * nh, LANES), F32),
                   jax.ShapeDtypeStruct((B, nch, nh, LANES), F32)],
        scratch_shapes=[pltpu.VMEM((HALO_ROWS, GDN_CONV_W), F32),
                        pltpu.VMEM((CONV_COLS // GDN_HEAD, HALO_ROWS + T, GDN_HEAD), F32)],
        compiler_params=_params("arbitrary", "arbitrary"),
        name="gdn_in",
    )(head, row(pre_norm[0]), w_in.astype(BF16), w_ba, w_bat, gdn_conv_w[0].astype(F32),
      lane_groups(a_log), lane_groups(dt_b), eo(a_log).reshape(nh, 1), eo(dt_b).reshape(nh, 1),
      *([x] * WIDE_TILES))

    npair = nh // 2
    tb, wk, aqk, kdt = pl.pallas_call(
        _gdn_prep_kernel,
        grid=(B, nt // WIDE_TILES),
        in_specs=[wide_spec(GDN_QK_W), wide_spec(GDN_QK_W), wide_spec(LANES),
                  pl.BlockSpec((1, WIDE_TILES * T // GDN_CHUNK, 2 * nh, LANES), lambda b, i: (b, i, 0, 0))],
        out_specs=[wide_spec(GDN_QK_W), wide_spec(GDN_V_W), wide_spec(GDN_QK_W),
                   pl.BlockSpec((1, WIDE_TILES * T // GDN_CHUNK, npair, GDN_HEAD, LANES),
                                lambda b, i: (b, i, 0, 0, 0))],
        out_shape=[jax.ShapeDtypeStruct((B, Lp, GDN_QK_W), BF16),
                   jax.ShapeDtypeStruct((B, Lp, GDN_V_W), BF16),
                   jax.ShapeDtypeStruct((B, Lp, GDN_QK_W), BF16),
                   jax.ShapeDtypeStruct((B, nch, npair, GDN_HEAD, LANES), BF16)],
        compiler_params=_params("parallel", "parallel"),
        name="gdn_prep",
    )(q, k, gcol, grow)

    cpt = T // GDN_CHUNK
    chunk_spec = lambda w: pl.BlockSpec((B, T, w), lambda n: (0, n, 0))
    o_gdn = pl.pallas_call(
        _gdn_scan_kernel,
        grid=(nt,),
        in_specs=[chunk_spec(GDN_QK_W), chunk_spec(GDN_V_W), chunk_spec(GDN_QK_W),
                  pl.BlockSpec((B, cpt, npair, GDN_HEAD, LANES), lambda n: (0, n, 0, 0, 0)),
                  chunk_spec(GDN_QK_W), chunk_spec(GDN_V_W), chunk_spec(GDN_V_W), chunk_spec(LANES),
                  pl.BlockSpec((B, cpt, nh, LANES), lambda n: (0, n, 0, 0)),
                  _const_spec((1, GDN_HEAD))],
        out_specs=chunk_spec(GDN_V_W),
        out_shape=jax.ShapeDtypeStruct((B, Lp, GDN_V_W), BF16),
        scratch_shapes=[pltpu.VMEM((B, nh, GDN_HEAD, GDN_HEAD), F32)],
        compiler_params=_params("arbitrary"),
        name="gdn_scan",
    )(tb, wk, aqk, kdt, q, v, zs, gcol, egl, row(gdn_out_norm[0]))

    h1 = pl.pallas_call(
        _out_proj_first_kernel,
        grid=(B, nt // WIDE_TILES),
        in_specs=[wide_spec(GDN_V_W), _const_spec((GDN_V_W, D)), _const_spec((1, D)), _const_spec((T, D))]
        + x_specs,
        out_specs=wide_spec(D),
        out_shape=jax.ShapeDtypeStruct((B, Lp, D), F32),
        compiler_params=_params("parallel", "arbitrary"),
        name="gdn_out",
    )(o_gdn, gdn_w_out[0].astype(BF16), row(post_norm[0]), head, *([x] * WIDE_TILES))

    w_in1 = mla_w_in[0]
    zw = w_in1.shape[1] - MLA_Q_RANK
    half = MLA_ROPE // 2
    rot = lambda w: jnp.concatenate([-w[..., half:], w[..., :half]], axis=-1)
    lane_pad = lambda w: jnp.pad(w, [(0, 0)] * (w.ndim - 1) + [(0, LANES - w.shape[-1])])
    wq = mla_w_q_up[0].reshape(MLA_Q_RANK, MLA_HEADS, MLA_QK)
    w_qn = wq[..., :MLA_NOPE].reshape(MLA_Q_RANK, MLA_HEADS * MLA_NOPE).astype(BF16)
    w_qr = wq[..., MLA_NOPE:].reshape(MLA_Q_RANK, MLA_HEADS * MLA_ROPE).astype(BF16)
    w_qrr = rot(wq[..., MLA_NOPE:]).reshape(MLA_Q_RANK, MLA_HEADS * MLA_ROPE).astype(BF16)
    wkd_r = kv_w_down[:, MLA_KV_RANK:]
    w_kd = jnp.concatenate([kv_w_down[:, :MLA_KV_RANK], lane_pad(wkd_r), lane_pad(rot(wkd_r))],
                           axis=-1).astype(BF16)
    wku = kv_w_up.reshape(MLA_KV_RANK, MLA_HEADS, MLA_NOPE + MLA_V)
    w_ku = wku[..., :MLA_NOPE].reshape(MLA_KV_RANK, MLA_HEADS * MLA_NOPE).astype(BF16)
    w_vt = jnp.transpose(wku[..., MLA_NOPE:], (1, 2, 0)).reshape(MLA_HEADS * MLA_V, MLA_KV_RANK).astype(BF16)
    vt_rows = MLA_V + ATTN_ONES_ROWS

    inv = ROPE_THETA ** (-jnp.arange(0, MLA_ROPE, 2, dtype=F32) / MLA_ROPE)
    pos = (jnp.arange(Lp, dtype=jnp.int32) - FRONT_PAD).astype(F32)
    ang = pos[:, None] * inv[None, :]
    cosp = jnp.tile(jnp.cos(ang), (1, LANES // half))
    sinp = jnp.tile(jnp.sin(ang), (1, LANES // half))

    head_tile = lambda w: pl.BlockSpec((1, MLA_HEADS, T, w), lambda b, i: (b, 0, i, 0))
    q1, k1, v1, zs1 = pl.pallas_call(
        _mla_in_kernel,
        grid=(B, nt),
        in_specs=[tile_spec(D), _const_spec((1, D)), _const_spec((1, D)),
                  _const_spec(w_in1.shape), _const_spec((1, MLA_Q_RANK)),
                  _const_spec((MLA_Q_RANK, MLA_HEADS * MLA_NOPE)),
                  _const_spec((MLA_Q_RANK, MLA_HEADS * MLA_ROPE)),
                  _const_spec((MLA_Q_RANK, MLA_HEADS * MLA_ROPE)),
                  _const_spec((D, 3 * LANES)), _const_spec((1, MLA_KV_RANK)),
                  _const_spec((MLA_KV_RANK, MLA_HEADS * MLA_NOPE)), _const_spec((MLA_HEADS * MLA_V, MLA_KV_RANK)),
                  pl.BlockSpec((T, LANES), lambda b, i: (i, 0)),
                  pl.BlockSpec((T, LANES), lambda b, i: (i, 0))],
        out_specs=[pl.BlockSpec((1, MLA_HEADS, T, MLA_QK), lambda b, i: (b, 0, jnp.maximum(i - 1, 0), 0)),
                   head_tile(MLA_QK),
                   pl.BlockSpec((1, MLA_HEADS, 1, vt_rows, T), lambda b, i: (b, 0, i, 0, 0)),
                   pl.BlockSpec((1, T, zw), lambda b, i: (b, jnp.maximum(i - 1, 0), 0))],
        out_shape=[jax.ShapeDtypeStruct((B, MLA_HEADS, S, MLA_QK), BF16),
                   jax.ShapeDtypeStruct((B, MLA_HEADS, Lp, MLA_QK), BF16),
                   jax.ShapeDtypeStruct((B, MLA_HEADS, nt, vt_rows, T), BF16),
                   jax.ShapeDtypeStruct((B, S, zw), BF16)],
        compiler_params=_params("arbitrary", "arbitrary"),
        name="mla_in",
    )(h1, row(pre_norm[1]), row(kv_norm), w_in1.astype(BF16), row(mla_q_latent_norm[0]), w_qn, w_qr, w_qrr,
      w_kd, row(kv_latent_norm), w_ku, w_vt, cosp, sinp)

    o_attn = pl.pallas_call(
        _attn_kernel,
        grid=(B, MLA_HEADS, S // ATTN_TQ),
        in_specs=[pl.BlockSpec((1, 1, ATTN_TQ, MLA_QK), lambda b, h, i: (b, h, i, 0)),
                  pl.BlockSpec((1, 1, Lp, MLA_QK), lambda b, h, i: (b, h, 0, 0)),
                  pl.BlockSpec((1, 1, nt, vt_rows, T), lambda b, h, i: (b, h, 0, 0, 0)),
                  pl.BlockSpec((1, ATTN_TQ, MLA_V), lambda b, h, i: (b, i, h))],
        out_specs=pl.BlockSpec((1, ATTN_TQ, MLA_V), lambda b, h, i: (b, i, h)),
        out_shape=jax.ShapeDtypeStruct((B, S, MLA_HEADS * MLA_V), BF16),
        scratch_shapes=[pltpu.VMEM((ATTN_TQ // ATTN_SUB, 8, ATTN_SUB), F32),
                        pltpu.VMEM((ATTN_TQ // ATTN_SUB, vt_rows, ATTN_SUB), F32),
                        pltpu.VMEM((ATTN_TQ // ATTN_SUB, ATTN_TK, ATTN_SUB), F32),
                        pltpu.VMEM((ATTN_TQ // ATTN_SUB, ATTN_TK, ATTN_SUB), F32),
                        pltpu.VMEM((ATTN_TQ // ATTN_SUB, 8, ATTN_SUB), F32),
                        pltpu.VMEM((ATTN_TQ // ATTN_SUB, 8, ATTN_SUB), F32),
                        pltpu.VMEM((ATTN_TQ // ATTN_SUB, ATTN_TK, ATTN_SUB), BF16)],
        compiler_params=_params("parallel", "parallel", "arbitrary"),
        name="mla_attn",
    )(q1, k1, v1, zs1)

    out = pl.pallas_call(
        _out_proj_last_kernel,
        grid=(B, S // (OUT_TILES * T)),
        in_specs=[pl.BlockSpec((1, OUT_TILES * T, MLA_HEADS * MLA_V), lambda b, i: (b, i, 0)),
                  _const_spec((MLA_HEADS * MLA_V, D)), _const_spec((1, D))]
        + [pl.BlockSpec((1, T, D), lambda b, i, j=j: (b, OUT_TILES * i + j + 1, 0)) for j in range(OUT_TILES)],
        out_specs=pl.BlockSpec((1, OUT_TILES * T, D), lambda b, i: (b, i, 0)),
        out_shape=jax.ShapeDtypeStruct((B, S, D), x.dtype),
        compiler_params=_params("parallel", "arbitrary"),
        name="mla_out",
    )(o_attn, mla_w_out[0].astype(BF16), row(post_norm[1]), *([h1] * OUT_TILES))
    return out
```

```python
import math

import jax
import jax.numpy as jnp
from jax import lax
from jax.experimental import pallas as pl
from jax.experimental.pallas import tpu as pltpu

NORM_EPS = 1e-6
N_META_ROWS = 16

GDN_QK_HEADS = 8
GDN_V_HEADS = 16
GDN_HEAD = 128
GDN_CONV_TAPS = 4
GDN_CHUNK = 64
GDN_QK_W = GDN_QK_HEADS * GDN_HEAD
GDN_V_W = GDN_V_HEADS * GDN_HEAD
GDN_CONV_W = 2 * GDN_QK_W + GDN_V_W

MLA_HEADS = 16
MLA_NOPE = 128
MLA_ROPE = 64
MLA_V = 128
MLA_Q_RANK = 256
MLA_KV_RANK = 128
MLA_QK = MLA_NOPE + MLA_ROPE
ROPE_THETA = 10000.0

LANES = 128
ROW_TILE = 256
FRONT_PAD = ROW_TILE - N_META_ROWS
WIDE_TILES = 3
OUT_TILES = 4
CONV_COLS = 512
HALO_ROWS = 8
ATTN_TQ = 2048
ATTN_SUB = 256
ATTN_TK = 1024
ATTN_GROUP = 128
ATTN_ONES_ROWS = 16
V7X_VMEM_BYTES = 64 * 1024 * 1024
VMEM_LIMIT = V7X_VMEM_BYTES - 4 * 1024 * 1024

F32 = jnp.float32
BF16 = jnp.bfloat16


def _dot(a, b):
    return jnp.dot(a, b, preferred_element_type=F32)


def _dot_nt(a, b):
    return lax.dot_general(a, b, (((1,), (1,)), ((), ())), preferred_element_type=F32)


def _dot_exact(a, b):
    return jnp.dot(a, b, preferred_element_type=F32, precision=lax.Precision.HIGHEST)


def _silu(x):
    return x * jax.nn.sigmoid(x)


def _softplus(x):
    return jnp.maximum(x, 0.0) + jnp.log1p(jnp.exp(-jnp.abs(x)))


def _rms_scale(x):
    return lax.rsqrt(jnp.mean(x * x, axis=-1, keepdims=True) + NORM_EPS)


def _const_spec(shape):
    nd = len(shape)
    return pl.BlockSpec(shape, lambda *_: (0,) * nd, pipeline_mode=pl.Buffered(1))


def _gdn_in_kernel(head_ref, gain_ref, win_ref, wba_ref, wbat_ref, convw_ref,
                   arow_ref, dtrow_ref, acol_ref, dtcol_ref, *refs):
    x_refs = refs[:WIDE_TILES]
    outs = refs[WIDE_TILES:WIDE_TILES + 7]
    halo_scr, buf_scr = refs[WIDE_TILES + 7:]
    i = pl.program_id(1)

    @pl.when(i == 0)
    def _():
        halo_scr[...] = jnp.zeros_like(halo_scr)

    ri = lax.broadcasted_iota(jnp.int32, (ROW_TILE, ROW_TILE), 0)
    ci = lax.broadcasted_iota(jnp.int32, (ROW_TILE, ROW_TILE), 1)
    same = (ri // GDN_CHUNK) == (ci // GDN_CHUNK)
    prefix = (jnp.where(same & (ri >= ci), 1.0, 0.0).astype(F32),
              jnp.where(same & (ri <= ci), 1.0, 0.0).astype(F32),
              jnp.where(same, 1.0, 0.0).astype(F32))
    for t, x_ref in enumerate(x_refs):
        x = x_ref[0] if t else jnp.where(i == 0, head_ref[...], x_ref[0])
        _gdn_in_tile(t, x, gain_ref, win_ref, wba_ref, wbat_ref, convw_ref,
                     arow_ref, dtrow_ref, acol_ref, dtcol_ref, prefix, outs, halo_scr, buf_scr)


def _gdn_in_tile(t, x, gain_ref, win_ref, wba_ref, wbat_ref, convw_ref,
                 arow_ref, dtrow_ref, acol_ref, dtcol_ref, prefix, outs, halo_scr, buf_scr):
    q_ref, k_ref, v_ref, zs_ref, gcol_ref, grow_ref, egl_ref = outs
    lower, upper, block = prefix
    rt = slice(t * ROW_TILE, (t + 1) * ROW_TILE)
    c0 = t * (ROW_TILE // GDN_CHUNK)
    hn = (x * _rms_scale(x) * gain_ref[...]).astype(BF16)

    lo = HALO_ROWS - (GDN_CONV_TAPS - 1)
    for c in range(GDN_CONV_W // CONV_COLS):
        cs = slice(c * CONV_COLS, (c + 1) * CONV_COLS)
        p = _dot(hn, win_ref[:, cs])
        for hh in range(CONV_COLS // GDN_HEAD):
            col = c * CONV_COLS + hh * GDN_HEAD
            buf_scr[hh, 0:HALO_ROWS, :] = halo_scr[:, col:col + GDN_HEAD]
            buf_scr[hh, HALO_ROWS:HALO_ROWS + ROW_TILE, :] = p[:, hh * GDN_HEAD:(hh + 1) * GDN_HEAD]
            halo_scr[:, col:col + GDN_HEAD] = p[ROW_TILE - HALO_ROWS:, hh * GDN_HEAD:(hh + 1) * GDN_HEAD]
            yh = convw_ref[0:1, col:col + GDN_HEAD] * buf_scr[hh, lo:lo + ROW_TILE, :]
            for j in range(1, GDN_CONV_TAPS):
                yh = yh + convw_ref[j:j + 1, col:col + GDN_HEAD] * buf_scr[hh, lo + j:lo + j + ROW_TILE, :]
            yh = _silu(yh)
            if col < 2 * GDN_QK_W:
                r = lax.rsqrt(jnp.sum(yh * yh, axis=-1, keepdims=True) + NORM_EPS)
                if col < GDN_QK_W:
                    q_ref[0, rt, col:col + GDN_HEAD] = (yh * (r * (GDN_HEAD ** -0.5))).astype(BF16)
                else:
                    k_ref[0, rt, col - GDN_QK_W:col - GDN_QK_W + GDN_HEAD] = (yh * r).astype(BF16)
            else:
                v_ref[0, rt, col - 2 * GDN_QK_W:col - 2 * GDN_QK_W + GDN_HEAD] = yh.astype(BF16)

    for c in range(GDN_V_W // CONV_COLS):
        cs = slice(c * CONV_COLS, (c + 1) * CONV_COLS)
        ws = slice(GDN_CONV_W + c * CONV_COLS, GDN_CONV_W + (c + 1) * CONV_COLS)
        zs_ref[0, rt, cs] = _silu(_dot(hn, win_ref[:, ws])).astype(BF16)

    nh = GDN_V_HEADS
    ba = _dot(hn, wba_ref[...])
    bat = _dot_nt(wbat_ref[...], hn)
    beta_c = 1.0 / (1.0 + jnp.exp(-ba))
    g_c = -jnp.exp(arow_ref[...]) * _softplus(ba + dtrow_ref[...])
    beta_r = 1.0 / (1.0 + jnp.exp(-bat[0:nh, :]))
    g_r = -jnp.exp(acol_ref[...]) * _softplus(bat[nh:2 * nh, :] + dtcol_ref[...])

    gc_c = _dot_exact(lower, g_c)
    gl_c = _dot_exact(block, g_c)
    gc_r = _dot_exact(g_r, upper)
    grp = lax.broadcasted_iota(jnp.int32, (ROW_TILE, LANES), 1) // nh
    gcol_ref[0, rt, :] = jnp.where(grp == 0, beta_c,
                                   jnp.where(grp == 1, gc_c,
                                             jnp.where(grp == 2, jnp.exp(gc_c),
                                                       jnp.where(grp == 3, jnp.exp(gl_c - gc_c), 0.0))))
    npair = nh // 2
    for kind, xr in enumerate((gc_r, beta_r, beta_r * jnp.exp(gc_r))):
        for c in range(ROW_TILE // GDN_CHUNK):
            cs = slice(c * GDN_CHUNK, (c + 1) * GDN_CHUNK)
            grow_ref[0, c0 + c, kind * npair:(kind + 1) * npair, :] = jnp.concatenate(
                [xr[0:npair, cs], xr[npair:nh, cs]], axis=-1)
    for c in range(ROW_TILE // GDN_CHUNK):
        cs = slice(c * GDN_CHUNK, (c + 1) * GDN_CHUNK)
        last = (c + 1) * GDN_CHUNK - 1
        gl = jnp.broadcast_to(gc_r[:, last:last + 1], (nh, LANES))
        egl_ref[0, c0 + c] = jnp.exp(gl)
        to_end = jnp.exp(gl[:, 0:GDN_CHUNK] - gc_r[:, cs])
        grow_ref[0, c0 + c, 3 * npair:4 * npair, :] =jnp.concatenate([to_end[0:npair], to_end[npair:nh]], axis=-1)


def _gdn_prep_kernel(q_ref, k_ref, gcol_ref, grow_ref, tb_ref, w_ref, a_ref, kdt_ref):
    C = GDN_CHUNK
    nh = GDN_V_HEADS
    npair = nh // 2
    row = lax.broadcasted_iota(jnp.int32, (C, LANES), 0)
    lane = lax.broadcasted_iota(jnp.int32, (C, LANES), 1)
    left = lane < C
    col = jnp.where(left, lane, lane - C)
    incl = row >= col
    strict = row > col
    eye = jnp.where(row == col, 1.0, 0.0).astype(F32)
    diag8 = strict & ((row // 8) == (col // 8))
    merges = tuple(strict & ((row // (2 * s)) == (col // (2 * s))) & ((row // s) != (col // s))
                   for s in (8, 16, 32))
    zero_b = jnp.zeros((C, LANES), BF16)

    def blockdiag(x):
        xb = x.astype(BF16)
        return jnp.concatenate([jnp.where(left, xb, zero_b), jnp.where(left, zero_b, xb)], axis=0)

    def pair_cols(g, base, p_):
        return jnp.take_along_axis(g, jnp.where(left, base + 2 * p_, base + 2 * p_ + 1), axis=1)

    cpt = ROW_TILE // C
    streams = [(c, p_) for c in range(cpt) for p_ in range(npair)]
    zero_k = jnp.zeros((C, GDN_HEAD), BF16)

    def row_tile(j, carry):
        rows = lambda c: pl.ds(pl.multiple_of(j * ROW_TILE + c * C, C), C)
        ms, kps, bege, betar = [], [], [], []
        for c, p_ in streams:
            rs, cj = rows(c), j * cpt + c
            ps = slice(p_ * GDN_HEAD, (p_ + 1) * GDN_HEAD)
            qp = q_ref[0, rs, ps]
            kp = k_ref[0, rs, ps]
            g = gcol_ref[0, rs, :]
            both = _dot_nt(jnp.concatenate([qp, kp], axis=0), jnp.concatenate([kp, kp], axis=0))
            qk2, kk2 = both[0:C], both[C:2 * C]
            decay = jnp.exp(jnp.where(incl, pair_cols(g, nh, p_) - grow_ref[0, cj, p_:p_ + 1, :], -jnp.inf))
            ms.append(jnp.where(strict, pair_cols(g, 0, p_) * kk2 * decay, 0.0))
            a_ref[0, rs, ps] = (qk2 * decay).astype(BF16)
            kpf = kp.astype(F32)
            kdt_ref[0, cj, p_] = (jnp.concatenate([kpf, kpf], axis=0).T
                                  * grow_ref[0, cj, 3 * npair + p_:3 * npair + p_ + 1, :]).astype(BF16)
            betar.append(grow_ref[0, cj, npair + p_:npair + p_ + 1, :])
            kps.append(kp)
            bege.append(grow_ref[0, cj, 2 * npair + p_:2 * npair + p_ + 1, :])

        m8 = [jnp.where(diag8, m, 0.0) for m in ms]
        q2 = [_dot(x.astype(BF16), blockdiag(x)) for x in m8]
        pinv = [eye - x for x in m8]
        pinv = [p + _dot(p.astype(BF16), blockdiag(y)) for p, y in zip(pinv, q2)]
        q4 = [_dot(y.astype(BF16), blockdiag(y)) for y in q2]
        pinv = [p + _dot(p.astype(BF16), blockdiag(y)) for p, y in zip(pinv, q4)]
        for mask in merges:
            cp = [_dot(jnp.where(mask, m, 0.0).astype(BF16), blockdiag(p)) for m, p in zip(ms, pinv)]
            pinv = [p - _dot(p.astype(BF16), blockdiag(y)) for p, y in zip(pinv, cp)]

        for (c, p_), p, kp, bg, br in zip(streams, pinv, kps, bege, betar):
            rs = rows(c)
            ps = slice(p_ * GDN_HEAD, (p_ + 1) * GDN_HEAD)
            kbd = jnp.concatenate([jnp.concatenate([kp, zero_k], axis=1),
                                   jnp.concatenate([zero_k, kp], axis=1)], axis=0)
            w_ref[0, rs, 2 * p_ * GDN_HEAD:(2 * p_ + 2) * GDN_HEAD] = _dot((p * bg).astype(BF16), kbd).astype(BF16)
            tb_ref[0, rs, ps] = (p * br).astype(BF16)
        return carry

    lax.fori_loop(0, q_ref.shape[1] // ROW_TILE, row_tile, 0, unroll=True)


def _gdn_scan_kernel(tb_ref, w_ref, a_ref, kdt_ref, q_ref, v_ref, zs_ref, gcol_ref, egl_ref, onorm_ref, o_ref,
                     state_scr):
    @pl.when(pl.program_id(0) == 0)
    def _():
        state_scr[...] = jnp.zeros_like(state_scr)

    C = GDN_CHUNK
    nh = GDN_V_HEADS
    nb = q_ref.shape[0]
    onorm = onorm_ref[...]
    zero_v = jnp.zeros((C, GDN_HEAD), BF16)
    ones_sq = jnp.ones((GDN_HEAD, GDN_HEAD), BF16)

    def blockdiag(x0, x1):
        return jnp.concatenate([jnp.concatenate([x0, zero_v], axis=1),
                                jnp.concatenate([zero_v, x1], axis=1)], axis=0)

    pairs = [(b, p_) for b in range(nb) for p_ in range(nh // 2)]
    heads = [(b, h) for b, p_ in pairs for h in (2 * p_, 2 * p_ + 1)]
    hsl = lambda h: slice(h * GDN_HEAD, (h + 1) * GDN_HEAD)

    def chunk(c, carry):
        rs = pl.ds(pl.multiple_of(c * C, C), C)
        u2 = [_dot(tb_ref[b, rs, hsl(p_)], blockdiag(v_ref[b, rs, hsl(2 * p_)], v_ref[b, rs, hsl(2 * p_ + 1)]))
              for b, p_ in pairs]
        u = {(b, 2 * p_ + j): x[:, j * GDN_HEAD:(j + 1) * GDN_HEAD]
             for (b, p_), x in zip(pairs, u2) for j in (0, 1)}
        s_old = {bh: state_scr[bh[0], bh[1]] for bh in heads}
        wq = {(b, h): _dot(jnp.concatenate([w_ref[b, rs, hsl(h)], q_ref[b, rs, hsl(h // 2)]], axis=0),
                           s_old[(b, h)].astype(BF16)) for b, h in heads}
        vnb = {bh: (u[bh] - wq[bh][0:C]).astype(BF16) for bh in heads}
        od = [_dot(jnp.concatenate([a_ref[b, rs, hsl(p_)], kdt_ref[b, c, p_]], axis=0),
                   blockdiag(vnb[(b, 2 * p_)], vnb[(b, 2 * p_ + 1)])) for b, p_ in pairs]
        os = {}
        for (b, p_), x in zip(pairs, od):
            for j in (0, 1):
                h = 2 * p_ + j
                js = slice(j * GDN_HEAD, (j + 1) * GDN_HEAD)
                erow = (h % 2) * (nh // 2) + h // 2
                state_scr[b, h] = s_old[(b, h)] * egl_ref[b, c, erow:erow + 1, :] + x[C:, js]
                os[(b, h)] = gcol_ref[b, rs, 2 * nh + h:2 * nh + h + 1] * wq[(b, h)][C:2 * C] + x[0:C, js]
        sq = {bh: _dot((o * o).astype(BF16), ones_sq) for bh, o in os.items()}
        for (b, h), o in os.items():
            on = o * lax.rsqrt(sq[(b, h)] * (1.0 / GDN_HEAD) + NORM_EPS) * onorm
            o_ref[b, rs, hsl(h)] = (on * zs_ref[b, rs, hsl(h)].astype(F32)).astype(BF16)
        return carry

    lax.fori_loop(0, q_ref.shape[1] // C, chunk, 0, unroll=2)


def _out_proj_first_kernel(o_ref, w_ref, gain_ref, head_ref, *refs):
    x_refs, h_ref = refs[:-1], refs[-1]
    y = _dot(o_ref[0], w_ref[...])
    yn = y * _rms_scale(y) * gain_ref[...]
    for j, x_ref in enumerate(x_refs):
        res = x_ref[0] if j else jnp.where(pl.program_id(1) == 0, head_ref[...], x_ref[0])
        h_ref[0, j * ROW_TILE:(j + 1) * ROW_TILE, :] = res + yn[j * ROW_TILE:(j + 1) * ROW_TILE]


def _out_proj_last_kernel(o_ref, w_ref, gain_ref, *refs):
    h_refs, out_ref = refs[:-1], refs[-1]
    y = _dot(o_ref[0], w_ref[...])
    yn = y * _rms_scale(y) * gain_ref[...]
    for j, h_ref in enumerate(h_refs):
        out_ref[0, j * ROW_TILE:(j + 1) * ROW_TILE, :] = h_ref[0] + yn[j * ROW_TILE:(j + 1) * ROW_TILE]


def _mla_in_kernel(h_ref, pre_ref, kvn_ref, win_ref, qln_ref, wqn_ref, wqr_ref, wqrr_ref,
                   wkd_ref, kvln_ref, wku_ref, wvt_ref, cos_ref, sin_ref,
                   q_ref, kk_ref, vt_ref, zs_ref):
    h = h_ref[0]
    hr = h * _rms_scale(h)
    hn = (hr * pre_ref[...]).astype(BF16)
    hk = (hr * kvn_ref[...]).astype(BF16)
    cosp = cos_ref[...]
    sinp = sin_ref[...]

    for c in range((win_ref.shape[1] - MLA_Q_RANK) // CONV_COLS):
        cs = slice(c * CONV_COLS, (c + 1) * CONV_COLS)
        ws = slice(MLA_Q_RANK + c * CONV_COLS, MLA_Q_RANK + (c + 1) * CONV_COLS)
        zs_ref[0, :, cs] = _silu(_dot(hn, win_ref[:, ws])).astype(BF16)

    cq = _dot(hn, win_ref[:, 0:MLA_Q_RANK])
    cq = (cq * _rms_scale(cq) * qln_ref[...]).astype(BF16)
    scale = MLA_QK ** -0.5 * math.log2(math.e)
    for hd in range(MLA_HEADS):
        hs = slice(hd * LANES, (hd + 1) * LANES)
        q_ref[0, hd, :, 0:MLA_NOPE] = (_dot(cq, wqn_ref[:, hs]) * scale).astype(BF16)
    per_tile = LANES // MLA_ROPE
    for t in range(MLA_HEADS // per_tile):
        ts = slice(t * LANES, (t + 1) * LANES)
        qr = ((_dot(cq, wqr_ref[:, ts]) * cosp + _dot(cq, wqrr_ref[:, ts]) * sinp) * scale).astype(BF16)
        for u in range(per_tile):
            q_ref[0, t * per_tile + u, :, MLA_NOPE:MLA_QK] = qr[:, u * MLA_ROPE:(u + 1) * MLA_ROPE]

    ckr = _dot(hk, wkd_ref[...])
    lat = ckr[:, 0:MLA_KV_RANK]
    ckv = (lat * _rms_scale(lat) * kvln_ref[...]).astype(BF16)
    kr = ckr[:, LANES:2 * LANES] * cosp + ckr[:, 2 * LANES:3 * LANES] * sinp
    krb = kr[:, 0:MLA_ROPE].astype(BF16)
    ones_rows = jnp.ones((ATTN_ONES_ROWS, ROW_TILE), BF16)
    for hd in range(MLA_HEADS):
        kn = _dot(ckv, wku_ref[:, hd * MLA_NOPE:(hd + 1) * MLA_NOPE])
        kk_ref[0, hd, :, 0:MLA_NOPE] = kn.astype(BF16)
        kk_ref[0, hd, :, MLA_NOPE:MLA_QK] = krb
        vt = _dot_nt(wvt_ref[hd * MLA_V:(hd + 1) * MLA_V, :], ckv)
        vt_ref[0, hd, 0, 0:MLA_V, :] = vt.astype(BF16)
        vt_ref[0, hd, 0, MLA_V:MLA_V + ATTN_ONES_ROWS, :] = ones_rows


def _attn_kernel(q_ref, k_ref, vt_ref, zs_ref, o_ref, m_scr, al_scr, sa_scr, sb_scr, ma_scr, mb_scr, p_scr):
    i = pl.program_id(2)
    tq, sb, tk = ATTN_TQ, ATTN_SUB, ATTN_TK
    m_scr[...] = jnp.full_like(m_scr, -jnp.inf)
    al_scr[...] = jnp.zeros_like(al_scr)

    def rows(r):
        return slice(r * sb, (r + 1) * sb)

    def scores(r, k0, nk):
        return _dot_nt(k_ref[0, 0, pl.ds(k0, nk), :], q_ref[0, 0, rows(r), :])

    def weighted_values(p, k0, nk):
        if nk < ROW_TILE:
            return _dot(vt_ref[0, 0, 0, :, k0:k0 + nk], p[...])
        kb = k0 // ROW_TILE
        out = _dot(vt_ref[0, 0, kb], p[0:ROW_TILE, :])
        for jb in range(1, nk // ROW_TILE):
            out = out + _dot(vt_ref[0, 0, kb + jb], p[jb * ROW_TILE:(jb + 1) * ROW_TILE, :])
        return out

    def absorb(r, parts):
        m_old = m_scr[r]
        m_new = m_old
        for s, _, _ in parts:
            m_new = jnp.maximum(m_new, jnp.max(s, axis=0, keepdims=True))
        al = jnp.exp2(m_old - m_new)[0:1, :] * al_scr[r]
        for s, k0, nk in parts:
            al = al + weighted_values(jnp.exp2(s - m_new[0:1, :]).astype(BF16), k0, nk)
        al_scr[r] = al
        m_scr[r] = m_new

    def stage_block(r, bufs, k0):
        s_buf, mx_buf = bufs
        s = scores(r, k0, tk)
        s_buf[r] = s
        mx_buf[r] = jnp.broadcast_to(jnp.max(s, axis=0, keepdims=True), mx_buf.shape[1:])

    def absorb_block(r, bufs, k0):
        s_buf, mx_buf = bufs
        m_old = m_scr[r]
        m_new = jnp.maximum(m_old, mx_buf[r])
        for g in range(tk // ATTN_GROUP):
            gr = slice(g * ATTN_GROUP, (g + 1) * ATTN_GROUP)
            p_scr[r, gr, :] = jnp.exp2(s_buf[r, gr, :] - m_new[0:1, :]).astype(BF16)
        al_scr[r] = jnp.exp2(m_old - m_new)[0:1, :] * al_scr[r] + weighted_values(p_scr.at[r], k0, tk)
        m_scr[r] = m_new

    nsub = tq // sb
    block0 = lambda j: pl.multiple_of(ROW_TILE + j * tk, ROW_TILE)

    ki = lax.broadcasted_iota(jnp.int32, (sb, sb), 0)
    qi = lax.broadcasted_iota(jnp.int32, (sb, sb), 1)
    base = pl.multiple_of(ROW_TILE + i * tq, ROW_TILE)
    meta0 = ROW_TILE - LANES
    is_meta = lax.broadcasted_iota(jnp.int32, (LANES, sb), 0) >= FRONT_PAD - meta0

    def mask_diag(s):
        tri = jnp.where(ki <= qi, s[-sb:, :], -jnp.inf)
        return tri if s.shape[0] == sb else jnp.concatenate([s[:-sb, :], tri], axis=0)

    base2 = pl.multiple_of(base + tk, ROW_TILE)

    def diag_scores_a(r):
        w = (r + 1) * sb
        if w <= tk:
            sa_scr[r, 0:w, :] = scores(r, base, w)
        else:
            sa_scr[r, 0:w - tk, :] = scores(r, base2, w - tk)

    def diag_scores_b(r):
        if (r + 1) * sb > tk:
            sb_scr[r] = scores(r, base, tk)

    assert tq == 2 * tk
    trips = i

    @pl.when(trips == 0)
    def _():
        for r in range(nsub):
            diag_scores_a(r)
            diag_scores_b(r)

    @pl.when(trips > 0)
    def _():
        buf_a, buf_b = (sa_scr, ma_scr), (sb_scr, mb_scr)
        for r in range(nsub):
            stage_block(r, buf_a, block0(0))

        def body(jj, carry):
            j = 2 * jj
            for cur, nxt, step in ((buf_a, buf_b, 0), (buf_b, buf_a, 1)):
                for r in range(nsub):
                    stage_block(r, nxt, block0(j + step + 1))
                    absorb_block(r, cur, block0(j + step))
            return carry

        lax.fori_loop(0, trips - 1, body, 0)
        j = 2 * (trips - 1)
        for r in range(nsub):
            stage_block(r, buf_b, block0(j + 1))
            absorb_block(r, buf_a, block0(j))
        for r in range(nsub):
            diag_scores_a(r)
            absorb_block(r, buf_b, block0(j + 1))
            diag_scores_b(r)

    s_meta = [jnp.where(is_meta, scores(r, meta0, LANES), -jnp.inf) for r in range(nsub)]
    for r in range(nsub):
        w = (r + 1) * sb
        if w <= tk:
            absorb(r, [(s_meta[r], meta0, LANES), (mask_diag(sa_scr[r, 0:w, :]), base, w)])
        else:
            absorb(r, [(s_meta[r], meta0, LANES), (sb_scr[r], base, tk),
                       (mask_diag(sa_scr[r, 0:w - tk, :]), base2, w - tk)])

    for r in range(nsub):
        o_t = al_scr[r, 0:MLA_V, :] / al_scr[r, MLA_V:MLA_V + 1, :]
        o_ref[0, rows(r), :] = (o_t.T * zs_ref[0, rows(r), :].astype(F32)).astype(BF16)


def _params(*sem):
    return pltpu.CompilerParams(dimension_semantics=sem, vmem_limit_bytes=VMEM_LIMIT)


def kernel(x, meta_tokens, pre_norm, post_norm, gdn_w_in, gdn_conv_w, gdn_a_log, gdn_dt_bias, gdn_out_norm, gdn_w_out, kv_norm, kv_w_down, kv_latent_norm, kv_w_up, mla_w_in, mla_q_latent_norm, mla_w_q_up, mla_w_out):
    B, S, D = x.shape
    assert S % ATTN_TQ == 0 and S % (OUT_TILES * ROW_TILE) == 0 and (S // ROW_TILE + 1) % WIDE_TILES == 0
    assert gdn_w_in.shape[0] == 1 and mla_w_in.shape[0] == 1
    T = ROW_TILE
    nt = S // T + 1
    Lp = nt * T
    nch = Lp // GDN_CHUNK
    nh = GDN_V_HEADS

    head = jnp.concatenate([jnp.zeros((FRONT_PAD, D), F32), meta_tokens.astype(F32)], axis=0)
    row = lambda a: a.reshape(1, -1).astype(F32)

    w_in = gdn_w_in[0]
    w_ba_f = w_in[:, GDN_CONV_W + GDN_V_W:]
    w_b, w_a = w_ba_f[:, :nh], w_ba_f[:, nh:]
    w_ba = jnp.concatenate([w_b, w_a, w_a, w_a, jnp.zeros((D, LANES - 4 * nh), F32)], axis=-1).astype(BF16)
    eo = lambda a: jnp.concatenate([a[0::2], a[1::2]], axis=0)
    w_bat = jnp.concatenate([eo(w_b.T), eo(w_a.T)], axis=0).astype(BF16)
    a_log = gdn_a_log[0].astype(F32)
    dt_b = gdn_dt_bias[0].astype(F32)
    lane_groups = lambda a: jnp.concatenate([jnp.zeros((nh,), F32), a, a, a,
                                             jnp.zeros((LANES - 4 * nh,), F32)]).reshape(1, LANES)

    tile_spec = lambda w: pl.BlockSpec((1, T, w), lambda b, i: (b, i, 0))
    wide_spec = lambda w: pl.BlockSpec((1, WIDE_TILES * T, w), lambda b, i: (b, i, 0))

    x_specs = [pl.BlockSpec((1, T, D), lambda b, i, j=j: (b, jnp.maximum(WIDE_TILES * i + j - 1, 0), 0))
               for j in range(WIDE_TILES)]
    gate_spec = lambda r: pl.BlockSpec((1, WIDE_TILES * T // GDN_CHUNK, r, LANES), lambda b, i: (b, i, 0, 0))
    q, k, v, zs, gcol, grow, egl = pl.pallas_call(
        _gdn_in_kernel,
        grid=(B, nt // WIDE_TILES),
        in_specs=[_const_spec((T, D)), _const_spec((1, D)),
                  _const_spec(w_in.shape), _const_spec((D, LANES)), _const_spec((2 * nh, D)),
                  _const_spec((GDN_CONV_TAPS, GDN_CONV_W)),
                  _const_spec((1, LANES)), _const_spec((1, LANES)), _const_spec((nh, 1)), _const_spec((nh, 1))]
        + x_specs,
        out_specs=[wide_spec(GDN_QK_W), wide_spec(GDN_QK_W), wide_spec(GDN_V_W), wide_spec(GDN_V_W),
                   wide_spec(LANES), gate_spec(2 * nh), gate_spec(nh)],
        out_shape=[jax.ShapeDtypeStruct((B, Lp, GDN_QK_W), BF16),
                   jax.ShapeDtypeStruct((B, Lp, GDN_QK_W), BF16),
                   jax.ShapeDtypeStruct((B, Lp, GDN_V_W), BF16),
                   jax.ShapeDtypeStruct((B, Lp, GDN_V_W), BF16),
                   jax.ShapeDtypeStruct((B, Lp, LANES), F32),
                   jax.ShapeDtypeStruct((B, nch, 2 * nh, LANES), F32),
                   jax.ShapeDtypeStruct((B, nch, nh, LANES), F32)],
        scratch_shapes=[pltpu.VMEM((HALO_ROWS, GDN_CONV_W), F32),
                        pltpu.VMEM((CONV_COLS // GDN_HEAD, HALO_ROWS + T, GDN_HEAD), F32)],
        compiler_params=_params("arbitrary", "arbitrary"),
        name="gdn_in",
    )(head, row(pre_norm[0]), w_in.astype(BF16), w_ba, w_bat, gdn_conv_w[0].astype(F32),
      lane_groups(a_log), lane_groups(dt_b), eo(a_log).reshape(nh, 1), eo(dt_b).reshape(nh, 1),
      *([x] * WIDE_TILES))

    npair = nh // 2
    tb, wk, aqk, kdt = pl.pallas_call(
        _gdn_prep_kernel,
        grid=(B, nt // WIDE_TILES),
        in_specs=[wide_spec(GDN_QK_W), wide_spec(GDN_QK_W), wide_spec(LANES),
                  pl.BlockSpec((1, WIDE_TILES * T // GDN_CHUNK, 2 * nh, LANES), lambda b, i: (b, i, 0, 0))],
        out_specs=[wide_spec(GDN_QK_W), wide_spec(GDN_V_W), wide_spec(GDN_QK_W),
                   pl.BlockSpec((1, WIDE_TILES * T // GDN_CHUNK, npair, GDN_HEAD, LANES),
                                lambda b, i: (b, i, 0, 0, 0))],
        out_shape=[jax.ShapeDtypeStruct((B, Lp, GDN_QK_W), BF16),
                   jax.ShapeDtypeStruct((B, Lp, GDN_V_W), BF16),
                   jax.ShapeDtypeStruct((B, Lp, GDN_QK_W), BF16),
                   jax.ShapeDtypeStruct((B, nch, npair, GDN_HEAD, LANES), BF16)],
        compiler_params=_params("parallel", "parallel"),
        name="gdn_prep",
    )(q, k, gcol, grow)

    cpt = T // GDN_CHUNK
    chunk_spec = lambda w: pl.BlockSpec((B, T, w), lambda n: (0, n, 0))
    o_gdn = pl.pallas_call(
        _gdn_scan_kernel,
        grid=(nt,),
        in_specs=[chunk_spec(GDN_QK_W), chunk_spec(GDN_V_W), chunk_spec(GDN_QK_W),
                  pl.BlockSpec((B, cpt, npair, GDN_HEAD, LANES), lambda n: (0, n, 0, 0, 0)),
                  chunk_spec(GDN_QK_W), chunk_spec(GDN_V_W), chunk_spec(GDN_V_W), chunk_spec(LANES),
                  pl.BlockSpec((B, cpt, nh, LANES), lambda n: (0, n, 0, 0)),
                  _const_spec((1, GDN_HEAD))],
        out_specs=chunk_spec(GDN_V_W),
        out_shape=jax.ShapeDtypeStruct((B, Lp, GDN_V_W), BF16),
        scratch_shapes=[pltpu.VMEM((B, nh, GDN_HEAD, GDN_HEAD), F32)],
        compiler_params=_params("arbitrary"),
        name="gdn_scan",
    )(tb, wk, aqk, kdt, q, v, zs, gcol, egl, row(gdn_out_norm[0]))

    h1 = pl.pallas_call(
        _out_proj_first_kernel,
        grid=(B, nt // WIDE_TILES),
        in_specs=[wide_spec(GDN_V_W), _const_spec((GDN_V_W, D)), _const_spec((1, D)), _const_spec((T, D))]
        + x_specs,
        out_specs=wide_spec(D),
        out_shape=jax.ShapeDtypeStruct((B, Lp, D), F32),
        compiler_params=_params("parallel", "arbitrary"),
        name="gdn_out",
    )(o_gdn, gdn_w_out[0].astype(BF16), row(post_norm[0]), head, *([x] * WIDE_TILES))

    w_in1 = mla_w_in[0]
    zw = w_in1.shape[1] - MLA_Q_RANK
    half = MLA_ROPE // 2
    rot = lambda w: jnp.concatenate([-w[..., half:], w[..., :half]], axis=-1)
    lane_pad = lambda w: jnp.pad(w, [(0, 0)] * (w.ndim - 1) + [(0, LANES - w.shape[-1])])
    wq = mla_w_q_up[0].reshape(MLA_Q_RANK, MLA_HEADS, MLA_QK)
    w_qn = wq[..., :MLA_NOPE].reshape(MLA_Q_RANK, MLA_HEADS * MLA_NOPE).astype(BF16)
    w_qr = wq[..., MLA_NOPE:].reshape(MLA_Q_RANK, MLA_HEADS * MLA_ROPE).astype(BF16)
    w_qrr = rot(wq[..., MLA_NOPE:]).reshape(MLA_Q_RANK, MLA_HEADS * MLA_ROPE).astype(BF16)
    wkd_r = kv_w_down[:, MLA_KV_RANK:]
    w_kd = jnp.concatenate([kv_w_down[:, :MLA_KV_RANK], lane_pad(wkd_r), lane_pad(rot(wkd_r))],
                           axis=-1).astype(BF16)
    wku = kv_w_up.reshape(MLA_KV_RANK, MLA_HEADS, MLA_NOPE + MLA_V)
    w_ku = wku[..., :MLA_NOPE].reshape(MLA_KV_RANK, MLA_HEADS * MLA_NOPE).astype(BF16)
    w_vt = jnp.transpose(wku[..., MLA_NOPE:], (1, 2, 0)).reshape(MLA_HEADS * MLA_V, MLA_KV_RANK).astype(BF16)
    vt_rows = MLA_V + ATTN_ONES_ROWS

    inv = ROPE_THETA ** (-jnp.arange(0, MLA_ROPE, 2, dtype=F32) / MLA_ROPE)
    pos = (jnp.arange(Lp, dtype=jnp.int32) - FRONT_PAD).astype(F32)
    ang = pos[:, None] * inv[None, :]
    cosp = jnp.tile(jnp.cos(ang), (1, LANES // half))
    sinp = jnp.tile(jnp.sin(ang), (1, LANES // half))

    head_tile = lambda w: pl.BlockSpec((1, MLA_HEADS, T, w), lambda b, i: (b, 0, i, 0))
    q1, k1, v1, zs1 = pl.pallas_call(
        _mla_in_kernel,
        grid=(B, nt),
        in_specs=[tile_spec(D), _const_spec((1, D)), _const_spec((1, D)),
                  _const_spec(w_in1.shape), _const_spec((1, MLA_Q_RANK)),
                  _const_spec((MLA_Q_RANK, MLA_HEADS * MLA_NOPE)),
                  _const_spec((MLA_Q_RANK, MLA_HEADS * MLA_ROPE)),
                  _const_spec((MLA_Q_RANK, MLA_HEADS * MLA_ROPE)),
                  _const_spec((D, 3 * LANES)), _const_spec((1, MLA_KV_RANK)),
                  _const_spec((MLA_KV_RANK, MLA_HEADS * MLA_NOPE)), _const_spec((MLA_HEADS * MLA_V, MLA_KV_RANK)),
                  pl.BlockSpec((T, LANES), lambda b, i: (i, 0)),
                  pl.BlockSpec((T, LANES), lambda b, i: (i, 0))],
        out_specs=[pl.BlockSpec((1, MLA_HEADS, T, MLA_QK), lambda b, i: (b, 0, jnp.maximum(i - 1, 0), 0)),
                   head_tile(MLA_QK),
                   pl.BlockSpec((1, MLA_HEADS, 1, vt_rows, T), lambda b, i: (b, 0, i, 0, 0)),
                   pl.BlockSpec((1, T, zw), lambda b, i: (b, jnp.maximum(i - 1, 0), 0))],
        out_shape=[jax.ShapeDtypeStruct((B, MLA_HEADS, S, MLA_QK), BF16),
                   jax.ShapeDtypeStruct((B, MLA_HEADS, Lp, MLA_QK), BF16),
                   jax.ShapeDtypeStruct((B, MLA_HEADS, nt, vt_rows, T), BF16),
                   jax.ShapeDtypeStruct((B, S, zw), BF16)],
        compiler_params=_params("arbitrary", "arbitrary"),
        name="mla_in",
    )(h1, row(pre_norm[1]), row(kv_norm), w_in1.astype(BF16), row(mla_q_latent_norm[0]), w_qn, w_qr, w_qrr,
      w_kd, row(kv_latent_norm), w_ku, w_vt, cosp, sinp)

    o_attn = pl.pallas_call(
        _attn_kernel,
        grid=(B, MLA_HEADS, S // ATTN_TQ),
        in_specs=[pl.BlockSpec((1, 1, ATTN_TQ, MLA_QK), lambda b, h, i: (b, h, i, 0)),
                  pl.BlockSpec((1, 1, Lp, MLA_QK), lambda b, h, i: (b, h, 0, 0)),
                  pl.BlockSpec((1, 1, nt, vt_rows, T), lambda b, h, i: (b, h, 0, 0, 0)),
                  pl.BlockSpec((1, ATTN_TQ, MLA_V), lambda b, h, i: (b, i, h))],
        out_specs=pl.BlockSpec((1, ATTN_TQ, MLA_V), lambda b, h, i: (b, i, h)),
        out_shape=jax.ShapeDtypeStruct((B, S, MLA_HEADS * MLA_V), BF16),
        scratch_shapes=[pltpu.VMEM((ATTN_TQ // ATTN_SUB, 8, ATTN_SUB), F32),
                        pltpu.VMEM((ATTN_TQ // ATTN_SUB, vt_rows, ATTN_SUB), F32),
                        pltpu.VMEM((ATTN_TQ // ATTN_SUB, ATTN_TK, ATTN_SUB), F32),
                        pltpu.VMEM((ATTN_TQ // ATTN_SUB, ATTN_TK, ATTN_SUB), F32),
                        pltpu.VMEM((ATTN_TQ // ATTN_SUB, 8, ATTN_SUB), F32),
                        pltpu.VMEM((ATTN_TQ // ATTN_SUB, 8, ATTN_SUB), F32),
                        pltpu.VMEM((ATTN_TQ // ATTN_SUB, ATTN_TK, ATTN_SUB), BF16)],
        compiler_params=_params("parallel", "parallel", "arbitrary"),
        name="mla_attn",
    )(q1, k1, v1, zs1)

    out = pl.pallas_call(
        _out_proj_last_kernel,
        grid=(B, S // (OUT_TILES * T)),
        in_specs=[pl.BlockSpec((1, OUT_TILES * T, MLA_HEADS * MLA_V), lambda b, i: (b, i, 0)),
                  _const_spec((MLA_HEADS * MLA_V, D)), _const_spec((1, D))]
        + [pl.BlockSpec((1, T, D), lambda b, i, j=j: (b, OUT_TILES * i + j + 1, 0)) for j in range(OUT_TILES)],
        out_specs=pl.BlockSpec((1, OUT_TILES * T, D), lambda b, i: (b, i, 0)),
        out_shape=jax.ShapeDtypeStruct((B, S, D), x.dtype),
        compiler_params=_params("parallel", "arbitrary"),
        name="mla_out",
    )(o_attn, mla_w_out[0].astype(BF16), row(post_norm[1]), *([h1] * OUT_TILES))
    return out
```
